```python
import functools
import jax
import jax.numpy as jnp
from jax import lax
import numpy as np

D_MODEL = 2048
BATCH = 16
SEQ = 256
DEPTH = 4
DEC_BATCH = 4
DEC_SEQ = 1024
PAST_LEN = 256

GRID_W = 64
MIX_WIDTH = D_MODEL
ATT_WIDTH = MIX_WIDTH // 2
HEAD_DIM = 128
ATT_HEADS = ATT_WIDTH // HEAD_DIM
ATT_KV_HEADS = 2
Q_PER_KV = ATT_HEADS // ATT_KV_HEADS
WINDOW = 128
ATT_BLOCK = 128
ROPE_BASE = 10000.0
HG_WIDTH = MIX_WIDTH - ATT_WIDTH
HG_HEADS = 8
HG_KEY = HG_WIDTH // HG_HEADS
HG_VAL = HG_WIDTH // HG_HEADS
HG_CHUNK = 32
D_FF = 5632
N_EXPERTS = 8
TOP_K = 2
D_FF_EXPERT = 2816
N_DENSE = (DEPTH + 1) // 2
N_MOE = DEPTH // 2
NORM_EPS = 1e-6
Q_COLS = ATT_HEADS * HEAD_DIM
KV_COLS = ATT_KV_HEADS * HEAD_DIM
IN_SPLITS = (Q_COLS, Q_COLS + KV_COLS, Q_COLS + 2 * KV_COLS, Q_COLS + 2 * KV_COLS + HG_WIDTH,
             Q_COLS + 2 * KV_COLS + 3 * HG_WIDTH, Q_COLS + 2 * KV_COLS + 4 * HG_WIDTH)
IN_COLS = Q_COLS + 2 * KV_COLS + 5 * HG_WIDTH

kernel_name = 'hybrid_diffusion_swa_hgrn2_moe_step'


def _rmsnorm(x, g):
    xf = x.astype(jnp.float32)
    y = xf * lax.rsqrt(jnp.mean(xf * xf, axis=-1, keepdims=True) + NORM_EPS)
    return (y * g.astype(jnp.float32)).astype(x.dtype)


def _axial_rope(n_tok):
    rows = n_tok // GRID_W
    row = jnp.repeat(jnp.arange(rows), GRID_W).astype(jnp.float32)
    col = jnp.tile(jnp.arange(GRID_W), rows).astype(jnp.float32)
    axis_dim = HEAD_DIM // 2
    inv = ROPE_BASE ** (-jnp.arange(0, axis_dim, 2, dtype=jnp.float32) / axis_dim)
    ang = jnp.stack([row[:, None] * inv, col[:, None] * inv], axis=1)
    return jnp.cos(ang), jnp.sin(ang)


def _apply_rope(x, cos, sin):
    B, T, H, _ = x.shape
    xf = x.astype(jnp.float32).reshape(B, T, H, 2, 2, HEAD_DIM // 4)
    x1, x2 = xf[..., 0, :], xf[..., 1, :]
    c = cos[None, :, None]
    s = sin[None, :, None]
    out = jnp.stack([x1 * c - x2 * s, x2 * c + x1 * s], axis=-2)
    return out.reshape(B, T, H, HEAD_DIM).astype(x.dtype)


def _sink_softmax(s, sink):
    sk = sink.astype(jnp.float32).reshape(1, ATT_KV_HEADS, Q_PER_KV, 1, 1)
    m = jnp.maximum(jnp.max(s, axis=-1, keepdims=True), sk)
    e = jnp.exp(s - m)
    return e / (jnp.sum(e, axis=-1, keepdims=True) + jnp.exp(sk - m))


def _context_attention(q, k, v, sink):
    B, L = q.shape[:2]
    nb = L // ATT_BLOCK
    scale = HEAD_DIM ** -0.5
    qb = q.reshape(B, nb, ATT_BLOCK, ATT_KV_HEADS, Q_PER_KV, HEAD_DIM).swapaxes(0, 1)

    def one_block(qi):
        s = jnp.einsum('bqgrd,bkgd->bgrqk', qi, k).astype(jnp.float32) * scale
        p = _sink_softmax(s, sink).astype(v.dtype)
        return jnp.einsum('bgrqk,bkgd->bqgrd', p, v)

    o = lax.map(one_block, qb)
    return o.swapaxes(0, 1).reshape(B, L, ATT_WIDTH)


def _latent_attention(q, k, v, k_ctx, v_ctx, sink):
    B, T = q.shape[:2]
    nb = T // ATT_BLOCK
    band = 3 * ATT_BLOCK
    scale = HEAD_DIM ** -0.5

    def bands(a):
        ap = jnp.pad(a, ((0, 0), (ATT_BLOCK, ATT_BLOCK), (0, 0), (0, 0)))
        ap = ap.reshape(B, nb + 2, ATT_BLOCK, ATT_KV_HEADS, HEAD_DIM)
        return jnp.concatenate([ap[:, :-2], ap[:, 1:-1], ap[:, 2:]], axis=2).swapaxes(0, 1)

    kb, vb = bands(k), bands(v)
    blk_start = jnp.arange(nb)[:, None, None] * ATT_BLOCK
    qpos = blk_start + jnp.arange(ATT_BLOCK)[None, :, None]
    kpos = blk_start - ATT_BLOCK + jnp.arange(band)[None, None, :]
    mask = (jnp.abs(kpos - qpos) <= WINDOW) & (kpos >= 0) & (kpos < T)
    qb = q.reshape(B, nb, ATT_BLOCK, ATT_KV_HEADS, Q_PER_KV, HEAD_DIM).swapaxes(0, 1)

    def one_block(xs):
        qi, ki, vi, mi = xs
        s_loc = jnp.einsum('bqgrd,bkgd->bgrqk', qi, ki).astype(jnp.float32) * scale
        s_loc = jnp.where(mi, s_loc, -jnp.inf)
        s_ctx = jnp.einsum('bqgrd,bcgd->bgrqc', qi, k_ctx).astype(jnp.float32) * scale
        p = _sink_softmax(jnp.concatenate([s_loc, s_ctx], axis=-1), sink).astype(v.dtype)
        return (jnp.einsum('bgrqk,bkgd->bqgrd', p[..., :band], vi)
                + jnp.einsum('bgrqc,bcgd->bqgrd', p[..., band:], v_ctx))

    o = lax.map(one_block, (qb, kb, vb, mask))
    return o.swapaxes(0, 1).reshape(B, T, ATT_WIDTH)


def _chunk_scan(q, logf, k, v, s0):
    B, T, H, _ = q.shape
    n = T // HG_CHUNK

    def to_chunks(a):
        return a.reshape(B, n, HG_CHUNK, H, a.shape[-1]).transpose(1, 0, 3, 2, 4)

    causal = jnp.tril(jnp.ones((HG_CHUNK, HG_CHUNK), dtype=bool))

    def step(S, xs):
        qi, fi, ki, vi = xs
        b = jnp.cumsum(fi, axis=2)
        b_last = b[:, :, -1:, :]
        o_inter = jnp.einsum('bhck,bhkv->bhcv', qi * jnp.exp(b), S)
        diff = b[:, :, :, None, :] - b[:, :, None, :, :]
        decay = jnp.exp(jnp.where(causal[:, :, None], diff, -jnp.inf))
        a = jnp.einsum('bhtk,bhtsk,bhsk->bhts', qi, decay, ki)
        o_intra = jnp.einsum('bhts,bhsv->bhtv', a, vi)
        S_new = (S * jnp.exp(b_last)[:, :, 0, :, None]
                 + jnp.einsum('bhsk,bhsv->bhkv', ki * jnp.exp(b_last - b), vi))
        return S_new, o_inter + o_intra

    s_fin, o = lax.scan(step, s0, (to_chunks(q), to_chunks(logf), to_chunks(k), to_chunks(v)))
    return o.transpose(1, 0, 3, 2, 4).reshape(B, T, H, v.shape[-1]), s_fin


def _hgrn_mixer(hq, hf, hi, hg, lb, s_fwd, s_bwd, norm_g):
    B, T = hq.shape[:2]
    f32 = jnp.float32
    q = jax.nn.silu(hq.astype(f32)).reshape(B, T, HG_HEADS, HG_KEY)
    z = hf.astype(f32).reshape(B, T, 2, HG_HEADS, HG_KEY)
    lbr = lb.reshape(2, HG_HEADS, HG_KEY)
    logf = jnp.logaddexp(jnp.log(lbr), jnp.log1p(-lbr) + jax.nn.log_sigmoid(z))
    kk = -jnp.expm1(logf)
    v = hi.astype(f32).reshape(B, T, HG_HEADS, HG_VAL)
    flip = lambda a: jnp.flip(a, axis=1)
    o_f, s_f = _chunk_scan(q, logf[:, :, 0], kk[:, :, 0], v, s_fwd.astype(f32))
    o_b, s_b = _chunk_scan(flip(q), flip(logf[:, :, 1]), flip(kk[:, :, 1]), flip(v), s_bwd.astype(f32))
    o = o_f + flip(o_b)
    o = o * lax.rsqrt(jnp.mean(o * o, axis=-1, keepdims=True) + NORM_EPS) * norm_g.astype(f32)
    o = o.reshape(B, T, HG_WIDTH) * jax.nn.silu(hg.astype(f32))
    return o.astype(hq.dtype), s_f.astype(hq.dtype), s_b.astype(hq.dtype)


def _token_mix(h, w_in_l, w_out_l, sink_l, lb_l, hg_g_l, ctx, rope):
    B, T = h.shape[:2]
    p = h @ w_in_l
    q, k, v, hq, hf, hi, hg = jnp.split(p, IN_SPLITS, axis=-1)
    q = q.reshape(B, T, ATT_HEADS, HEAD_DIM)
    k = k.reshape(B, T, ATT_KV_HEADS, HEAD_DIM)
    v = v.reshape(B, T, ATT_KV_HEADS, HEAD_DIM)
    if ctx is None:
        a = _context_attention(q, k, v, sink_l)
        zero = jnp.zeros((B, HG_HEADS, HG_KEY, HG_VAL), jnp.float32)
        g, s_f, s_b = _hgrn_mixer(hq, hf, hi, hg, lb_l, zero, zero, hg_g_l)
        side = (k, v, s_f, s_b)
    else:
        k_ctx, v_ctx, s0_f, s0_b = ctx
        cos, sin = rope
        a = _latent_attention(_apply_rope(q, cos, sin), _apply_rope(k, cos, sin), v, k_ctx, v_ctx, sink_l)
        g, _, _ = _hgrn_mixer(hq, hf, hi, hg, lb_l, s0_f, s0_b, hg_g_l)
        side = None
    return jnp.concatenate([a, g], axis=-1) @ w_out_l, side


def _swiglu(h, w1, w3, w2):
    return (jax.nn.silu(h @ w1) * (h @ w3)) @ w2


def _moe(h, router, w1, w3, w2):
    shp = h.shape
    x = h.reshape(-1, shp[-1])
    logits = (x @ router).astype(jnp.float32)
    top_v, top_i = lax.top_k(logits, TOP_K)
    gate_k = jax.nn.softmax(top_v, axis=-1)
    gates = jnp.sum(jax.nn.one_hot(top_i, N_EXPERTS, dtype=jnp.float32) * gate_k[..., None], axis=1)
    y = jnp.zeros(x.shape, jnp.float32)
    for e in range(N_EXPERTS):
        y = y + gates[:, e:e + 1] * _swiglu(x, w1[e], w3[e], w2[e]).astype(jnp.float32)
    return y.astype(h.dtype).reshape(shp)


def _layer(x, mod, ctx, rope, norm_g, w_in_l, w_out_l, sink_l, lb_l, hg_g_l, ffn):
    sh1, sc1, g1, sh2, sc2, g2 = jnp.split(mod.astype(x.dtype), 6, axis=-1)
    h = _rmsnorm(x, norm_g[0]) * (1 + sc1) + sh1
    m, side = _token_mix(h, w_in_l, w_out_l, sink_l, lb_l, hg_g_l, ctx, rope)
    x = x + g1 * _rmsnorm(m, norm_g[1])
    h = _rmsnorm(x, norm_g[2]) * (1 + sc2) + sh2
    x = x + g2 * _rmsnorm(ffn(h), norm_g[3])
    return x, side


def setup_inputs(seed: int = 0) -> dict:
    key = jax.random.key(seed)
    ks = jax.random.split(key, 24)
    n = lambda k, shape, s: jax.random.normal(k, shape, jnp.float32) * s
    return {
        'x_prompt': n(ks[0], (BATCH, SEQ, D_MODEL), 1.0),
        'x_sample': n(ks[1], (DEC_BATCH, DEC_SEQ, D_MODEL), 1.0),
        'cache_k': n(ks[2], (DEC_BATCH, DEPTH, PAST_LEN, ATT_KV_HEADS, HEAD_DIM), 1.0),
        'cache_v': n(ks[3], (DEC_BATCH, DEPTH, PAST_LEN, ATT_KV_HEADS, HEAD_DIM), 1.0),
        'state_hgrn': n(ks[4], (DEC_BATCH, DEPTH, 2, HG_HEADS, HG_KEY, HG_VAL), 0.5),
        'c': n(ks[5], (DEC_BATCH, D_MODEL), 1.0),
        'c_ctx': n(ks[6], (D_MODEL,), 1.0),
        'w_mod': n(ks[7], (DEPTH, D_MODEL, 6 * D_MODEL), 0.5 * D_MODEL ** -0.5),
        'b_mod': n(ks[8], (DEPTH, 6 * D_MODEL), 0.02),
        'norm_gains': 1.0 + n(ks[9], (DEPTH, 4, D_MODEL), 0.05),
        'w_in': n(ks[10], (DEPTH, D_MODEL, IN_COLS), D_MODEL ** -0.5),
        'w_out': n(ks[11], (DEPTH, MIX_WIDTH, D_MODEL), MIX_WIDTH ** -0.5),
        'attn_sink': n(ks[12], (DEPTH, ATT_HEADS), 0.5),
        'hg_lb_logits': n(ks[13], (DEPTH, 2, HG_WIDTH), 1.0),
        'hg_norm_gain': 1.0 + n(ks[14], (DEPTH, HG_VAL), 0.05),
        'ffn_w1': n(ks[15], (N_DENSE, D_MODEL, D_FF), D_MODEL ** -0.5),
        'ffn_w3': n(ks[16], (N_DENSE, D_MODEL, D_FF), D_MODEL ** -0.5),
        'ffn_w2': n(ks[17], (N_DENSE, D_FF, D_MODEL), D_FF ** -0.5),
        'moe_router': n(ks[18], (N_MOE, D_MODEL, N_EXPERTS), D_MODEL ** -0.5),
        'moe_w1': n(ks[19], (N_MOE, N_EXPERTS, D_MODEL, D_FF_EXPERT), D_MODEL ** -0.5),
        'moe_w3': n(ks[20], (N_MOE, N_EXPERTS, D_MODEL, D_FF_EXPERT), D_MODEL ** -0.5),
        'moe_w2': n(ks[21], (N_MOE, N_EXPERTS, D_FF_EXPERT, D_MODEL), D_FF_EXPERT ** -0.5),
    }


def reference(x_prompt, x_sample, cache_k, cache_v, state_hgrn, c, c_ctx, w_mod, b_mod, norm_gains,
              w_in, w_out, attn_sink, hg_lb_logits, hg_norm_gain, ffn_w1, ffn_w3, ffn_w2,
              moe_router, moe_w1, moe_w3, moe_w2):
    lb_cum = jnp.cumsum(jax.nn.softmax(hg_lb_logits.astype(jnp.float32), axis=0), axis=0)
    rope = _axial_rope(x_sample.shape[1])
    yp, ys = x_prompt, x_sample
    ks_, vs_, ss_ = [], [], []
    for l in range(DEPTH):
        i = l // 2
        if l % 2 == 0:
            ffn = functools.partial(_swiglu, w1=ffn_w1[i], w3=ffn_w3[i], w2=ffn_w2[i])
        else:
            ffn = functools.partial(_moe, router=moe_router[i], w1=moe_w1[i], w3=moe_w3[i], w2=moe_w2[i])
        lb_l = lb_cum[l] - lb_cum[0]
        shared = (norm_gains[l], w_in[l], w_out[l], attn_sink[l], lb_l, hg_norm_gain[l], ffn)
        mod_p = (jax.nn.silu(c_ctx) @ w_mod[l] + b_mod[l])[None, None, :]
        mod_s = (jax.nn.silu(c) @ w_mod[l] + b_mod[l])[:, None, :]
        yp, (k_l, v_l, sf_l, sb_l) = _layer(yp, mod_p, None, None, *shared)
        ctx = (cache_k[:, l], cache_v[:, l], state_hgrn[:, l, 0], state_hgrn[:, l, 1])
        ys, _ = _layer(ys, mod_s, ctx, rope, *shared)
        ks_.append(k_l)
        vs_.append(v_l)
        ss_.append(jnp.stack([sf_l, sb_l], axis=1))
    return (yp, ys, jnp.stack(ks_, axis=1), jnp.stack(vs_, axis=1), jnp.stack(ss_, axis=1))
```

```python
import functools

import numpy as np
import jax
import jax.numpy as jnp
from jax import lax
from jax.experimental import pallas as pl
from jax.experimental.pallas import tpu as pltpu

F32 = jnp.float32
BF16 = jnp.bfloat16

D_MODEL = 2048
BATCH = 16
SEQ = 256
DEPTH = 4
DEC_BATCH = 4
DEC_SEQ = 1024
PAST_LEN = 256
GRID_W = 64
HEAD_DIM = 128
ATT_HEADS = 8
ATT_KV_HEADS = 2
Q_PER_KV = 4
ATT_WIDTH = 1024
WINDOW = 128
ATT_BLOCK = 128
ROPE_BASE = 10000.0
HG_WIDTH = 1024
HG_HEADS = 8
HG_DIM = 128
N_EXPERTS = 8
D_FF = 5632
D_FF_EXPERT = 2816
NORM_EPS = 1e-6
IN_COLS = 6656

N_PROMPT = BATCH * SEQ
N_SAMPLE = DEC_BATCH * DEC_SEQ
N_TOK = N_PROMPT + N_SAMPLE
MOD_GROUP = 1024
N_GROUPS = N_TOK // MOD_GROUP

COL_K = 8
COL_V = 10
COL_HQ = 12
COL_HF = 20
COL_HB = 28
COL_HI = 36
COL_HG = 44

CHUNK = 128
N_LEVELS = 7

VMEM_LIMIT = 56 * 1024 * 1024

MOE_TM = 256
MOE_ROWS = 2 * N_TOK + N_EXPERTS * MOE_TM
MOE_TILES = MOE_ROWS // MOE_TM


def _params(sem):
    return pltpu.CompilerParams(dimension_semantics=sem, vmem_limit_bytes=VMEM_LIMIT)


def _silu(x):
    return x / (1.0 + jnp.exp(-x))


def _dot_nt(a, b):
    return lax.dot_general(a, b, (((1,), (1,)), ((), ())), preferred_element_type=F32)


def _dot_tn(a, b):
    return lax.dot_general(a, b, (((0,), (0,)), ((), ())), preferred_element_type=F32)


def _mod_kernel(c_ref, w_ref, b_ref, o_ref):
    s = _silu(c_ref[...]).astype(BF16)
    o_ref[...] = jnp.dot(s, w_ref[...].astype(BF16), preferred_element_type=F32) + b_ref[...]


def _modulation(cvec, w_mod, b_mod):
    tn = 1024
    n = 6 * D_MODEL
    return pl.pallas_call(
        _mod_kernel,
        grid=(DEPTH, n // tn),
        in_specs=[
            pl.BlockSpec((8, D_MODEL), lambda l, j: (0, 0)),
            pl.BlockSpec((None, D_MODEL, tn), lambda l, j: (l, 0, j)),
            pl.BlockSpec((None, 1, tn), lambda l, j: (l, 0, j)),
        ],
        out_specs=pl.BlockSpec((None, 8, tn), lambda l, j: (l, 0, j)),
        out_shape=jax.ShapeDtypeStruct((DEPTH, 8, n), F32),
        compiler_params=_params(("arbitrary", "arbitrary")),
        name="modulation",
    )(cvec, w_mod, b_mod.reshape(DEPTH, 1, n))


def _rms(x, gain):
    return x * lax.rsqrt(jnp.mean(x * x, axis=-1, keepdims=True) + NORM_EPS) * gain


def _resnorm_kernel(*refs, n_m, gate_row, has_h, sc_row, sh_row):
    refs = list(refs)
    x_ref = refs.pop(0)
    x = x_ref[...]
    if n_m == 1:
        m = refs.pop(0)[...]
    elif n_m == 2:
        y0 = refs.pop(0)[...]
        y1 = refs.pop(0)[...]
        gt = refs.pop(0)[...]
        m = gt[:, 0:1] * y0 + gt[:, 1:2] * y1
    if n_m:
        ga = refs.pop(0)[...]
        modg = refs.pop(0)
        x = x + modg[gate_row:gate_row + 1, :] * _rms(m, ga)
    if has_h:
        gb = refs.pop(0)[...]
        modh = refs.pop(0)
        h = _rms(x, gb) * (1.0 + modh[sc_row:sc_row + 1, :]) + modh[sh_row:sh_row + 1, :]
    if n_m:
        refs.pop(0)[...] = x
    if has_h:
        refs.pop(0)[...] = h.astype(BF16)


def _resnorm(x, ms, gain_m, mod_gate, gate_row, gain_h, mod_h, sc_row, sh_row, gates=None):
    tm = 256
    per = MOD_GROUP // tm
    n_m = len(ms)
    has_h = gain_h is not None
    row = pl.BlockSpec((tm, D_MODEL), lambda i: (i, 0))
    vec = pl.BlockSpec((1, D_MODEL), lambda i: (0, 0))
    mod = pl.BlockSpec((None, 8, D_MODEL), lambda i: (i // per, 0, 0))
    args, specs = [x], [row]
    for m in ms:
        args.append(m)
        specs.append(row)
    if n_m == 2:
        args.append(gates)
        specs.append(pl.BlockSpec((tm, 2), lambda i: (i, 0)))
    if n_m:
        args += [gain_m.reshape(1, D_MODEL), mod_gate]
        specs += [vec, mod]
    if has_h:
        args += [gain_h.reshape(1, D_MODEL), mod_h]
        specs += [vec, mod]
    out_shape, out_specs = [], []
    if n_m:
        out_shape.append(jax.ShapeDtypeStruct((N_TOK, D_MODEL), F32))
        out_specs.append(row)
    if has_h:
        out_shape.append(jax.ShapeDtypeStruct((N_TOK, D_MODEL), BF16))
        out_specs.append(row)
    outs = pl.pallas_call(
        functools.partial(_resnorm_kernel, n_m=n_m, gate_row=gate_row, has_h=has_h,
                          sc_row=sc_row, sh_row=sh_row),
        grid=(N_TOK // tm,),
        in_specs=specs,
        out_specs=out_specs,
        out_shape=out_shape,
        compiler_params=_params(("arbitrary",)),
        name="resnorm",
    )(*args)
    return outs


def _gmm_kernel(te_ref, tv_ref, x_ref, w_ref, o_ref):
    del te_ref
    valid = tv_ref[pl.program_id(1)] != 0

    @pl.when(valid)
    def _():
        o_ref[...] = jnp.dot(x_ref[...], w_ref[...], preferred_element_type=F32).astype(o_ref.dtype)

    @pl.when(jnp.logical_not(valid))
    def _():
        o_ref[...] = jnp.zeros_like(o_ref)


def _gmm_swiglu_kernel(te_ref, tv_ref, x_ref, w1_ref, w3_ref, o_ref):
    del te_ref
    valid = tv_ref[pl.program_id(1)] != 0

    @pl.when(valid)
    def _():
        x = x_ref[...]
        a = jnp.dot(x, w1_ref[...], preferred_element_type=F32)
        b = jnp.dot(x, w3_ref[...], preferred_element_type=F32)
        o_ref[...] = (_silu(a) * b).astype(o_ref.dtype)

    @pl.when(jnp.logical_not(valid))
    def _():
        o_ref[...] = jnp.zeros_like(o_ref)


def _gmm(x, ws, tile_expert, tile_valid, tm, tn, out_dtype, name):
    m, k = x.shape
    n = ws[0].shape[-1]
    x_spec = pl.BlockSpec((tm, k), lambda j, i, te, tv: (i, 0))
    w_spec = pl.BlockSpec((None, k, tn), lambda j, i, te, tv: (te[i], 0, j))
    o_spec = pl.BlockSpec((tm, tn), lambda j, i, te, tv: (i, j))
    body = _gmm_kernel if len(ws) == 1 else _gmm_swiglu_kernel
    return pl.pallas_call(
        body,
        grid_spec=pltpu.PrefetchScalarGridSpec(
            num_scalar_prefetch=2,
            grid=(n // tn, m // tm),
            in_specs=[x_spec] + [w_spec] * len(ws),
            out_specs=o_spec,
        ),
        out_shape=jax.ShapeDtypeStruct((m, n), out_dtype),
        compiler_params=_params(("arbitrary", "arbitrary")),
        name=name,
    )(tile_expert, tile_valid, x, *ws)


def _dense(x, ws, tm, tn, out_dtype, name):
    tiles = x.shape[0] // tm
    return _gmm(x, [w[None] for w in ws], jnp.zeros((tiles,), jnp.int32),
                jnp.ones((tiles,), jnp.int32), tm, tn, out_dtype, name)


def _sink_rows(sink_ref, g, rows, per):
    r = lax.broadcasted_iota(jnp.int32, (rows, 1), 0)
    sk = jnp.full((rows, 1), sink_ref[g * Q_PER_KV + Q_PER_KV - 1], F32)
    for h in range(Q_PER_KV - 2, -1, -1):
        sk = jnp.where(r < (h + 1) * per, sink_ref[g * Q_PER_KV + h], sk)
    return sk


def _attn_ctx_kernel(sink_ref, q_ref, k_ref, v_ref, o_ref):
    g = pl.program_id(1)
    scale = HEAD_DIM ** -0.5
    q = q_ref[...]
    qs = jnp.concatenate([q[:, r * HEAD_DIM:(r + 1) * HEAD_DIM] for r in range(Q_PER_KV)], axis=0)
    s = _dot_nt(qs.astype(BF16), k_ref[...].astype(BF16)) * scale
    sk = _sink_rows(sink_ref, g, Q_PER_KV * SEQ, SEQ)
    m = jnp.maximum(jnp.max(s, axis=-1, keepdims=True), sk)
    e = jnp.exp(s - m)
    den = jnp.sum(e, axis=-1, keepdims=True) + jnp.exp(sk - m)
    o = jnp.dot(e.astype(BF16), v_ref[...].astype(BF16), preferred_element_type=F32) / den
    o_ref[...] = jnp.concatenate([o[r * SEQ:(r + 1) * SEQ] for r in range(Q_PER_KV)],
                                 axis=1).astype(o_ref.dtype)


def _attn_ctx(p, sink):
    return pl.pallas_call(
        _attn_ctx_kernel,
        grid=(BATCH, ATT_KV_HEADS),
        in_specs=[
            pl.BlockSpec(memory_space=pltpu.SMEM),
            pl.BlockSpec((SEQ, Q_PER_KV * HEAD_DIM), lambda b, g: (b, g)),
            pl.BlockSpec((SEQ, HEAD_DIM), lambda b, g: (b, COL_K + g)),
            pl.BlockSpec((SEQ, HEAD_DIM), lambda b, g: (b, COL_V + g)),
        ],
        out_specs=pl.BlockSpec((SEQ, Q_PER_KV * HEAD_DIM), lambda b, g: (b, g)),
        out_shape=jax.ShapeDtypeStruct((N_PROMPT, ATT_WIDTH), BF16),
        compiler_params=_params(("arbitrary", "arbitrary")),
        name="attn_ctx",
    )(sink, p, p, p)


def _rope(x, c, sa, sb):
    return x * c + pltpu.roll(x, 96, 1) * sa + pltpu.roll(x, 32, 1) * sb


def _attn_lat_kernel(sink_ref, q_ref, k_ref, v_ref, kc_ref, vc_ref, cq_ref, sqa_ref, sqb_ref,
                     ck_ref, ska_ref, skb_ref, o_ref, kr_s, vb_s):
    g = pl.program_id(1)
    qb = pl.program_id(2)
    scale = HEAD_DIM ** -0.5
    band = 3 * ATT_BLOCK

    @pl.when(qb == 0)
    def _():
        kr_s[...] = _rope(k_ref[...], ck_ref[...], ska_ref[...], skb_ref[...]).astype(BF16)
        vb_s[...] = v_ref[...].astype(BF16)

    q = q_ref[...]
    cq, sqa, sqb = cq_ref[...], sqa_ref[...], sqb_ref[...]
    qs = jnp.concatenate(
        [_rope(q[:, r * HEAD_DIM:(r + 1) * HEAD_DIM], cq, sqa, sqb) for r in range(Q_PER_KV)],
        axis=0).astype(BF16)
    rows = Q_PER_KV * ATT_BLOCK
    start = pl.multiple_of(jnp.clip(qb - 1, 0, DEC_SEQ // ATT_BLOCK - 3) * ATT_BLOCK, ATT_BLOCK)
    kb = kr_s[pl.ds(start, band), :]
    vb = vb_s[pl.ds(start, band), :]
    s_loc = _dot_nt(qs, kb) * scale
    kpos = start + lax.broadcasted_iota(jnp.int32, (rows, band), 1)
    qpos = qb * ATT_BLOCK + (lax.broadcasted_iota(jnp.int32, (rows, band), 0) & (ATT_BLOCK - 1))
    s_loc = jnp.where(jnp.abs(kpos - qpos) <= WINDOW, s_loc, -jnp.inf)
    s_ctx = _dot_nt(qs, kc_ref[...]) * scale
    sk = _sink_rows(sink_ref, g, rows, ATT_BLOCK)
    m = jnp.maximum(jnp.maximum(jnp.max(s_loc, axis=-1, keepdims=True),
                                jnp.max(s_ctx, axis=-1, keepdims=True)), sk)
    e_loc = jnp.exp(s_loc - m)
    e_ctx = jnp.exp(s_ctx - m)
    den = (jnp.sum(e_loc, axis=-1, keepdims=True) + jnp.sum(e_ctx, axis=-1, keepdims=True)
           + jnp.exp(sk - m))
    o = (jnp.dot(e_loc.astype(BF16), vb, preferred_element_type=F32)
         + jnp.dot(e_ctx.astype(BF16), vc_ref[...], preferred_element_type=F32)) / den
    o_ref[...] = jnp.concatenate([o[r * ATT_BLOCK:(r + 1) * ATT_BLOCK] for r in range(Q_PER_KV)],
                                 axis=1).astype(o_ref.dtype)


def _attn_lat(p, sink, kc, vc, rope):
    nqb = DEC_SEQ // ATT_BLOCK
    row0 = N_PROMPT // DEC_SEQ
    qrow0 = N_PROMPT // ATT_BLOCK
    tq = pl.BlockSpec((ATT_BLOCK, HEAD_DIM), lambda b, g, qb: (qb, 0))
    tk = pl.BlockSpec((DEC_SEQ, HEAD_DIM), lambda b, g, qb: (0, 0))
    ctx = pl.BlockSpec((None, None, PAST_LEN, HEAD_DIM), lambda b, g, qb: (b, g, 0, 0))
    cos, sin_a, sin_b = rope
    return pl.pallas_call(
        _attn_lat_kernel,
        grid=(DEC_BATCH, ATT_KV_HEADS, nqb),
        in_specs=[
            pl.BlockSpec(memory_space=pltpu.SMEM),
            pl.BlockSpec((ATT_BLOCK, Q_PER_KV * HEAD_DIM), lambda b, g, qb: (qrow0 + b * nqb + qb, g)),
            pl.BlockSpec((DEC_SEQ, HEAD_DIM), lambda b, g, qb: (row0 + b, COL_K + g)),
            pl.BlockSpec((DEC_SEQ, HEAD_DIM), lambda b, g, qb: (row0 + b, COL_V + g)),
            ctx, ctx, tq, tq, tq, tk, tk, tk,
        ],
        out_specs=pl.BlockSpec((ATT_BLOCK, Q_PER_KV * HEAD_DIM), lambda b, g, qb: (b * nqb + qb, g)),
        out_shape=jax.ShapeDtypeStruct((N_SAMPLE, ATT_WIDTH), BF16),
        scratch_shapes=[pltpu.VMEM((DEC_SEQ, HEAD_DIM), BF16), pltpu.VMEM((DEC_SEQ, HEAD_DIM), BF16)],
        compiler_params=_params(("arbitrary", "arbitrary", "arbitrary")),
        name="attn_lat",
    )(sink, p, p, p, kc, vc, cos, sin_a, sin_b, cos, sin_a, sin_b)


def _rope_tables():
    rows = DEC_SEQ // GRID_W
    row = np.repeat(np.arange(rows), GRID_W).astype(np.float32)
    col = np.tile(np.arange(GRID_W), rows).astype(np.float32)
    axis_dim = HEAD_DIM // 2
    inv = (ROPE_BASE ** (-np.arange(0, axis_dim, 2, dtype=np.float32) / axis_dim)).astype(np.float32)
    lane = np.arange(HEAD_DIM)
    pos = np.where((lane // axis_dim)[None, :] == 0, row[:, None], col[:, None])
    ang = pos * inv[lane % (axis_dim // 2)][None, :]
    first = ((lane % axis_dim) < axis_dim // 2)[None, :]
    cos = np.cos(ang).astype(np.float32)
    sin = np.sin(ang).astype(np.float32)
    sin_a = np.where(first, -sin, 0.0).astype(np.float32)
    sin_b = np.where(first, 0.0, sin).astype(np.float32)
    return jnp.asarray(cos), jnp.asarray(sin_a), jnp.asarray(sin_b)


def _level_tables():
    t = np.arange(CHUNK)
    x = t[:, None] ^ t[None, :]
    lvl = np.where(x > 0, np.floor(np.log2(np.maximum(x, 1))).astype(np.int32), -1)
    fwd = np.where(t[:, None] >= t[None, :], lvl, -2).astype(np.int32)
    tri = (t[:, None] >= t[None, :]).astype(np.float32)
    return (jnp.asarray(np.stack([fwd, fwd.T])), jnp.asarray(np.stack([tri, tri.T]), dtype=BF16))


def _forget_gate(z, log_lb, log1m_lb):
    t = jnp.log1p(jnp.exp(-jnp.abs(z)))
    c = log1m_lb + jnp.minimum(z, 0.0) - t
    log_f = jnp.maximum(log_lb, c) + jnp.log1p(jnp.exp(-jnp.abs(log_lb - c)))
    return log_f, jnp.exp(log1m_lb + jnp.minimum(-z, 0.0) - t)


def _hgrn_chunk(q, g, k, v, state, tri, lvl, rev):
    g1 = g.astype(BF16)
    r1 = g - g1.astype(F32)
    g2 = r1.astype(BF16)
    g3 = (r1 - g2.astype(F32)).astype(BF16)
    dot = functools.partial(jnp.dot, preferred_element_type=F32)
    b = dot(tri, g1) + dot(tri, g2) + dot(tri, g3)
    btot = b[0:1] if rev else b[CHUNK - 1:CHUNK]
    row = lax.broadcasted_iota(jnp.int32, (CHUNK, HG_DIM), 0)
    qb16 = q.astype(BF16)
    kb16 = k.astype(BF16)
    a = jnp.where(lvl == -1, _dot_nt(qb16, kb16), 0.0)
    for j in range(N_LEVELS):
        half = 1 << j
        mid = half if rev else half - 1
        qside = ((row & half) == 0) if rev else ((row & half) != 0)
        if j == 0:
            dec = jnp.where(qside, g, 0.0)
        elif j == 1:
            b8 = b.reshape(CHUNK // 8, 8, HG_DIM)
            sub = lax.broadcasted_iota(jnp.int32, b8.shape, 1)
            ref = jnp.where(sub < 4, b8[:, mid:mid + 1, :], b8[:, 4 + mid:5 + mid, :])
            dec = (-jnp.abs(b8 - ref)).reshape(CHUNK, HG_DIM)
        else:
            blk = max(2 * half, 8)
            bb = b.reshape(CHUNK // blk, blk, HG_DIM)
            dec = (-jnp.abs(bb - bb[:, mid:mid + 1, :])).reshape(CHUNK, HG_DIM)
        x = (jnp.where(qside, q, k) * jnp.exp(dec)).astype(BF16)
        a = jnp.where(lvl == j, _dot_nt(x, x), a)
    vb = v.astype(BF16)
    o = dot((q * jnp.exp(b)).astype(BF16), state.astype(BF16)) + dot(a.astype(BF16), vb)
    kd = (k * jnp.exp(btot - b)).astype(BF16)
    dcol = jnp.transpose(jnp.broadcast_to(jnp.exp(btot), (HG_DIM, HG_DIM)))
    return o, state * dcol + _dot_tn(kd, vb)


def _hgrn_kernel(*refs, seq_len, has_init):
    refs = list(refs)
    hq_ref, hf_ref, hb_ref, hi_ref, hg_ref, lb_ref, gain_ref, tri_ref, lvl_ref = refs[:9]
    refs = refs[9:]
    init_ref = refs.pop(0) if has_init else None
    out_ref = refs.pop(0)
    fin_ref = None if has_init else refs.pop(0)
    of_s = refs.pop(0)
    n_chunks = seq_len // CHUNK

    def load(ci, z_ref, lrow):
        sl = pl.ds(pl.multiple_of(ci * CHUNK, CHUNK), CHUNK)
        g, k = _forget_gate(z_ref[sl, :], lb_ref[lrow:lrow + 1, :], lb_ref[lrow + 1:lrow + 2, :])
        return sl, _silu(hq_ref[sl, :]), g, k, hi_ref[sl, :]

    def fwd_step(ci, state):
        sl, q, g, k, v = load(ci, hf_ref, 0)
        o, state = _hgrn_chunk(q, g, k, v, state, tri_ref[0], lvl_ref[0], False)
        of_s[sl, :] = o
        return state

    def bwd_step(i, state):
        ci = n_chunks - 1 - i
        sl, q, g, k, v = load(ci, hb_ref, 2)
        o, state = _hgrn_chunk(q, g, k, v, state, tri_ref[1], lvl_ref[1], True)
        o = o + of_s[sl, :]
        o = o * lax.rsqrt(jnp.mean(o * o, axis=-1, keepdims=True) + NORM_EPS) * gain_ref[...]
        out_ref[sl, :] = (o * _silu(hg_ref[sl, :])).astype(out_ref.dtype)
        return state

    zero = jnp.zeros((HG_DIM, HG_DIM), F32)
    s_f = lax.fori_loop(0, n_chunks, fwd_step, init_ref[0] if has_init else zero)
    s_b = lax.fori_loop(0, n_chunks, bwd_step, init_ref[1] if has_init else zero)
    if not has_init:
        fin_ref[0] = s_f
        fin_ref[1] = s_b


def _hgrn(p, lb_tab, gain, tables, seq_len, n_seq, row0, init, layer):
    lvl, tri = tables
    has_init = init is not None

    def col(c):
        return pl.BlockSpec((seq_len, HG_DIM), lambda b, h: (row0 + b, c + h))

    in_specs = [col(COL_HQ), col(COL_HF), col(COL_HB), col(COL_HI), col(COL_HG),
                pl.BlockSpec((4, HG_DIM), lambda b, h: (0, h)),
                pl.BlockSpec((1, HG_DIM), lambda b, h: (0, 0)),
                pl.BlockSpec((2, CHUNK, CHUNK), lambda b, h: (0, 0, 0)),
                pl.BlockSpec((2, CHUNK, CHUNK), lambda b, h: (0, 0, 0))]
    args = [p, p, p, p, p, lb_tab, gain.reshape(1, HG_DIM), tri, lvl]
    out_specs = [pl.BlockSpec((seq_len, HG_DIM), lambda b, h: (b, h))]
    out_shape = [jax.ShapeDtypeStruct((n_seq * seq_len, HG_WIDTH), BF16)]
    if has_init:
        in_specs.append(pl.BlockSpec((None, None, 2, None, HG_DIM, HG_DIM),
                                     lambda b, h: (b, layer, 0, h, 0, 0)))
        args.append(init)
    else:
        out_specs.append(pl.BlockSpec((None, 2, None, HG_DIM, HG_DIM), lambda b, h: (b, 0, h, 0, 0)))
        out_shape.append(jax.ShapeDtypeStruct((n_seq, 2, HG_HEADS, HG_DIM, HG_DIM), F32))
    return pl.pallas_call(
        functools.partial(_hgrn_kernel, seq_len=seq_len, has_init=has_init),
        grid=(n_seq, HG_HEADS),
        in_specs=in_specs,
        out_specs=out_specs,
        out_shape=out_shape,
        scratch_shapes=[pltpu.VMEM((seq_len, HG_DIM), F32)],
        compiler_params=_params(("arbitrary", "arbitrary")),
        name="hgrn_lat" if has_init else "hgrn_ctx",
    )(*args)


def _router_kernel(h_ref, r_ref, o_ref):
    lg = jnp.dot(h_ref[...], r_ref[...], preferred_element_type=F32)
    lane = lax.broadcasted_iota(jnp.int32, lg.shape, 1)
    lg = jnp.where(lane < N_EXPERTS, lg, -jnp.inf)
    m1 = jnp.max(lg, axis=-1, keepdims=True)
    i1 = jnp.min(jnp.where(lg == m1, lane, HEAD_DIM), axis=-1, keepdims=True)
    lg2 = jnp.where(lane == i1, -jnp.inf, lg)
    m2 = jnp.max(lg2, axis=-1, keepdims=True)
    i2 = jnp.min(jnp.where(lg2 == m2, lane, HEAD_DIM), axis=-1, keepdims=True)
    e2 = jnp.exp(m2 - m1)
    den = 1.0 + e2
    o_ref[...] = jnp.where(lane == 0, i1.astype(F32),
                           jnp.where(lane == 1, i2.astype(F32),
                                     jnp.where(lane == 2, 1.0 / den,
                                               jnp.where(lane == 3, e2 / den, 0.0))))


def _route(h, router):
    tm = 512
    rp = jnp.pad(router, ((0, 0), (0, HEAD_DIM - N_EXPERTS))).astype(BF16)
    out = pl.pallas_call(
        _router_kernel,
        grid=(N_TOK // tm,),
        in_specs=[pl.BlockSpec((tm, D_MODEL), lambda i: (i, 0)),
                  pl.BlockSpec((D_MODEL, HEAD_DIM), lambda i: (0, 0))],
        out_specs=pl.BlockSpec((tm, HEAD_DIM), lambda i: (i, 0)),
        out_shape=jax.ShapeDtypeStruct((N_TOK, HEAD_DIM), F32),
        compiler_params=_params(("arbitrary",)),
        name="router",
    )(h, rp)
    top_i = out[:, 0:2].astype(jnp.int32)
    gates = out[:, 2:4]
    e_flat = top_i.reshape(-1)
    onehot = (e_flat[:, None] == jnp.arange(N_EXPERTS)[None, :]).astype(jnp.int32)
    csum = jnp.cumsum(onehot, axis=0)
    counts = csum[-1]
    padded = ((counts + MOE_TM - 1) // MOE_TM) * MOE_TM
    ends = jnp.cumsum(padded)
    starts = ends - padded
    pos = jnp.sum(onehot * (csum - 1 + starts[None, :]), axis=1)
    row_token = jnp.zeros((MOE_ROWS,), jnp.int32).at[pos].set(jnp.arange(2 * N_TOK, dtype=jnp.int32) // 2)
    tile_start = jnp.arange(MOE_TILES, dtype=jnp.int32) * MOE_TM
    tile_expert = jnp.minimum(jnp.sum(tile_start[:, None] >= ends[None, :], axis=1), N_EXPERTS - 1)
    tile_valid = (tile_start < ends[-1]).astype(jnp.int32)
    return gates, pos.reshape(N_TOK, 2), row_token, tile_expert.astype(jnp.int32), tile_valid


def kernel(x_prompt, x_sample, cache_k, cache_v, state_hgrn, c, c_ctx, w_mod, b_mod, norm_gains,
           w_in, w_out, attn_sink, hg_lb_logits, hg_norm_gain, ffn_w1, ffn_w3, ffn_w2,
           moe_router, moe_w1, moe_w3, moe_w2):
    x = jnp.concatenate([x_prompt.reshape(N_PROMPT, D_MODEL), x_sample.reshape(N_SAMPLE, D_MODEL)], axis=0)

    cvec = jnp.concatenate([c_ctx[None], c, jnp.zeros((3, D_MODEL), F32)], axis=0)
    mod = _modulation(cvec, w_mod, b_mod).reshape(DEPTH, 8, 6, D_MODEL)
    mod = jnp.pad(mod, ((0, 0), (0, 0), (0, 2), (0, 0)))
    group_src = np.array([0] * (N_PROMPT // MOD_GROUP) + list(range(1, 1 + DEC_BATCH)))
    mod = mod[:, group_src]

    lb_cum = jnp.cumsum(jax.nn.softmax(hg_lb_logits.astype(F32), axis=0), axis=0)
    lb = lb_cum - lb_cum[0:1]
    lb_tab = jnp.stack([jnp.log(lb[:, 0]), jnp.log1p(-lb[:, 0]),
                        jnp.log(lb[:, 1]), jnp.log1p(-lb[:, 1])], axis=1)

    rope = _rope_tables()
    tables = _level_tables()
    kc_all = cache_k.transpose(1, 0, 3, 2, 4).astype(BF16)
    vc_all = cache_v.transpose(1, 0, 3, 2, 4).astype(BF16)

    (h,) = _resnorm(x, [], None, None, 0, norm_gains[0, 0], mod[0], 1, 0)
    ks, vs, ss = [], [], []
    for l in range(DEPTH):
        i = l // 2
        p = _dense(h, [w_in[l].astype(BF16)], 512, 1664, F32, "proj_in")
        a_ctx = _attn_ctx(p, attn_sink[l])
        a_lat = _attn_lat(p, attn_sink[l], kc_all[l], vc_all[l], rope)
        g_ctx, s_ctx = _hgrn(p, lb_tab[l], hg_norm_gain[l], tables, SEQ, BATCH, 0, None, l)
        (g_lat,) = _hgrn(p, lb_tab[l], hg_norm_gain[l], tables, DEC_SEQ, DEC_BATCH,
                         N_PROMPT // DEC_SEQ, state_hgrn, l)
        mix = jnp.concatenate([jnp.concatenate([a_ctx, g_ctx], axis=1),
                               jnp.concatenate([a_lat, g_lat], axis=1)], axis=0)
        m = _dense(mix, [w_out[l].astype(BF16)], 512, 1024, F32, "proj_out")
        x, h = _resnorm(x, [m], norm_gains[l, 1], mod[l], 2, norm_gains[l, 2], mod[l], 4, 3)
        if l % 2 == 0:
            act = _dense(h, [ffn_w1[i].astype(BF16), ffn_w3[i].astype(BF16)], 512, 1408, BF16, "ffn_up")
            f = [_dense(act, [ffn_w2[i].astype(BF16)], 512, 1024, F32, "ffn_down")]
            gates = None
        else:
            gates, pos, row_token, tile_expert, tile_valid = _route(h, moe_router[i])
            xs = jnp.take(h, row_token, axis=0)
            act = _gmm(xs, [moe_w1[i].astype(BF16), moe_w3[i].astype(BF16)], tile_expert, tile_valid,
                       MOE_TM, 1408, BF16, "moe_up")
            ys = _gmm(act, [moe_w2[i].astype(BF16)], tile_expert, tile_valid, MOE_TM, 1024, F32, "moe_down")
            f = [jnp.take(ys, pos[:, 0], axis=0), jnp.take(ys, pos[:, 1], axis=0)]
        if l + 1 < DEPTH:
            x, h = _resnorm(x, f, norm_gains[l, 3], mod[l], 5, norm_gains[l + 1, 0], mod[l + 1], 1, 0,
                            gates=gates)
        else:
            (x,) = _resnorm(x, f, norm_gains[l, 3], mod[l], 5, None, None, 0, 0, gates=gates)
        kv = p[:N_PROMPT, COL_K * HEAD_DIM:COL_HQ * HEAD_DIM].reshape(BATCH, SEQ, 2, ATT_KV_HEADS, HEAD_DIM)
        ks.append(kv[:, :, 0])
        vs.append(kv[:, :, 1])
        ss.append(s_ctx)
    return (x[:N_PROMPT].reshape(BATCH, SEQ, D_MODEL), x[N_PROMPT:].reshape(DEC_BATCH, DEC_SEQ, D_MODEL),
            jnp.stack(ks, axis=1), jnp.stack(vs, axis=1), jnp.stack(ss, axis=1))
```

```python
import functools

import numpy as np
import jax
import jax.numpy as jnp
from jax import lax
from jax.experimental import pallas as pl
from jax.experimental.pallas import tpu as pltpu

F32 = jnp.float32
BF16 = jnp.bfloat16

D_MODEL = 2048
BATCH = 16
SEQ = 256
DEPTH = 4
DEC_BATCH = 4
DEC_SEQ = 1024
PAST_LEN = 256
GRID_W = 64
HEAD_DIM = 128
ATT_HEADS = 8
ATT_KV_HEADS = 2
Q_PER_KV = 4
ATT_WIDTH = 1024
WINDOW = 128
ATT_BLOCK = 128
ROPE_BASE = 10000.0
HG_WIDTH = 1024
HG_HEADS = 8
HG_DIM = 128
N_EXPERTS = 8
D_FF = 5632
D_FF_EXPERT = 2816
NORM_EPS = 1e-6
IN_COLS = 6656

N_PROMPT = BATCH * SEQ
N_SAMPLE = DEC_BATCH * DEC_SEQ
N_TOK = N_PROMPT + N_SAMPLE
MOD_GROUP = 1024
N_GROUPS = N_TOK // MOD_GROUP

COL_K = 8
COL_V = 10
COL_HQ = 12
COL_HF = 20
COL_HB = 28
COL_HI = 36
COL_HG = 44
MIX_HG = ATT_WIDTH // HG_DIM

CHUNK = 128
N_LEVELS = 7
LOG2E = 1.4426950408889634

VMEM_LIMIT = 56 * 1024 * 1024

MOE_TM = 256
MOE_ROWS = 2 * N_TOK + N_EXPERTS * MOE_TM
MOE_TILES = MOE_ROWS // MOE_TM

ANY_SPEC = pl.BlockSpec(memory_space=pl.ANY)


def _params(sem):
    return pltpu.CompilerParams(dimension_semantics=sem, vmem_limit_bytes=VMEM_LIMIT)


def _silu(x):
    return x / (1.0 + jnp.exp(-x))


def _dot_nt(a, b):
    return lax.dot_general(a, b, (((1,), (1,)), ((), ())), preferred_element_type=F32)


def _dot_tn(a, b):
    return lax.dot_general(a, b, (((0,), (0,)), ((), ())), preferred_element_type=F32)


def _mod_kernel(c_ref, w_ref, b_ref, o_ref):
    s = _silu(c_ref[...]).astype(BF16)
    o_ref[...] = jnp.dot(s, w_ref[...].astype(BF16), preferred_element_type=F32) + b_ref[...]


def _modulation(cvec, w_mod, b_mod):
    tn = 1024
    n = 6 * D_MODEL
    return pl.pallas_call(
        _mod_kernel,
        grid=(DEPTH, n // tn),
        in_specs=[
            pl.BlockSpec((8, D_MODEL), lambda l, j: (0, 0)),
            pl.BlockSpec((None, D_MODEL, tn), lambda l, j: (l, 0, j)),
            pl.BlockSpec((None, 1, tn), lambda l, j: (l, 0, j)),
        ],
        out_specs=pl.BlockSpec((None, 8, tn), lambda l, j: (l, 0, j)),
        out_shape=jax.ShapeDtypeStruct((DEPTH, 8, n), F32),
        compiler_params=_params(("arbitrary", "arbitrary")),
        name="modulation",
    )(cvec, w_mod, b_mod.reshape(DEPTH, 1, n))


def _rms(x, gain):
    return x * lax.rsqrt(jnp.mean(x * x, axis=-1, keepdims=True) + NORM_EPS) * gain


def _resnorm_kernel(*refs, n_m, gate_row, has_h, sc_row, sh_row, wide_h):
    refs = list(refs)
    x_ref = refs.pop(0)
    x = x_ref[...]
    if n_m == 1:
        m = refs.pop(0)[...]
    elif n_m == 2:
        y0 = refs.pop(0)[...]
        y1 = refs.pop(0)[...]
        gt = refs.pop(0)[...]
        m = gt[:, 0:1] * y0 + gt[:, 1:2] * y1
    if n_m:
        ga = refs.pop(0)[...]
        modg = refs.pop(0)
        x = x + modg[gate_row:gate_row + 1, :] * _rms(m, ga)
    if has_h:
        gb = refs.pop(0)[...]
        modh = refs.pop(0)
        h = _rms(x, gb) * (1.0 + modh[sc_row:sc_row + 1, :]) + modh[sh_row:sh_row + 1, :]
    if n_m:
        refs.pop(0)[...] = x
    if has_h:
        refs.pop(0)[...] = h.astype(BF16)
        if wide_h:
            refs.pop(0)[...] = h


def _resnorm(x, ms, gain_m, mod_gate, gate_row, gain_h, mod_h, sc_row, sh_row, gates=None, wide_h=False):
    tm = 256
    per = MOD_GROUP // tm
    n_m = len(ms)
    has_h = gain_h is not None
    row = pl.BlockSpec((tm, D_MODEL), lambda i: (i, 0))
    vec = pl.BlockSpec((1, D_MODEL), lambda i: (0, 0))
    mod = pl.BlockSpec((None, 8, D_MODEL), lambda i: (i // per, 0, 0))
    args, specs = [x], [row]
    for m in ms:
        args.append(m)
        specs.append(row)
    if n_m == 2:
        args.append(gates)
        specs.append(pl.BlockSpec((tm, 2), lambda i: (i, 0)))
    if n_m:
        args += [gain_m.reshape(1, D_MODEL), mod_gate]
        specs += [vec, mod]
    if has_h:
        args += [gain_h.reshape(1, D_MODEL), mod_h]
        specs += [vec, mod]
    out_shape, out_specs = [], []
    if n_m:
        out_shape.append(jax.ShapeDtypeStruct((N_TOK, D_MODEL), F32))
        out_specs.append(row)
    if has_h:
        out_shape.append(jax.ShapeDtypeStruct((N_TOK, D_MODEL), BF16))
        out_specs.append(row)
        if wide_h:
            out_shape.append(jax.ShapeDtypeStruct((N_TOK, D_MODEL), F32))
            out_specs.append(row)
    outs = pl.pallas_call(
        functools.partial(_resnorm_kernel, n_m=n_m, gate_row=gate_row, has_h=has_h,
                          sc_row=sc_row, sh_row=sh_row, wide_h=wide_h),
        grid=(N_TOK // tm,),
        in_specs=specs,
        out_specs=out_specs,
        out_shape=out_shape,
        compiler_params=_params(("arbitrary",)),
        name="resnorm",
    )(*args)
    return outs


def _gmm_kernel(te_ref, tv_ref, x_ref, w_ref, o_ref):
    del te_ref
    valid = tv_ref[pl.program_id(1)] != 0

    @pl.when(valid)
    def _():
        o_ref[...] = jnp.dot(x_ref[...].astype(BF16), w_ref[...],
                             preferred_element_type=F32).astype(o_ref.dtype)

    @pl.when(jnp.logical_not(valid))
    def _():
        o_ref[...] = jnp.zeros_like(o_ref)


def _gmm_swiglu_kernel(te_ref, tv_ref, x_ref, w1_ref, w3_ref, o_ref):
    del te_ref
    valid = tv_ref[pl.program_id(1)] != 0

    @pl.when(valid)
    def _():
        x = x_ref[...].astype(BF16)
        a = jnp.dot(x, w1_ref[...], preferred_element_type=F32)
        b = jnp.dot(x, w3_ref[...], preferred_element_type=F32)
        o_ref[...] = (_silu(a) * b).astype(o_ref.dtype)

    @pl.when(jnp.logical_not(valid))
    def _():
        o_ref[...] = jnp.zeros_like(o_ref)


def _gmm(x, ws, tile_expert, tile_valid, tm, tn, out_dtype, name):
    m, k = x.shape
    n = ws[0].shape[-1]
    x_spec = pl.BlockSpec((tm, k), lambda j, i, te, tv: (i, 0))
    w_spec = pl.BlockSpec((None, k, tn), lambda j, i, te, tv: (te[i], 0, j))
    o_spec = pl.BlockSpec((tm, tn), lambda j, i, te, tv: (i, j))
    body = _gmm_kernel if len(ws) == 1 else _gmm_swiglu_kernel
    return pl.pallas_call(
        body,
        grid_spec=pltpu.PrefetchScalarGridSpec(
            num_scalar_prefetch=2,
            grid=(n // tn, m // tm),
            in_specs=[x_spec] + [w_spec] * len(ws),
            out_specs=o_spec,
        ),
        out_shape=jax.ShapeDtypeStruct((m, n), out_dtype),
        compiler_params=_params(("arbitrary", "arbitrary")),
        name=name,
    )(tile_expert, tile_valid, x, *ws)


def _dense(x, ws, slab, tm, tn, out_dtype, name):
    tiles = x.shape[0] // tm
    return _gmm(x, ws, jnp.full((tiles,), slab, jnp.int32), jnp.ones((tiles,), jnp.int32),
                tm, tn, out_dtype, name)


def _sink_rows(sink_ref, g, rows, per):
    r = lax.broadcasted_iota(jnp.int32, (rows, 1), 0)
    sk = jnp.full((rows, 1), sink_ref[g * Q_PER_KV + Q_PER_KV - 1], F32)
    for h in range(Q_PER_KV - 2, -1, -1):
        sk = jnp.where(r < (h + 1) * per, sink_ref[g * Q_PER_KV + h], sk)
    return sk


def _attn_ctx_kernel(*refs):
    sink_ref, q_ref, k_ref, v_ref = refs[:4]
    o_ref, ko_ref, vo_ref = refs[-3:]
    g = pl.program_id(1)
    scale = HEAD_DIM ** -0.5
    k = k_ref[...]
    v = v_ref[...]
    ko_ref[...] = k
    vo_ref[...] = v
    q = q_ref[...]
    qs = jnp.concatenate([q[:, r * HEAD_DIM:(r + 1) * HEAD_DIM] for r in range(Q_PER_KV)], axis=0)
    s = _dot_nt(qs.astype(BF16), k.astype(BF16)) * scale
    sk = _sink_rows(sink_ref, g, Q_PER_KV * SEQ, SEQ)
    m = jnp.maximum(jnp.max(s, axis=-1, keepdims=True), sk)
    e = jnp.exp(s - m)
    den = jnp.sum(e, axis=-1, keepdims=True) + jnp.exp(sk - m)
    o = jnp.dot(e.astype(BF16), v.astype(BF16), preferred_element_type=F32) / den
    o_ref[...] = jnp.concatenate([o[r * SEQ:(r + 1) * SEQ] for r in range(Q_PER_KV)],
                                 axis=1).astype(o_ref.dtype)


def _attn_ctx(p, sink, layer, kbuf, vbuf):
    kv_spec = pl.BlockSpec((None, None, SEQ, HEAD_DIM), lambda b, g: (b, layer, 0, g))
    kv_shape = jax.ShapeDtypeStruct((BATCH, DEPTH, SEQ, ATT_KV_HEADS * HEAD_DIM), F32)
    in_specs = [
        pl.BlockSpec(memory_space=pltpu.SMEM),
        pl.BlockSpec((SEQ, Q_PER_KV * HEAD_DIM), lambda b, g: (b, g)),
        pl.BlockSpec((SEQ, HEAD_DIM), lambda b, g: (b, COL_K + g)),
        pl.BlockSpec((SEQ, HEAD_DIM), lambda b, g: (b, COL_V + g)),
    ]
    args = [sink, p, p, p]
    aliases = {}
    if kbuf is not None:
        in_specs += [ANY_SPEC, ANY_SPEC]
        args += [kbuf, vbuf]
        aliases = {4: 1, 5: 2}
    return pl.pallas_call(
        _attn_ctx_kernel,
        grid=(BATCH, ATT_KV_HEADS),
        in_specs=in_specs,
        out_specs=[pl.BlockSpec((SEQ, Q_PER_KV * HEAD_DIM), lambda b, g: (b, g)), kv_spec, kv_spec],
        out_shape=[jax.ShapeDtypeStruct((N_TOK, D_MODEL), BF16), kv_shape, kv_shape],
        input_output_aliases=aliases,
        compiler_params=_params(("arbitrary", "arbitrary")),
        name="attn_ctx",
    )(*args)


def _rope(x, c, sa, sb):
    return x * c + pltpu.roll(x, 96, 1) * sa + pltpu.roll(x, 32, 1) * sb


def _attn_lat_kernel(sink_ref, q_ref, k_ref, v_ref, kc_ref, vc_ref, cq_ref, sqa_ref, sqb_ref,
                     ck_ref, ska_ref, skb_ref, mix_ref, o_ref, kr_s, vb_s):
    del mix_ref
    g = pl.program_id(1)
    qb = pl.program_id(2)
    scale = HEAD_DIM ** -0.5
    band = 3 * ATT_BLOCK

    @pl.when(qb == 0)
    def _():
        kr_s[...] = _rope(k_ref[...], ck_ref[...], ska_ref[...], skb_ref[...]).astype(BF16)
        vb_s[...] = v_ref[...].astype(BF16)

    q = q_ref[...]
    cq, sqa, sqb = cq_ref[...], sqa_ref[...], sqb_ref[...]
    qs = jnp.concatenate(
        [_rope(q[:, r * HEAD_DIM:(r + 1) * HEAD_DIM], cq, sqa, sqb) for r in range(Q_PER_KV)],
        axis=0).astype(BF16)
    rows = Q_PER_KV * ATT_BLOCK
    start = pl.multiple_of(jnp.clip(qb - 1, 0, DEC_SEQ // ATT_BLOCK - 3) * ATT_BLOCK, ATT_BLOCK)
    kb = kr_s[pl.ds(start, band), :]
    vb = vb_s[pl.ds(start, band), :]
    s_loc = _dot_nt(qs, kb) * scale
    kpos = start + lax.broadcasted_iota(jnp.int32, (rows, band), 1)
    qpos = qb * ATT_BLOCK + (lax.broadcasted_iota(jnp.int32, (rows, band), 0) & (ATT_BLOCK - 1))
    s_loc = jnp.where(jnp.abs(kpos - qpos) <= WINDOW, s_loc, -jnp.inf)
    s_ctx = _dot_nt(qs, kc_ref[...]) * scale
    sk = _sink_rows(sink_ref, g, rows, ATT_BLOCK)
    m = jnp.maximum(jnp.maximum(jnp.max(s_loc, axis=-1, keepdims=True),
                                jnp.max(s_ctx, axis=-1, keepdims=True)), sk)
    e_loc = jnp.exp(s_loc - m)
    e_ctx = jnp.exp(s_ctx - m)
    den = (jnp.sum(e_loc, axis=-1, keepdims=True) + jnp.sum(e_ctx, axis=-1, keepdims=True)
           + jnp.exp(sk - m))
    o = (jnp.dot(e_loc.astype(BF16), vb, preferred_element_type=F32)
         + jnp.dot(e_ctx.astype(BF16), vc_ref[...], preferred_element_type=F32)) / den
    o_ref[...] = jnp.concatenate([o[r * ATT_BLOCK:(r + 1) * ATT_BLOCK] for r in range(Q_PER_KV)],
                                 axis=1).astype(o_ref.dtype)


def _attn_lat(p, sink, kc, vc, layer, rope, mix):
    nqb = DEC_SEQ // ATT_BLOCK
    row0 = N_PROMPT // DEC_SEQ
    qrow0 = N_PROMPT // ATT_BLOCK
    tq = pl.BlockSpec((ATT_BLOCK, HEAD_DIM), lambda b, g, qb: (qb, 0))
    tk = pl.BlockSpec((DEC_SEQ, HEAD_DIM), lambda b, g, qb: (0, 0))
    ctx = pl.BlockSpec((None, None, None, PAST_LEN, HEAD_DIM), lambda b, g, qb: (layer, b, g, 0, 0))
    cos, sin_a, sin_b = rope
    return pl.pallas_call(
        _attn_lat_kernel,
        grid=(DEC_BATCH, ATT_KV_HEADS, nqb),
        in_specs=[
            pl.BlockSpec(memory_space=pltpu.SMEM),
            pl.BlockSpec((ATT_BLOCK, Q_PER_KV * HEAD_DIM), lambda b, g, qb: (qrow0 + b * nqb + qb, g)),
            pl.BlockSpec((DEC_SEQ, HEAD_DIM), lambda b, g, qb: (row0 + b, COL_K + g)),
            pl.BlockSpec((DEC_SEQ, HEAD_DIM), lambda b, g, qb: (row0 + b, COL_V + g)),
            ctx, ctx, tq, tq, tq, tk, tk, tk, ANY_SPEC,
        ],
        out_specs=pl.BlockSpec((ATT_BLOCK, Q_PER_KV * HEAD_DIM),
                               lambda b, g, qb: (qrow0 + b * nqb + qb, g)),
        out_shape=jax.ShapeDtypeStruct((N_TOK, D_MODEL), BF16),
        scratch_shapes=[pltpu.VMEM((DEC_SEQ, HEAD_DIM), BF16), pltpu.VMEM((DEC_SEQ, HEAD_DIM), BF16)],
        input_output_aliases={12: 0},
        compiler_params=_params(("arbitrary", "arbitrary", "arbitrary")),
        name="attn_lat",
    )(sink, p, p, p, kc, vc, cos, sin_a, sin_b, cos, sin_a, sin_b, mix)


def _rope_tables():
    rows = DEC_SEQ // GRID_W
    row = np.repeat(np.arange(rows), GRID_W).astype(np.float32)
    col = np.tile(np.arange(GRID_W), rows).astype(np.float32)
    axis_dim = HEAD_DIM // 2
    inv = (ROPE_BASE ** (-np.arange(0, axis_dim, 2, dtype=np.float32) / axis_dim)).astype(np.float32)
    lane = np.arange(HEAD_DIM)
    pos = np.where((lane // axis_dim)[None, :] == 0, row[:, None], col[:, None])
    ang = pos * inv[lane % (axis_dim // 2)][None, :]
    first = ((lane % axis_dim) < axis_dim // 2)[None, :]
    cos = np.cos(ang).astype(np.float32)
    sin = np.sin(ang).astype(np.float32)
    sin_a = np.where(first, -sin, 0.0).astype(np.float32)
    sin_b = np.where(first, 0.0, sin).astype(np.float32)
    return jnp.asarray(cos), jnp.asarray(sin_a), jnp.asarray(sin_b)


def _level_tables():
    t = np.arange(CHUNK)
    x = t[:, None] ^ t[None, :]
    lvl = np.where(x > 0, np.floor(np.log2(np.maximum(x, 1))).astype(np.int32), -1)
    fwd = np.where(t[:, None] >= t[None, :], lvl, -2).astype(np.float32)
    tri = (t[:, None] >= t[None, :]).astype(np.float32)
    bit = np.stack([np.broadcast_to(((t >> j) & 1)[:, None], (CHUNK, HG_DIM)) for j in range(N_LEVELS)])
    qside = np.stack([bit, 1 - bit]).astype(np.float32)
    return (jnp.asarray(np.stack([fwd, fwd.T]), dtype=BF16), jnp.asarray(np.stack([tri, tri.T]), dtype=BF16),
            jnp.asarray(qside, dtype=BF16))


def _forget_gate(z, log_lb, log1m_lb):
    t = jnp.log(1.0 + jnp.exp(-jnp.abs(z)))
    c = log1m_lb + jnp.minimum(z, 0.0) - t
    log_f = jnp.maximum(log_lb, c) + jnp.log(1.0 + jnp.exp(-jnp.abs(log_lb - c)))
    return log_f, jnp.exp(log1m_lb - jnp.maximum(z, 0.0) - t)


def _hgrn_chunk(q, g, k, vb, state, tri, lvl, qside, rev):
    g1 = g.astype(BF16)
    r1 = g - g1.astype(F32)
    g2 = r1.astype(BF16)
    g3 = (r1 - g2.astype(F32)).astype(BF16)
    dot = functools.partial(jnp.dot, preferred_element_type=F32)
    b = (dot(tri, g1) + dot(tri, g2) + dot(tri, g3)) * LOG2E
    btot = b[0:1] if rev else b[CHUNK - 1:CHUNK]
    qb = q.astype(BF16)
    kb = k.astype(BF16)
    a = jnp.where(lvl == -1.0, _dot_nt(qb, kb).astype(BF16), jnp.zeros((), BF16))
    b8 = b.reshape(CHUNK // 8, 8, HG_DIM)
    sub = lax.broadcasted_iota(jnp.int32, b8.shape, 1)
    for j in range(N_LEVELS):
        half = 1 << j
        mid = half if rev else half - 1
        ask = qside(j) > 0.0
        if j == 0:
            x = jnp.where(ask, (q * (1.0 - k)).astype(BF16), kb)
        else:
            if j == 1:
                ref = jnp.where(sub < 4, b8[:, mid:mid + 1, :], b8[:, 4 + mid:5 + mid, :])
                dec = -jnp.abs(b8 - ref)
            elif j == 2:
                dec = -jnp.abs(b8 - b8[:, mid:mid + 1, :])
            else:
                bb = b.reshape(CHUNK // (2 * half), 2 * half, HG_DIM)
                ref = bb[:, mid:mid + 1, :]
                lo, hi = bb[:, :half, :], bb[:, half:, :]
                dec = jnp.concatenate([lo - ref, ref - hi] if rev else [ref - lo, hi - ref], axis=1)
            x = jnp.where(ask, qb, kb) * jnp.exp2(dec.reshape(CHUNK, HG_DIM).astype(BF16))
        a = jnp.where(lvl == float(j), _dot_nt(x, x).astype(BF16), a)
    qd = (q * jnp.exp2(b)).astype(BF16)
    o = dot(jnp.concatenate([qd, a], axis=1), jnp.concatenate([state.astype(BF16), vb], axis=0))
    kd = (k * jnp.exp2(btot - b)).astype(BF16)
    dcol = jnp.transpose(jnp.broadcast_to(jnp.exp2(btot), (HG_DIM, HG_DIM)))
    return o, state * dcol + _dot_tn(kd, vb)


def _hgrn_kernel(*refs, seq_len, has_init):
    refs = list(refs)
    hq_ref, hf_ref, hb_ref, hi_ref, hg_ref, lb_ref, gain_ref, tri_ref, lvl_ref, qs_ref = refs[:10]
    refs = refs[10:]
    init_ref = refs.pop(0) if has_init else None
    refs.pop(0)
    if not has_init and len(refs) == 6:
        refs.pop(0)
    out_ref = refs.pop(0)
    fin_ref = None if has_init else refs.pop(0)
    q_s, of_s, ob_s = refs
    n_chunks = seq_len // CHUNK

    q_s[...] = _silu(hq_ref[...])

    def load(ci, z_ref, lrow):
        sl = pl.ds(pl.multiple_of(ci * CHUNK, CHUNK), CHUNK)
        g, k = _forget_gate(z_ref[sl, :], lb_ref[lrow:lrow + 1, :], lb_ref[lrow + 1:lrow + 2, :])
        return sl, q_s[sl, :], g, k, hi_ref[sl, :].astype(BF16)

    def step(i, carry):
        s_f, s_b = carry
        sl, q, g, k, vb = load(i, hf_ref, 0)
        o, s_f = _hgrn_chunk(q, g, k, vb, s_f, tri_ref[0], lvl_ref[0], lambda j: qs_ref[0, j], False)
        of_s[sl, :] = o
        sl, q, g, k, vb = load(n_chunks - 1 - i, hb_ref, 2)
        o, s_b = _hgrn_chunk(q, g, k, vb, s_b, tri_ref[1], lvl_ref[1], lambda j: qs_ref[1, j], True)
        ob_s[sl, :] = o
        return s_f, s_b

    zero = jnp.zeros((HG_DIM, HG_DIM), F32)
    init = (init_ref[0], init_ref[1]) if has_init else (zero, zero)
    s_f, s_b = lax.fori_loop(0, n_chunks, step, init)
    if not has_init:
        fin_ref[0] = s_f
        fin_ref[1] = s_b

    def finish(ci, carry):
        sl = pl.ds(pl.multiple_of(ci * CHUNK, CHUNK), CHUNK)
        o = of_s[sl, :] + ob_s[sl, :]
        o = o * lax.rsqrt(jnp.mean(o * o, axis=-1, keepdims=True) + NORM_EPS) * gain_ref[...]
        out_ref[sl, :] = (o * _silu(hg_ref[sl, :])).astype(out_ref.dtype)
        return carry

    lax.fori_loop(0, n_chunks, finish, 0)


def _hgrn(p, lb_tab, gain, tables, seq_len, n_seq, row0, layer, mix, init=None, sbuf=None):
    lvl, tri, qside = tables
    has_init = init is not None

    def col(c):
        return pl.BlockSpec((seq_len, HG_DIM), lambda b, h: (row0 + b, c + h))

    in_specs = [col(COL_HQ), col(COL_HF), col(COL_HB), col(COL_HI), col(COL_HG),
                pl.BlockSpec((None, 4, HG_DIM), lambda b, h: (layer, 0, h)),
                pl.BlockSpec((1, HG_DIM), lambda b, h: (0, 0)),
                pl.BlockSpec((2, CHUNK, CHUNK), lambda b, h: (0, 0, 0)),
                pl.BlockSpec((2, CHUNK, CHUNK), lambda b, h: (0, 0, 0)),
                pl.BlockSpec((2, N_LEVELS, CHUNK, HG_DIM), lambda b, h: (0, 0, 0, 0))]
    args = [p, p, p, p, p, lb_tab, gain.reshape(1, HG_DIM), tri, lvl, qside]
    if has_init:
        in_specs.append(pl.BlockSpec((None, None, 2, None, HG_DIM, HG_DIM),
                                     lambda b, h: (b, layer, 0, h, 0, 0)))
        args.append(init)
    in_specs.append(ANY_SPEC)
    args.append(mix)
    aliases = {len(args) - 1: 0}
    out_specs = [pl.BlockSpec((seq_len, HG_DIM), lambda b, h: (row0 + b, MIX_HG + h))]
    out_shape = [jax.ShapeDtypeStruct((N_TOK, D_MODEL), BF16)]
    if not has_init:
        if sbuf is not None:
            in_specs.append(ANY_SPEC)
            args.append(sbuf)
            aliases[len(args) - 1] = 1
        out_specs.append(pl.BlockSpec((None, None, 2, None, HG_DIM, HG_DIM),
                                      lambda b, h: (b, layer, 0, h, 0, 0)))
        out_shape.append(jax.ShapeDtypeStruct((n_seq, DEPTH, 2, HG_HEADS, HG_DIM, HG_DIM), F32))
    return pl.pallas_call(
        functools.partial(_hgrn_kernel, seq_len=seq_len, has_init=has_init),
        grid=(n_seq, HG_HEADS),
        in_specs=in_specs,
        out_specs=out_specs,
        out_shape=out_shape,
        scratch_shapes=[pltpu.VMEM((seq_len, HG_DIM), F32)] * 3,
        input_output_aliases=aliases,
        compiler_params=_params(("arbitrary", "arbitrary")),
        name="hgrn_lat" if has_init else "hgrn_ctx",
    )(*args)


def _router_kernel(h_ref, r_ref, o_ref):
    lg = jnp.dot(h_ref[...], r_ref[...], preferred_element_type=F32)
    lane = lax.broadcasted_iota(jnp.int32, lg.shape, 1)
    lg = jnp.where(lane < N_EXPERTS, lg, -jnp.inf)
    m1 = jnp.max(lg, axis=-1, keepdims=True)
    i1 = jnp.min(jnp.where(lg == m1, lane, HEAD_DIM), axis=-1, keepdims=True)
    lg2 = jnp.where(lane == i1, -jnp.inf, lg)
    m2 = jnp.max(lg2, axis=-1, keepdims=True)
    i2 = jnp.min(jnp.where(lg2 == m2, lane, HEAD_DIM), axis=-1, keepdims=True)
    e2 = jnp.exp(m2 - m1)
    den = 1.0 + e2
    o_ref[...] = jnp.where(lane == 0, i1.astype(F32),
                           jnp.where(lane == 1, i2.astype(F32),
                                     jnp.where(lane == 2, 1.0 / den,
                                               jnp.where(lane == 3, e2 / den, 0.0))))


def _route(h, router, slab0):
    tm = 512
    rp = jnp.pad(router, ((0, 0), (0, HEAD_DIM - N_EXPERTS))).astype(BF16)
    out = pl.pallas_call(
        _router_kernel,
        grid=(N_TOK // tm,),
        in_specs=[pl.BlockSpec((tm, D_MODEL), lambda i: (i, 0)),
                  pl.BlockSpec((D_MODEL, HEAD_DIM), lambda i: (0, 0))],
        out_specs=pl.BlockSpec((tm, HEAD_DIM), lambda i: (i, 0)),
        out_shape=jax.ShapeDtypeStruct((N_TOK, HEAD_DIM), F32),
        compiler_params=_params(("arbitrary",)),
        name="router",
    )(h, rp)
    top_i = out[:, 0:2].astype(jnp.int32)
    gates = out[:, 2:4]
    e_flat = top_i.reshape(-1)
    onehot = (e_flat[:, None] == jnp.arange(N_EXPERTS)[None, :]).astype(jnp.int32)
    csum = jnp.cumsum(onehot, axis=0)
    counts = csum[-1]
    padded = ((counts + MOE_TM - 1) // MOE_TM) * MOE_TM
    ends = jnp.cumsum(padded)
    starts = ends - padded
    pos = jnp.sum(onehot * (csum - 1 + starts[None, :]), axis=1)
    row_token = jnp.zeros((MOE_ROWS,), jnp.int32).at[pos].set(jnp.arange(2 * N_TOK, dtype=jnp.int32) // 2)
    tile_start = jnp.arange(MOE_TILES, dtype=jnp.int32) * MOE_TM
    tile_expert = jnp.minimum(jnp.sum(tile_start[:, None] >= ends[None, :], axis=1), N_EXPERTS - 1)
    tile_valid = (tile_start < ends[-1]).astype(jnp.int32)
    return gates, pos.reshape(N_TOK, 2), row_token, (slab0 + tile_expert).astype(jnp.int32), tile_valid


def _rows(table, idx):
    return table.at[idx].get(mode="promise_in_bounds")


def kernel(x_prompt, x_sample, cache_k, cache_v, state_hgrn, c, c_ctx, w_mod, b_mod, norm_gains,
           w_in, w_out, attn_sink, hg_lb_logits, hg_norm_gain, ffn_w1, ffn_w3, ffn_w2,
           moe_router, moe_w1, moe_w3, moe_w2):
    x = jnp.concatenate([x_prompt.reshape(N_PROMPT, D_MODEL), x_sample.reshape(N_SAMPLE, D_MODEL)], axis=0)

    cvec = jnp.concatenate([c_ctx[None], c, jnp.zeros((3, D_MODEL), F32)], axis=0)
    mod = _modulation(cvec, w_mod, b_mod).reshape(DEPTH, 8, 6, D_MODEL)
    mod = jnp.pad(mod, ((0, 0), (0, 0), (0, 2), (0, 0)))
    group_src = np.array([0] * (N_PROMPT // MOD_GROUP) + list(range(1, 1 + DEC_BATCH)))
    mod = mod[:, group_src]

    lb_cum = jnp.cumsum(jax.nn.softmax(hg_lb_logits.astype(F32), axis=0), axis=0)
    lb = lb_cum - lb_cum[0:1]
    lb_tab = jnp.stack([jnp.log(lb[:, 0]), jnp.log1p(-lb[:, 0]),
                        jnp.log(lb[:, 1]), jnp.log1p(-lb[:, 1])], axis=1)

    rope = _rope_tables()
    tables = _level_tables()
    kc = cache_k.transpose(1, 0, 3, 2, 4).astype(BF16)
    vc = cache_v.transpose(1, 0, 3, 2, 4).astype(BF16)
    w_in_b, w_out_b = w_in.astype(BF16), w_out.astype(BF16)
    ffn_b = [w.astype(BF16) for w in (ffn_w1, ffn_w3, ffn_w2)]
    moe_b = [w.astype(BF16).reshape((-1,) + w.shape[2:]) for w in (moe_w1, moe_w3, moe_w2)]

    (h,) = _resnorm(x, [], None, None, 0, norm_gains[0, 0], mod[0], 1, 0)
    kbuf = vbuf = sbuf = None
    for l in range(DEPTH):
        i = l // 2
        moe = l % 2 == 1
        p = _dense(h, [w_in_b], l, 512, 1664, F32, "proj_in")
        mix, kbuf, vbuf = _attn_ctx(p, attn_sink[l], l, kbuf, vbuf)
        mix = _attn_lat(p, attn_sink[l], kc, vc, l, rope, mix)
        mix, sbuf = _hgrn(p, lb_tab, hg_norm_gain[l], tables, SEQ, BATCH, 0, l, mix, sbuf=sbuf)
        (mix,) = _hgrn(p, lb_tab, hg_norm_gain[l], tables, DEC_SEQ, DEC_BATCH, N_PROMPT // DEC_SEQ,
                       l, mix, init=state_hgrn)
        m = _dense(mix, [w_out_b], l, 512, 1024, F32, "proj_out")
        outs = _resnorm(x, [m], norm_gains[l, 1], mod[l], 2, norm_gains[l, 2], mod[l], 4, 3, wide_h=moe)
        x, h = outs[0], outs[1]
        if not moe:
            act = _dense(h, ffn_b[:2], i, 512, 1408, BF16, "ffn_up")
            f = [_dense(act, ffn_b[2:], i, 512, 1024, F32, "ffn_down")]
            gates = None
        else:
            gates, pos, row_token, tile_slab, tile_valid = _route(h, moe_router[i], i * N_EXPERTS)
            xs = _rows(outs[2], row_token)
            act = _gmm(xs, moe_b[:2], tile_slab, tile_valid, MOE_TM, 1408, BF16, "moe_up")
            ys = _gmm(act, moe_b[2:], tile_slab, tile_valid, MOE_TM, 1024, F32, "moe_down")
            f = [_rows(ys, pos[:, 0]), _rows(ys, pos[:, 1])]
        if l + 1 < DEPTH:
            x, h = _resnorm(x, f, norm_gains[l, 3], mod[l], 5, norm_gains[l + 1, 0], mod[l + 1], 1, 0,
                            gates=gates)
        else:
            (x,) = _resnorm(x, f, norm_gains[l, 3], mod[l], 5, None, None, 0, 0, gates=gates)
    kv_shape = (BATCH, DEPTH, SEQ, ATT_KV_HEADS, HEAD_DIM)
    return (x[:N_PROMPT].reshape(BATCH, SEQ, D_MODEL), x[N_PROMPT:].reshape(DEC_BATCH, DEC_SEQ, D_MODEL),
            kbuf.reshape(kv_shape), vbuf.reshape(kv_shape), sbuf)
```

```python
import functools

import numpy as np
import jax
import jax.numpy as jnp
from jax import lax
from jax.experimental import pallas as pl
from jax.experimental.pallas import tpu as pltpu

F32 = jnp.float32
BF16 = jnp.bfloat16

D_MODEL = 2048
BATCH = 16
SEQ = 256
DEPTH = 4
DEC_BATCH = 4
DEC_SEQ = 1024
PAST_LEN = 256
GRID_W = 64
HEAD_DIM = 128
ATT_HEADS = 8
ATT_KV_HEADS = 2
Q_PER_KV = 4
ATT_WIDTH = 1024
WINDOW = 128
ATT_BLOCK = 128
ROPE_BASE = 10000.0
HG_WIDTH = 1024
HG_HEADS = 8
HG_DIM = 128
N_EXPERTS = 8
D_FF = 5632
D_FF_EXPERT = 2816
NORM_EPS = 1e-6
IN_COLS = 6656

N_PROMPT = BATCH * SEQ
N_SAMPLE = DEC_BATCH * DEC_SEQ
N_TOK = N_PROMPT + N_SAMPLE
MOD_GROUP = 1024
N_GROUPS = N_TOK // MOD_GROUP

COL_K = 8
COL_V = 10
COL_HQ = 12
COL_HF = 20
COL_HB = 28
COL_HI = 36
COL_HG = 44
MIX_HG = ATT_WIDTH // HG_DIM

CHUNK = 128
N_LEVELS = 7
LOG2E = 1.4426950408889634

VMEM_LIMIT = 56 * 1024 * 1024

MOE_TM = 256
MOE_ROWS = 2 * N_TOK + N_EXPERTS * MOE_TM
MOE_TILES = MOE_ROWS // MOE_TM

ANY_SPEC = pl.BlockSpec(memory_space=pl.ANY)


def _params(sem):
    return pltpu.CompilerParams(dimension_semantics=sem, vmem_limit_bytes=VMEM_LIMIT)


def _silu(x):
    return x / (1.0 + jnp.exp(-x))


def _dot_nt(a, b):
    return lax.dot_general(a, b, (((1,), (1,)), ((), ())), preferred_element_type=F32)


def _dot_tn(a, b):
    return lax.dot_general(a, b, (((0,), (0,)), ((), ())), preferred_element_type=F32)


def _mod_kernel(c_ref, w_ref, b_ref, o_ref):
    s = _silu(c_ref[...]).astype(BF16)
    o_ref[...] = jnp.dot(s, w_ref[...].astype(BF16), preferred_element_type=F32) + b_ref[...]


def _modulation(cvec, w_mod, b_mod):
    tn = 1024
    n = 6 * D_MODEL
    return pl.pallas_call(
        _mod_kernel,
        grid=(DEPTH, n // tn),
        in_specs=[
            pl.BlockSpec((8, D_MODEL), lambda l, j: (0, 0)),
            pl.BlockSpec((None, D_MODEL, tn), lambda l, j: (l, 0, j)),
            pl.BlockSpec((None, 1, tn), lambda l, j: (l, 0, j)),
        ],
        out_specs=pl.BlockSpec((None, 8, tn), lambda l, j: (l, 0, j)),
        out_shape=jax.ShapeDtypeStruct((DEPTH, 8, n), F32),
        compiler_params=_params(("arbitrary", "arbitrary")),
        name="modulation",
    )(cvec, w_mod, b_mod.reshape(DEPTH, 1, n))


def _rms(x, gain):
    return x * lax.rsqrt(jnp.mean(x * x, axis=-1, keepdims=True) + NORM_EPS) * gain


def _resnorm_kernel(*refs, n_m, gate_row, has_h, sc_row, sh_row, wide_h):
    refs = list(refs)
    x_ref = refs.pop(0)
    x = x_ref[...]
    if n_m == 1:
        m = refs.pop(0)[...]
    elif n_m == 2:
        y0 = refs.pop(0)[...]
        y1 = refs.pop(0)[...]
        gt = refs.pop(0)[...]
        m = gt[:, 0:1] * y0 + gt[:, 1:2] * y1
    if n_m:
        ga = refs.pop(0)[...]
        modg = refs.pop(0)
        x = x + modg[gate_row:gate_row + 1, :] * _rms(m, ga)
    if has_h:
        gb = refs.pop(0)[...]
        modh = refs.pop(0)
        h = _rms(x, gb) * (1.0 + modh[sc_row:sc_row + 1, :]) + modh[sh_row:sh_row + 1, :]
    if n_m:
        refs.pop(0)[...] = x
    if has_h:
        refs.pop(0)[...] = h.astype(BF16)
        if wide_h:
            refs.pop(0)[...] = h


def _resnorm(x, ms, gain_m, mod_gate, gate_row, gain_h, mod_h, sc_row, sh_row, gates=None, wide_h=False):
    tm = 256
    per = MOD_GROUP // tm
    n_m = len(ms)
    has_h = gain_h is not None
    row = pl.BlockSpec((tm, D_MODEL), lambda i: (i, 0))
    vec = pl.BlockSpec((1, D_MODEL), lambda i: (0, 0))
    mod = pl.BlockSpec((None, 8, D_MODEL), lambda i: (i // per, 0, 0))
    args, specs = [x], [row]
    for m in ms:
        args.append(m)
        specs.append(row)
    if n_m == 2:
        args.append(gates)
        specs.append(pl.BlockSpec((tm, 2), lambda i: (i, 0)))
    if n_m:
        args += [gain_m.reshape(1, D_MODEL), mod_gate]
        specs += [vec, mod]
    if has_h:
        args += [gain_h.reshape(1, D_MODEL), mod_h]
        specs += [vec, mod]
    out_shape, out_specs = [], []
    if n_m:
        out_shape.append(jax.ShapeDtypeStruct((N_TOK, D_MODEL), F32))
        out_specs.append(row)
    if has_h:
        out_shape.append(jax.ShapeDtypeStruct((N_TOK, D_MODEL), BF16))
        out_specs.append(row)
        if wide_h:
            out_shape.append(jax.ShapeDtypeStruct((N_TOK, D_MODEL), F32))
            out_specs.append(row)
    outs = pl.pallas_call(
        functools.partial(_resnorm_kernel, n_m=n_m, gate_row=gate_row, has_h=has_h,
                          sc_row=sc_row, sh_row=sh_row, wide_h=wide_h),
        grid=(N_TOK // tm,),
        in_specs=specs,
        out_specs=out_specs,
        out_shape=out_shape,
        compiler_params=_params(("arbitrary",)),
        name="resnorm",
    )(*args)
    return outs


def _gmm_kernel(te_ref, tv_ref, x_ref, w_ref, o_ref):
    del te_ref
    valid = tv_ref[pl.program_id(1)] != 0

    @pl.when(valid)
    def _():
        o_ref[...] = jnp.dot(x_ref[...].astype(BF16), w_ref[...],
                             preferred_element_type=F32).astype(o_ref.dtype)

    @pl.when(jnp.logical_not(valid))
    def _():
        o_ref[...] = jnp.zeros_like(o_ref)


def _gmm_swiglu_kernel(te_ref, tv_ref, x_ref, w1_ref, w3_ref, o_ref):
    del te_ref
    valid = tv_ref[pl.program_id(1)] != 0

    @pl.when(valid)
    def _():
        x = x_ref[...].astype(BF16)
        a = jnp.dot(x, w1_ref[...], preferred_element_type=F32)
        b = jnp.dot(x, w3_ref[...], preferred_element_type=F32)
        o_ref[...] = (_silu(a) * b).astype(o_ref.dtype)

    @pl.when(jnp.logical_not(valid))
    def _():
        o_ref[...] = jnp.zeros_like(o_ref)


def _gmm(x, ws, tile_expert, tile_valid, tm, tn, out_dtype, name):
    m, k = x.shape
    n = ws[0].shape[-1]
    x_spec = pl.BlockSpec((tm, k), lambda j, i, te, tv: (i, 0))
    w_spec = pl.BlockSpec((None, k, tn), lambda j, i, te, tv: (te[i], 0, j))
    o_spec = pl.BlockSpec((tm, tn), lambda j, i, te, tv: (i, j))
    body = _gmm_kernel if len(ws) == 1 else _gmm_swiglu_kernel
    return pl.pallas_call(
        body,
        grid_spec=pltpu.PrefetchScalarGridSpec(
            num_scalar_prefetch=2,
            grid=(n // tn, m // tm),
            in_specs=[x_spec] + [w_spec] * len(ws),
            out_specs=o_spec,
        ),
        out_shape=jax.ShapeDtypeStruct((m, n), out_dtype),
        compiler_params=_params(("arbitrary", "arbitrary")),
        name=name,
    )(tile_expert, tile_valid, x, *ws)


def _dense(x, ws, slab, tm, tn, out_dtype, name):
    tiles = x.shape[0] // tm
    return _gmm(x, ws, jnp.full((tiles,), slab, jnp.int32), jnp.ones((tiles,), jnp.int32),
                tm, tn, out_dtype, name)


def _sink_rows(sink_ref, g, rows, per):
    r = lax.broadcasted_iota(jnp.int32, (rows, 1), 0)
    sk = jnp.full((rows, 1), sink_ref[g * Q_PER_KV + Q_PER_KV - 1], F32)
    for h in range(Q_PER_KV - 2, -1, -1):
        sk = jnp.where(r < (h + 1) * per, sink_ref[g * Q_PER_KV + h], sk)
    return sk


def _attn_ctx_kernel(*refs):
    sink_ref, q_ref, k_ref, v_ref = refs[:4]
    o_ref, ko_ref, vo_ref = refs[-3:]
    g = pl.program_id(1)
    scale = HEAD_DIM ** -0.5
    k = k_ref[...]
    v = v_ref[...]
    ko_ref[...] = k
    vo_ref[...] = v
    q = q_ref[...]
    qs = jnp.concatenate([q[:, r * HEAD_DIM:(r + 1) * HEAD_DIM] for r in range(Q_PER_KV)], axis=0)
    s = _dot_nt(qs.astype(BF16), k.astype(BF16)) * scale
    sk = _sink_rows(sink_ref, g, Q_PER_KV * SEQ, SEQ)
    m = jnp.maximum(jnp.max(s, axis=-1, keepdims=True), sk)
    e = jnp.exp(s - m)
    den = jnp.sum(e, axis=-1, keepdims=True) + jnp.exp(sk - m)
    o = jnp.dot(e.astype(BF16), v.astype(BF16), preferred_element_type=F32) / den
    o_ref[...] = jnp.concatenate([o[r * SEQ:(r + 1) * SEQ] for r in range(Q_PER_KV)],
                                 axis=1).astype(o_ref.dtype)


def _attn_ctx(p, sink, layer, kbuf, vbuf):
    kv_spec = pl.BlockSpec((None, None, SEQ, HEAD_DIM), lambda b, g: (b, layer, 0, g))
    kv_shape = jax.ShapeDtypeStruct((BATCH, DEPTH, SEQ, ATT_KV_HEADS * HEAD_DIM), F32)
    in_specs = [
        pl.BlockSpec(memory_space=pltpu.SMEM),
        pl.BlockSpec((SEQ, Q_PER_KV * HEAD_DIM), lambda b, g: (b, g)),
        pl.BlockSpec((SEQ, HEAD_DIM), lambda b, g: (b, COL_K + g)),
        pl.BlockSpec((SEQ, HEAD_DIM), lambda b, g: (b, COL_V + g)),
    ]
    args = [sink, p, p, p]
    aliases = {}
    if kbuf is not None:
        in_specs += [ANY_SPEC, ANY_SPEC]
        args += [kbuf, vbuf]
        aliases = {4: 1, 5: 2}
    return pl.pallas_call(
        _attn_ctx_kernel,
        grid=(BATCH, ATT_KV_HEADS),
        in_specs=in_specs,
        out_specs=[pl.BlockSpec((SEQ, Q_PER_KV * HEAD_DIM), lambda b, g: (b, g)), kv_spec, kv_spec],
        out_shape=[jax.ShapeDtypeStruct((N_TOK, D_MODEL), BF16), kv_shape, kv_shape],
        input_output_aliases=aliases,
        compiler_params=_params(("arbitrary", "arbitrary")),
        name="attn_ctx",
    )(*args)


def _rope(x, c, sa, sb):
    return x * c + pltpu.roll(x, 96, 1) * sa + pltpu.roll(x, 32, 1) * sb


def _attn_lat_kernel(sink_ref, q_ref, k_ref, v_ref, kc_ref, vc_ref, cq_ref, sqa_ref, sqb_ref,
                     ck_ref, ska_ref, skb_ref, mix_ref, o_ref, kr_s, vb_s):
    del mix_ref
    g = pl.program_id(1)
    qb = pl.program_id(2)
    scale = HEAD_DIM ** -0.5
    band = 3 * ATT_BLOCK

    @pl.when(qb == 0)
    def _():
        kr_s[...] = _rope(k_ref[...], ck_ref[...], ska_ref[...], skb_ref[...]).astype(BF16)
        vb_s[...] = v_ref[...].astype(BF16)

    q = q_ref[...]
    cq, sqa, sqb = cq_ref[...], sqa_ref[...], sqb_ref[...]
    qs = jnp.concatenate(
        [_rope(q[:, r * HEAD_DIM:(r + 1) * HEAD_DIM], cq, sqa, sqb) for r in range(Q_PER_KV)],
        axis=0).astype(BF16)
    rows = Q_PER_KV * ATT_BLOCK
    start = pl.multiple_of(jnp.clip(qb - 1, 0, DEC_SEQ // ATT_BLOCK - 3) * ATT_BLOCK, ATT_BLOCK)
    kb = kr_s[pl.ds(start, band), :]
    vb = vb_s[pl.ds(start, band), :]
    s_loc = _dot_nt(qs, kb) * scale
    kpos = start + lax.broadcasted_iota(jnp.int32, (rows, band), 1)
    qpos = qb * ATT_BLOCK + (lax.broadcasted_iota(jnp.int32, (rows, band), 0) & (ATT_BLOCK - 1))
    s_loc = jnp.where(jnp.abs(kpos - qpos) <= WINDOW, s_loc, -jnp.inf)
    s_ctx = _dot_nt(qs, kc_ref[...]) * scale
    sk = _sink_rows(sink_ref, g, rows, ATT_BLOCK)
    m = jnp.maximum(jnp.maximum(jnp.max(s_loc, axis=-1, keepdims=True),
                                jnp.max(s_ctx, axis=-1, keepdims=True)), sk)
    e_loc = jnp.exp(s_loc - m)
    e_ctx = jnp.exp(s_ctx - m)
    den = (jnp.sum(e_loc, axis=-1, keepdims=True) + jnp.sum(e_ctx, axis=-1, keepdims=True)
           + jnp.exp(sk - m))
    o = (jnp.dot(e_loc.astype(BF16), vb, preferred_element_type=F32)
         + jnp.dot(e_ctx.astype(BF16), vc_ref[...], preferred_element_type=F32)) / den
    o_ref[...] = jnp.concatenate([o[r * ATT_BLOCK:(r + 1) * ATT_BLOCK] for r in range(Q_PER_KV)],
                                 axis=1).astype(o_ref.dtype)


def _attn_lat(p, sink, kc, vc, layer, rope, mix):
    nqb = DEC_SEQ // ATT_BLOCK
    row0 = N_PROMPT // DEC_SEQ
    qrow0 = N_PROMPT // ATT_BLOCK
    tq = pl.BlockSpec((ATT_BLOCK, HEAD_DIM), lambda b, g, qb: (qb, 0))
    tk = pl.BlockSpec((DEC_SEQ, HEAD_DIM), lambda b, g, qb: (0, 0))
    ctx = pl.BlockSpec((None, None, None, PAST_LEN, HEAD_DIM), lambda b, g, qb: (layer, b, g, 0, 0))
    cos, sin_a, sin_b = rope
    return pl.pallas_call(
        _attn_lat_kernel,
        grid=(DEC_BATCH, ATT_KV_HEADS, nqb),
        in_specs=[
            pl.BlockSpec(memory_space=pltpu.SMEM),
            pl.BlockSpec((ATT_BLOCK, Q_PER_KV * HEAD_DIM), lambda b, g, qb: (qrow0 + b * nqb + qb, g)),
            pl.BlockSpec((DEC_SEQ, HEAD_DIM), lambda b, g, qb: (row0 + b, COL_K + g)),
            pl.BlockSpec((DEC_SEQ, HEAD_DIM), lambda b, g, qb: (row0 + b, COL_V + g)),
            ctx, ctx, tq, tq, tq, tk, tk, tk, ANY_SPEC,
        ],
        out_specs=pl.BlockSpec((ATT_BLOCK, Q_PER_KV * HEAD_DIM),
                               lambda b, g, qb: (qrow0 + b * nqb + qb, g)),
        out_shape=jax.ShapeDtypeStruct((N_TOK, D_MODEL), BF16),
        scratch_shapes=[pltpu.VMEM((DEC_SEQ, HEAD_DIM), BF16), pltpu.VMEM((DEC_SEQ, HEAD_DIM), BF16)],
        input_output_aliases={12: 0},
        compiler_params=_params(("arbitrary", "arbitrary", "arbitrary")),
        name="attn_lat",
    )(sink, p, p, p, kc, vc, cos, sin_a, sin_b, cos, sin_a, sin_b, mix)


def _rope_tables():
    rows = DEC_SEQ // GRID_W
    row = np.repeat(np.arange(rows), GRID_W).astype(np.float32)
    col = np.tile(np.arange(GRID_W), rows).astype(np.float32)
    axis_dim = HEAD_DIM // 2
    inv = (ROPE_BASE ** (-np.arange(0, axis_dim, 2, dtype=np.float32) / axis_dim)).astype(np.float32)
    lane = np.arange(HEAD_DIM)
    pos = np.where((lane // axis_dim)[None, :] == 0, row[:, None], col[:, None])
    ang = pos * inv[lane % (axis_dim // 2)][None, :]
    first = ((lane % axis_dim) < axis_dim // 2)[None, :]
    cos = np.cos(ang).astype(np.float32)
    sin = np.sin(ang).astype(np.float32)
    sin_a = np.where(first, -sin, 0.0).astype(np.float32)
    sin_b = np.where(first, 0.0, sin).astype(np.float32)
    return jnp.asarray(cos), jnp.asarray(sin_a), jnp.asarray(sin_b)


def _level_tables():
    t = np.arange(CHUNK)
    x = t[:, None] ^ t[None, :]
    lvl = np.where(x > 0, np.floor(np.log2(np.maximum(x, 1))), -1.0).astype(np.float32)
    tri = (t[:, None] >= t[None, :]).astype(np.float32)
    return jnp.asarray(lvl), jnp.asarray(np.stack([tri, tri.T]), dtype=BF16)


def _forget_gate(z, log_lb, log1m_lb):
    t = jnp.log(1.0 + jnp.exp(-jnp.abs(z)))
    c = log1m_lb + jnp.minimum(z, 0.0) - t
    log_f = jnp.maximum(log_lb, c) + jnp.log(1.0 + jnp.exp(-jnp.abs(log_lb - c)))
    return log_f, jnp.exp(log1m_lb - jnp.maximum(z, 0.0) - t)


def _running_log2_decay(g, tri):
    g1 = g.astype(BF16)
    g2 = (g - g1.astype(F32)).astype(BF16)
    dot = functools.partial(jnp.dot, preferred_element_type=F32)
    return (dot(tri, g1) + dot(tri, g2)) * LOG2E


def _intra_chunk_weights(qs, kfs, kbs, bfs, bbs, lvl):
    n = len(qs)
    acc = []
    for c in range(n):
        gram = _dot_nt(qs[c].astype(BF16), (kfs[c] + kbs[c]).astype(BF16))
        acc.append(jnp.where(lvl == -1.0, gram, 0.0))
    shape8 = (CHUNK // 8, 8, HG_DIM)
    sub = lax.broadcasted_iota(jnp.int32, shape8, 1)
    for j in range(N_LEVELS):
        half = 1 << j
        for c in range(n):
            q, kf, kb, bf, bb = qs[c], kfs[c], kbs[c], bfs[c], bbs[c]
            if j < 3:
                q, kf, kb, bf, bb = (a.reshape(shape8) for a in (q, kf, kb, bf, bb))
                upper = (sub & half) != 0
                if j == 0:
                    lhs = q * jnp.where(upper, 1.0 - kf, 1.0 - kb)
                    rhs = jnp.where(upper, kb, kf)
                else:
                    if j == 1:
                        ref_f = jnp.where(sub < 4, bf[:, 1:2, :], bf[:, 5:6, :])
                        ref_b = jnp.where(sub < 4, bb[:, 2:3, :], bb[:, 6:7, :])
                    else:
                        ref_f, ref_b = bf[:, 3:4, :], bb[:, 4:5, :]
                    lhs = q * jnp.exp2(jnp.where(upper, bf - ref_f, bb - ref_b))
                    rhs = jnp.where(upper, kb, kf) * jnp.exp2(jnp.where(upper, ref_b - bb, ref_f - bf))
            else:
                shape = (CHUNK // (2 * half), 2 * half, HG_DIM)
                q, kf, kb, bf, bb = (a.reshape(shape) for a in (q, kf, kb, bf, bb))
                ref_f, ref_b = bf[:, half - 1:half, :], bb[:, half:half + 1, :]
                lo, up = slice(0, half), slice(half, 2 * half)
                lhs = q * jnp.exp2(jnp.concatenate([bb[:, lo] - ref_b, bf[:, up] - ref_f], axis=1))
                rhs = (jnp.concatenate([kf[:, lo], kb[:, up]], axis=1)
                       * jnp.exp2(jnp.concatenate([ref_f - bf[:, lo], ref_b - bb[:, up]], axis=1)))
            gram = _dot_nt(lhs.reshape(CHUNK, HG_DIM).astype(BF16), rhs.reshape(CHUNK, HG_DIM).astype(BF16))
            acc[c] = jnp.where(lvl == float(j), gram, acc[c])
    return acc


def _hgrn_kernel(*refs, seq_len, has_init, has_sbuf):
    refs = list(refs)
    hq_ref, hf_ref, hb_ref, hi_ref, hg_ref, lb_ref, gain_ref, tri_ref, lvl_ref = refs[:9]
    refs = refs[9:]
    init_ref = refs.pop(0) if has_init else None
    refs.pop(0)
    if has_sbuf:
        refs.pop(0)
    out_ref = refs.pop(0)
    fin_ref = None if has_init else refs.pop(0)
    q_s, vb_s, op_s, qdb_s, ub_s, dcb_s, sf_s = refs
    n_chunks = seq_len // CHUNK
    pair = 2
    dot = functools.partial(jnp.dot, preferred_element_type=F32)

    q_s[...] = _silu(hq_ref[...])
    vb_s[...] = hi_ref[...].astype(BF16)

    def chunk_slice(ci):
        return pl.ds(pl.multiple_of(ci * CHUNK, CHUNK), CHUNK)

    def fwd_step(i, carry):
        sls = [chunk_slice(pair * i + c) for c in range(pair)]
        lvl = lvl_ref[...]
        gf = [_forget_gate(hf_ref[sl, :], lb_ref[0:1, :], lb_ref[1:2, :]) for sl in sls]
        gb = [_forget_gate(hb_ref[sl, :], lb_ref[2:3, :], lb_ref[3:4, :]) for sl in sls]
        bfs = [_running_log2_decay(g, tri_ref[0]) for g, _ in gf]
        bbs = [_running_log2_decay(g, tri_ref[1]) for g, _ in gb]
        qs = [q_s[sl, :] for sl in sls]
        kfs = [k for _, k in gf]
        kbs = [k for _, k in gb]
        acc = _intra_chunk_weights(qs, kfs, kbs, bfs, bbs, lvl)
        for c in range(pair):
            sl, q, bf, bb = sls[c], qs[c], bfs[c], bbs[c]
            vb = vb_s[sl, :]
            s_f = sf_s[...]
            qd = (q * jnp.exp2(bf)).astype(BF16)
            op_s[sl, :] = dot(jnp.concatenate([qd, acc[c].astype(BF16)], axis=1),
                              jnp.concatenate([s_f.astype(BF16), vb], axis=0))
            tot_f = bf[CHUNK - 1:CHUNK]
            kd = (kfs[c] * jnp.exp2(tot_f - bf)).astype(BF16)
            dcol = jnp.transpose(jnp.broadcast_to(jnp.exp2(tot_f), (HG_DIM, HG_DIM)))
            sf_s[...] = s_f * dcol + _dot_tn(kd, vb)
            tot_b = bb[0:1]
            qdb_s[sl, :] = (q * jnp.exp2(bb)).astype(BF16)
            kd = (kbs[c] * jnp.exp2(tot_b - bb)).astype(BF16)
            ub_s[pair * i + c] = _dot_tn(kd, vb)
            dcb_s[pair * i + c] = jnp.transpose(jnp.broadcast_to(jnp.exp2(tot_b), (HG_DIM, HG_DIM)))
        return carry

    zero = jnp.zeros((HG_DIM, HG_DIM), F32)
    sf_s[...] = init_ref[0] if has_init else zero
    lax.fori_loop(0, n_chunks // pair, fwd_step, 0)

    def bwd_step(i, s_b):
        outs = []
        for c in range(pair):
            ci = n_chunks - 1 - (pair * i + c)
            sl = chunk_slice(ci)
            outs.append((sl, op_s[sl, :] + dot(qdb_s[sl, :], s_b.astype(BF16))))
            s_b = s_b * dcb_s[ci] + ub_s[ci]
        for sl, o in outs:
            o = o * lax.rsqrt(jnp.mean(o * o, axis=-1, keepdims=True) + NORM_EPS) * gain_ref[...]
            out_ref[sl, :] = (o * _silu(hg_ref[sl, :])).astype(out_ref.dtype)
        return s_b

    s_b = lax.fori_loop(0, n_chunks // pair, bwd_step, init_ref[1] if has_init else zero)
    if not has_init:
        fin_ref[0] = sf_s[...]
        fin_ref[1] = s_b


def _hgrn(p, lb_tab, gain, tables, seq_len, n_seq, row0, layer, mix, init=None, sbuf=None):
    lvl, tri = tables
    has_init = init is not None

    def col(c):
        return pl.BlockSpec((seq_len, HG_DIM), lambda b, h: (row0 + b, c + h))

    in_specs = [col(COL_HQ), col(COL_HF), col(COL_HB), col(COL_HI), col(COL_HG),
                pl.BlockSpec((None, 4, HG_DIM), lambda b, h: (layer, 0, h)),
                pl.BlockSpec((1, HG_DIM), lambda b, h: (0, 0)),
                pl.BlockSpec((2, CHUNK, CHUNK), lambda b, h: (0, 0, 0)),
                pl.BlockSpec((CHUNK, CHUNK), lambda b, h: (0, 0))]
    args = [p, p, p, p, p, lb_tab, gain.reshape(1, HG_DIM), tri, lvl]
    if has_init:
        in_specs.append(pl.BlockSpec((None, None, 2, None, HG_DIM, HG_DIM),
                                     lambda b, h: (b, layer, 0, h, 0, 0)))
        args.append(init)
    in_specs.append(ANY_SPEC)
    args.append(mix)
    aliases = {len(args) - 1: 0}
    out_specs = [pl.BlockSpec((seq_len, HG_DIM), lambda b, h: (row0 + b, MIX_HG + h))]
    out_shape = [jax.ShapeDtypeStruct((N_TOK, D_MODEL), BF16)]
    if not has_init:
        if sbuf is not None:
            in_specs.append(ANY_SPEC)
            args.append(sbuf)
            aliases[len(args) - 1] = 1
        out_specs.append(pl.BlockSpec((None, None, 2, None, HG_DIM, HG_DIM),
                                      lambda b, h: (b, layer, 0, h, 0, 0)))
        out_shape.append(jax.ShapeDtypeStruct((n_seq, DEPTH, 2, HG_HEADS, HG_DIM, HG_DIM), F32))

    def rows(dtype):
        return pltpu.VMEM((seq_len, HG_DIM), dtype)

    return pl.pallas_call(
        functools.partial(_hgrn_kernel, seq_len=seq_len, has_init=has_init, has_sbuf=sbuf is not None),
        grid=(n_seq, HG_HEADS),
        in_specs=in_specs,
        out_specs=out_specs,
        out_shape=out_shape,
        scratch_shapes=[rows(F32), rows(BF16), rows(F32), rows(BF16),
                        pltpu.VMEM((seq_len // CHUNK, HG_DIM, HG_DIM), F32),
                        pltpu.VMEM((seq_len // CHUNK, HG_DIM, HG_DIM), F32),
                        pltpu.VMEM((HG_DIM, HG_DIM), F32)],
        input_output_aliases=aliases,
        compiler_params=_params(("arbitrary", "arbitrary")),
        name="hgrn_lat" if has_init else "hgrn_ctx",
    )(*args)


def _router_kernel(h_ref, r_ref, o_ref):
    lg = jnp.dot(h_ref[...], r_ref[...], preferred_element_type=F32)
    lane = lax.broadcasted_iota(jnp.int32, lg.shape, 1)
    lg = jnp.where(lane < N_EXPERTS, lg, -jnp.inf)
    m1 = jnp.max(lg, axis=-1, keepdims=True)
    i1 = jnp.min(jnp.where(lg == m1, lane, HEAD_DIM), axis=-1, keepdims=True)
    lg2 = jnp.where(lane == i1, -jnp.inf, lg)
    m2 = jnp.max(lg2, axis=-1, keepdims=True)
    i2 = jnp.min(jnp.where(lg2 == m2, lane, HEAD_DIM), axis=-1, keepdims=True)
    e2 = jnp.exp(m2 - m1)
    den = 1.0 + e2
    o_ref[...] = jnp.where(lane == 0, i1.astype(F32),
                           jnp.where(lane == 1, i2.astype(F32),
                                     jnp.where(lane == 2, 1.0 / den,
                                               jnp.where(lane == 3, e2 / den, 0.0))))


def _route(h, router, slab0):
    tm = 512
    rp = jnp.pad(router, ((0, 0), (0, HEAD_DIM - N_EXPERTS))).astype(BF16)
    out = pl.pallas_call(
        _router_kernel,
        grid=(N_TOK // tm,),
        in_specs=[pl.BlockSpec((tm, D_MODEL), lambda i: (i, 0)),
                  pl.BlockSpec((D_MODEL, HEAD_DIM), lambda i: (0, 0))],
        out_specs=pl.BlockSpec((tm, HEAD_DIM), lambda i: (i, 0)),
        out_shape=jax.ShapeDtypeStruct((N_TOK, HEAD_DIM), F32),
        compiler_params=_params(("arbitrary",)),
        name="router",
    )(h, rp)
    top_i = out[:, 0:2].astype(jnp.int32)
    gates = out[:, 2:4]
    e_flat = top_i.reshape(-1)
    onehot = (e_flat[:, None] == jnp.arange(N_EXPERTS)[None, :]).astype(jnp.int32)
    csum = jnp.cumsum(onehot, axis=0)
    counts = csum[-1]
    padded = ((counts + MOE_TM - 1) // MOE_TM) * MOE_TM
    ends = jnp.cumsum(padded)
    starts = ends - padded
    pos = jnp.sum(onehot * (csum - 1 + starts[None, :]), axis=1)
    row_token = jnp.zeros((MOE_ROWS,), jnp.int32).at[pos].set(jnp.arange(2 * N_TOK, dtype=jnp.int32) // 2)
    tile_start = jnp.arange(MOE_TILES, dtype=jnp.int32) * MOE_TM
    tile_expert = jnp.minimum(jnp.sum(tile_start[:, None] >= ends[None, :], axis=1), N_EXPERTS - 1)
    tile_valid = (tile_start < ends[-1]).astype(jnp.int32)
    return gates, pos.reshape(N_TOK, 2), row_token, (slab0 + tile_expert).astype(jnp.int32), tile_valid


def _rows(table, idx):
    return table.at[idx].get(mode="promise_in_bounds")


def kernel(x_prompt, x_sample, cache_k, cache_v, state_hgrn, c, c_ctx, w_mod, b_mod, norm_gains,
           w_in, w_out, attn_sink, hg_lb_logits, hg_norm_gain, ffn_w1, ffn_w3, ffn_w2,
           moe_router, moe_w1, moe_w3, moe_w2):
    x = jnp.concatenate([x_prompt.reshape(N_PROMPT, D_MODEL), x_sample.reshape(N_SAMPLE, D_MODEL)], axis=0)

    cvec = jnp.concatenate([c_ctx[None], c, jnp.zeros((3, D_MODEL), F32)], axis=0)
    mod = _modulation(cvec, w_mod, b_mod).reshape(DEPTH, 8, 6, D_MODEL)
    mod = jnp.pad(mod, ((0, 0), (0, 0), (0, 2), (0, 0)))
    group_src = np.array([0] * (N_PROMPT // MOD_GROUP) + list(range(1, 1 + DEC_BATCH)))
    mod = mod[:, group_src]

    lb_cum = jnp.cumsum(jax.nn.softmax(hg_lb_logits.astype(F32), axis=0), axis=0)
    lb = lb_cum - lb_cum[0:1]
    lb_tab = jnp.stack([jnp.log(lb[:, 0]), jnp.log1p(-lb[:, 0]),
                        jnp.log(lb[:, 1]), jnp.log1p(-lb[:, 1])], axis=1)

    rope = _rope_tables()
    tables = _level_tables()
    kc = cache_k.transpose(1, 0, 3, 2, 4).astype(BF16)
    vc = cache_v.transpose(1, 0, 3, 2, 4).astype(BF16)
    w_in_b, w_out_b = w_in.astype(BF16), w_out.astype(BF16)
    ffn_b = [w.astype(BF16) for w in (ffn_w1, ffn_w3, ffn_w2)]
    moe_b = [w.astype(BF16).reshape((-1,) + w.shape[2:]) for w in (moe_w1, moe_w3, moe_w2)]

    (h,) = _resnorm(x, [], None, None, 0, norm_gains[0, 0], mod[0], 1, 0)
    kbuf = vbuf = sbuf = None
    for l in range(DEPTH):
        i = l // 2
        moe = l % 2 == 1
        p = _dense(h, [w_in_b], l, 512, 1664, F32, "proj_in")
        mix, kbuf, vbuf = _attn_ctx(p, attn_sink[l], l, kbuf, vbuf)
        mix = _attn_lat(p, attn_sink[l], kc, vc, l, rope, mix)
        mix, sbuf = _hgrn(p, lb_tab, hg_norm_gain[l], tables, SEQ, BATCH, 0, l, mix, sbuf=sbuf)
        (mix,) = _hgrn(p, lb_tab, hg_norm_gain[l], tables, DEC_SEQ, DEC_BATCH, N_PROMPT // DEC_SEQ,
                       l, mix, init=state_hgrn)
        m = _dense(mix, [w_out_b], l, 512, 1024, F32, "proj_out")
        outs = _resnorm(x, [m], norm_gains[l, 1], mod[l], 2, norm_gains[l, 2], mod[l], 4, 3, wide_h=moe)
        x, h = outs[0], outs[1]
        if not moe:
            act = _dense(h, ffn_b[:2], i, 512, 1408, BF16, "ffn_up")
            f = [_dense(act, ffn_b[2:], i, 512, 1024, F32, "ffn_down")]
            gates = None
        else:
            gates, pos, row_token, tile_slab, tile_valid = _route(h, moe_router[i], i * N_EXPERTS)
            xs = _rows(outs[2], row_token)
            act = _gmm(xs, moe_b[:2], tile_slab, tile_valid, MOE_TM, 1408, BF16, "moe_up")
            ys = _gmm(act, moe_b[2:], tile_slab, tile_valid, MOE_TM, 1024, F32, "moe_down")
            f = [_rows(ys, pos[:, 0]), _rows(ys, pos[:, 1])]
        if l + 1 < DEPTH:
            x, h = _resnorm(x, f, norm_gains[l, 3], mod[l], 5, norm_gains[l + 1, 0], mod[l + 1], 1, 0,
                            gates=gates)
        else:
            (x,) = _resnorm(x, f, norm_gains[l, 3], mod[l], 5, None, None, 0, 0, gates=gates)
    kv_shape = (BATCH, DEPTH, SEQ, ATT_KV_HEADS, HEAD_DIM)
    return (x[:N_PROMPT].reshape(BATCH, SEQ, D_MODEL), x[N_PROMPT:].reshape(DEC_BATCH, DEC_SEQ, D_MODEL),
            kbuf.reshape(kv_shape), vbuf.reshape(kv_shape), sbuf)
```

```python
import functools

import numpy as np
import jax
import jax.numpy as jnp
from jax import lax
from jax.experimental import pallas as pl
from jax.experimental.pallas import tpu as pltpu

F32 = jnp.float32
BF16 = jnp.bfloat16

D_MODEL = 2048
BATCH = 16
SEQ = 256
DEPTH = 4
DEC_BATCH = 4
DEC_SEQ = 1024
PAST_LEN = 256
GRID_W = 64
HEAD_DIM = 128
ATT_HEADS = 8
ATT_KV_HEADS = 2
Q_PER_KV = 4
ATT_WIDTH = 1024
WINDOW = 128
ATT_BLOCK = 128
ROPE_BASE = 10000.0
HG_WIDTH = 1024
HG_HEADS = 8
HG_DIM = 128
N_EXPERTS = 8
D_FF = 5632
D_FF_EXPERT = 2816
NORM_EPS = 1e-6
IN_COLS = 6656

N_PROMPT = BATCH * SEQ
N_SAMPLE = DEC_BATCH * DEC_SEQ
N_TOK = N_PROMPT + N_SAMPLE
MOD_GROUP = 1024
N_GROUPS = N_TOK // MOD_GROUP

COL_K = 8
COL_V = 10
COL_HQ = 12
COL_HF = 20
COL_HB = 28
COL_HI = 36
COL_HG = 44
MIX_HG = ATT_WIDTH // HG_DIM

CHUNK = 128
N_LEVELS = 7
LOG2E = 1.4426950408889634

VMEM_LIMIT = 56 * 1024 * 1024

MOE_TM = 256
MOE_ROWS = 2 * N_TOK + N_EXPERTS * MOE_TM
MOE_TILES = MOE_ROWS // MOE_TM

ANY_SPEC = pl.BlockSpec(memory_space=pl.ANY)


def _params(sem):
    return pltpu.CompilerParams(dimension_semantics=sem, vmem_limit_bytes=VMEM_LIMIT)


def _silu(x):
    return x / (1.0 + jnp.exp(-x))


def _dot_nt(a, b):
    return lax.dot_general(a, b, (((1,), (1,)), ((), ())), preferred_element_type=F32)


def _dot_tn(a, b):
    return lax.dot_general(a, b, (((0,), (0,)), ((), ())), preferred_element_type=F32)


def _mod_kernel(c_ref, w_ref, b_ref, o_ref):
    s = _silu(c_ref[...]).astype(BF16)
    o_ref[...] = jnp.dot(s, w_ref[...].astype(BF16), preferred_element_type=F32) + b_ref[...]


def _modulation(cvec, w_mod, b_mod):
    tn = 1024
    n = 6 * D_MODEL
    return pl.pallas_call(
        _mod_kernel,
        grid=(DEPTH, n // tn),
        in_specs=[
            pl.BlockSpec((8, D_MODEL), lambda l, j: (0, 0)),
            pl.BlockSpec((None, D_MODEL, tn), lambda l, j: (l, 0, j)),
            pl.BlockSpec((None, 1, tn), lambda l, j: (l, 0, j)),
        ],
        out_specs=pl.BlockSpec((None, 8, tn), lambda l, j: (l, 0, j)),
        out_shape=jax.ShapeDtypeStruct((DEPTH, 8, n), F32),
        compiler_params=_params(("arbitrary", "arbitrary")),
        name="modulation",
    )(cvec, w_mod, b_mod.reshape(DEPTH, 1, n))


def _rms(x, gain):
    return x * lax.rsqrt(jnp.mean(x * x, axis=-1, keepdims=True) + NORM_EPS) * gain


def _resnorm_kernel(*refs, n_m, gate_row, has_h, sc_row, sh_row, wide_h):
    refs = list(refs)
    x_ref = refs.pop(0)
    x = x_ref[...]
    if n_m == 1:
        m = refs.pop(0)[...]
    elif n_m == 2:
        y0 = refs.pop(0)[...]
        y1 = refs.pop(0)[...]
        gt = refs.pop(0)[...]
        m = gt[:, 0:1] * y0 + gt[:, 1:2] * y1
    if n_m:
        ga = refs.pop(0)[...]
        modg = refs.pop(0)
        x = x + modg[gate_row:gate_row + 1, :] * _rms(m, ga)
    if has_h:
        gb = refs.pop(0)[...]
        modh = refs.pop(0)
        h = _rms(x, gb) * (1.0 + modh[sc_row:sc_row + 1, :]) + modh[sh_row:sh_row + 1, :]
    if n_m:
        refs.pop(0)[...] = x
    if has_h:
        refs.pop(0)[...] = h.astype(BF16)
        if wide_h:
            refs.pop(0)[...] = h


def _resnorm(x, ms, gain_m, mod_gate, gate_row, gain_h, mod_h, sc_row, sh_row, gates=None, wide_h=False):
    tm = 256
    per = MOD_GROUP // tm
    n_m = len(ms)
    has_h = gain_h is not None
    row = pl.BlockSpec((tm, D_MODEL), lambda i: (i, 0))
    vec = pl.BlockSpec((1, D_MODEL), lambda i: (0, 0))
    mod = pl.BlockSpec((None, 8, D_MODEL), lambda i: (i // per, 0, 0))
    args, specs = [x], [row]
    for m in ms:
        args.append(m)
        specs.append(row)
    if n_m == 2:
        args.append(gates)
        specs.append(pl.BlockSpec((tm, 2), lambda i: (i, 0)))
    if n_m:
        args += [gain_m.reshape(1, D_MODEL), mod_gate]
        specs += [vec, mod]
    if has_h:
        args += [gain_h.reshape(1, D_MODEL), mod_h]
        specs += [vec, mod]
    out_shape, out_specs = [], []
    if n_m:
        out_shape.append(jax.ShapeDtypeStruct((N_TOK, D_MODEL), F32))
        out_specs.append(row)
    if has_h:
        out_shape.append(jax.ShapeDtypeStruct((N_TOK, D_MODEL), BF16))
        out_specs.append(row)
        if wide_h:
            out_shape.append(jax.ShapeDtypeStruct((N_TOK, D_MODEL), F32))
            out_specs.append(row)
    outs = pl.pallas_call(
        functools.partial(_resnorm_kernel, n_m=n_m, gate_row=gate_row, has_h=has_h,
                          sc_row=sc_row, sh_row=sh_row, wide_h=wide_h),
        grid=(N_TOK // tm,),
        in_specs=specs,
        out_specs=out_specs,
        out_shape=out_shape,
        compiler_params=_params(("arbitrary",)),
        name="resnorm",
    )(*args)
    return outs


CAST_ROWS = 256


def _gmm_kernel(load_ref, slab_ref, col_ref, more_ref, nslab_ref, ncol_ref, valid_ref, x_ref, *rest,
                nw, tn, k_rows):
    w_hbm = rest[:nw]
    o_ref = rest[nw]
    stage = rest[nw + 1:2 * nw + 1]
    wb = rest[2 * nw + 1:3 * nw + 1]
    sem = rest[3 * nw + 1]
    t = pl.program_id(0) * pl.num_programs(1) + pl.program_id(1)

    def copies(slab, col):
        cols = pl.ds(pl.multiple_of(col * tn, 128), tn)
        return [pltpu.make_async_copy(w_hbm[k].at[slab, :, cols], stage[k], sem.at[k]) for k in range(nw)]

    @pl.when(t == 0)
    def _():
        for cp in copies(slab_ref[0], col_ref[0]):
            cp.start()

    @pl.when(load_ref[t] != 0)
    def _():
        for cp in copies(slab_ref[t], col_ref[t]):
            cp.wait()

        def cast(r, carry):
            rows = pl.ds(pl.multiple_of(r * CAST_ROWS, CAST_ROWS), CAST_ROWS)
            for k in range(nw):
                wb[k][rows, :] = stage[k][rows, :].astype(BF16)
            return carry

        lax.fori_loop(0, k_rows // CAST_ROWS, cast, 0)

        @pl.when(more_ref[t] != 0)
        def _():
            for cp in copies(nslab_ref[t], ncol_ref[t]):
                cp.start()

    valid = valid_ref[pl.program_id(1)] != 0

    @pl.when(valid)
    def _():
        x = x_ref[...].astype(BF16)
        a = jnp.dot(x, wb[0][...], preferred_element_type=F32)
        if nw == 2:
            a = _silu(a) * jnp.dot(x, wb[1][...], preferred_element_type=F32)
        o_ref[...] = a.astype(o_ref.dtype)

    @pl.when(jnp.logical_not(valid))
    def _():
        o_ref[...] = jnp.zeros_like(o_ref)


def _weight_schedule(tile_slab, tile_valid, n_col):
    n_row = tile_slab.shape[0]
    idx = jnp.arange(n_row, dtype=jnp.int32)
    keep = lax.cummax(jnp.where(tile_valid != 0, idx, 0), axis=0)
    slab = jnp.tile(tile_slab[keep], n_col)
    col = jnp.repeat(jnp.arange(n_col, dtype=jnp.int32), n_row)
    steps = n_row * n_col
    key = slab * n_col + col
    load = jnp.concatenate([jnp.ones((1,), bool), key[1:] != key[:-1]])
    load_at = jnp.where(load, jnp.arange(steps, dtype=jnp.int32), steps)
    first_from = lax.cummin(load_at, axis=0, reverse=True)
    nxt = jnp.concatenate([first_from[1:], jnp.full((1,), steps, jnp.int32)])
    more = nxt < steps
    nxt = jnp.minimum(nxt, steps - 1)
    return (load.astype(jnp.int32), slab.astype(jnp.int32), col, more.astype(jnp.int32),
            slab[nxt].astype(jnp.int32), col[nxt])


def _gmm(x, ws, tile_slab, tile_valid, tm, tn, out_dtype, name):
    m, k = x.shape
    n = ws[0].shape[-1]
    nw = len(ws)
    sched = _weight_schedule(tile_slab, tile_valid, n // tn)
    return pl.pallas_call(
        functools.partial(_gmm_kernel, nw=nw, tn=tn, k_rows=k),
        grid_spec=pltpu.PrefetchScalarGridSpec(
            num_scalar_prefetch=7,
            grid=(n // tn, m // tm),
            in_specs=[pl.BlockSpec((tm, k), lambda j, i, *_: (i, 0))] + [ANY_SPEC] * nw,
            out_specs=pl.BlockSpec((tm, tn), lambda j, i, *_: (i, j)),
            scratch_shapes=([pltpu.VMEM((k, tn), F32)] * nw + [pltpu.VMEM((k, tn), BF16)] * nw
                            + [pltpu.SemaphoreType.DMA((nw,))]),
        ),
        out_shape=jax.ShapeDtypeStruct((m, n), out_dtype),
        compiler_params=_params(("arbitrary", "arbitrary")),
        name=name,
    )(*sched, tile_valid, x, *ws)


def _dense(x, ws, slab, tm, tn, out_dtype, name):
    tiles = x.shape[0] // tm
    return _gmm(x, ws, jnp.full((tiles,), slab, jnp.int32), jnp.ones((tiles,), jnp.int32),
                tm, tn, out_dtype, name)


def _sink_rows(sink_ref, g, rows, per):
    r = lax.broadcasted_iota(jnp.int32, (rows, 1), 0)
    sk = jnp.full((rows, 1), sink_ref[g * Q_PER_KV + Q_PER_KV - 1], F32)
    for h in range(Q_PER_KV - 2, -1, -1):
        sk = jnp.where(r < (h + 1) * per, sink_ref[g * Q_PER_KV + h], sk)
    return sk


def _attn_ctx_kernel(*refs):
    sink_ref, q_ref, k_ref, v_ref = refs[:4]
    o_ref, ko_ref, vo_ref = refs[-3:]
    g = pl.program_id(1)
    scale = HEAD_DIM ** -0.5
    k = k_ref[...]
    v = v_ref[...]
    ko_ref[...] = k
    vo_ref[...] = v
    q = q_ref[...]
    qs = jnp.concatenate([q[:, r * HEAD_DIM:(r + 1) * HEAD_DIM] for r in range(Q_PER_KV)], axis=0)
    s = _dot_nt(qs.astype(BF16), k.astype(BF16)) * scale
    sk = _sink_rows(sink_ref, g, Q_PER_KV * SEQ, SEQ)
    m = jnp.maximum(jnp.max(s, axis=-1, keepdims=True), sk)
    e = jnp.exp(s - m)
    den = jnp.sum(e, axis=-1, keepdims=True) + jnp.exp(sk - m)
    o = jnp.dot(e.astype(BF16), v.astype(BF16), preferred_element_type=F32) / den
    o_ref[...] = jnp.concatenate([o[r * SEQ:(r + 1) * SEQ] for r in range(Q_PER_KV)],
                                 axis=1).astype(o_ref.dtype)


def _attn_ctx(p, sink, layer, mix, kbuf, vbuf):
    kv_spec = pl.BlockSpec((None, None, SEQ, HEAD_DIM), lambda b, g: (b, layer, 0, g))
    kv_shape = jax.ShapeDtypeStruct((BATCH, DEPTH, SEQ, ATT_KV_HEADS * HEAD_DIM), F32)
    in_specs = [
        pl.BlockSpec(memory_space=pltpu.SMEM),
        pl.BlockSpec((SEQ, Q_PER_KV * HEAD_DIM), lambda b, g: (b, g)),
        pl.BlockSpec((SEQ, HEAD_DIM), lambda b, g: (b, COL_K + g)),
        pl.BlockSpec((SEQ, HEAD_DIM), lambda b, g: (b, COL_V + g)),
    ]
    in_specs += [ANY_SPEC] * 3
    args = [sink, p, p, p, mix, kbuf, vbuf]
    aliases = {4: 0, 5: 1, 6: 2}
    return pl.pallas_call(
        _attn_ctx_kernel,
        grid=(BATCH, ATT_KV_HEADS),
        in_specs=in_specs,
        out_specs=[pl.BlockSpec((SEQ, Q_PER_KV * HEAD_DIM), lambda b, g: (b, g)), kv_spec, kv_spec],
        out_shape=[jax.ShapeDtypeStruct((N_TOK, D_MODEL), BF16), kv_shape, kv_shape],
        input_output_aliases=aliases,
        compiler_params=_params(("arbitrary", "arbitrary")),
        name="attn_ctx",
    )(*args)


def _rope(x, c, sa, sb):
    return x * c + pltpu.roll(x, 96, 1) * sa + pltpu.roll(x, 32, 1) * sb


def _attn_lat_kernel(sink_ref, q_ref, k_ref, v_ref, kc_ref, vc_ref, cq_ref, sqa_ref, sqb_ref,
                     ck_ref, ska_ref, skb_ref, mix_ref, o_ref, kr_s, vb_s):
    del mix_ref
    g = pl.program_id(1)
    qb = pl.program_id(2)
    scale = HEAD_DIM ** -0.5
    band = 3 * ATT_BLOCK

    @pl.when(qb == 0)
    def _():
        kr_s[...] = _rope(k_ref[...], ck_ref[...], ska_ref[...], skb_ref[...]).astype(BF16)
        vb_s[...] = v_ref[...].astype(BF16)

    q = q_ref[...]
    cq, sqa, sqb = cq_ref[...], sqa_ref[...], sqb_ref[...]
    qs = jnp.concatenate(
        [_rope(q[:, r * HEAD_DIM:(r + 1) * HEAD_DIM], cq, sqa, sqb) for r in range(Q_PER_KV)],
        axis=0).astype(BF16)
    rows = Q_PER_KV * ATT_BLOCK
    start = pl.multiple_of(jnp.clip(qb - 1, 0, DEC_SEQ // ATT_BLOCK - 3) * ATT_BLOCK, ATT_BLOCK)
    kb = kr_s[pl.ds(start, band), :]
    vb = vb_s[pl.ds(start, band), :]
    s_loc = _dot_nt(qs, kb) * scale
    kpos = start + lax.broadcasted_iota(jnp.int32, (rows, band), 1)
    qpos = qb * ATT_BLOCK + (lax.broadcasted_iota(jnp.int32, (rows, band), 0) & (ATT_BLOCK - 1))
    s_loc = jnp.where(jnp.abs(kpos - qpos) <= WINDOW, s_loc, -jnp.inf)
    s_ctx = _dot_nt(qs, kc_ref[...]) * scale
    sk = _sink_rows(sink_ref, g, rows, ATT_BLOCK)
    m = jnp.maximum(jnp.maximum(jnp.max(s_loc, axis=-1, keepdims=True),
                                jnp.max(s_ctx, axis=-1, keepdims=True)), sk)
    e_loc = jnp.exp(s_loc - m)
    e_ctx = jnp.exp(s_ctx - m)
    den = (jnp.sum(e_loc, axis=-1, keepdims=True) + jnp.sum(e_ctx, axis=-1, keepdims=True)
           + jnp.exp(sk - m))
    o = (jnp.dot(e_loc.astype(BF16), vb, preferred_element_type=F32)
         + jnp.dot(e_ctx.astype(BF16), vc_ref[...], preferred_element_type=F32)) / den
    o_ref[...] = jnp.concatenate([o[r * ATT_BLOCK:(r + 1) * ATT_BLOCK] for r in range(Q_PER_KV)],
                                 axis=1).astype(o_ref.dtype)


def _attn_lat(p, sink, kc, vc, layer, rope, mix):
    nqb = DEC_SEQ // ATT_BLOCK
    row0 = N_PROMPT // DEC_SEQ
    qrow0 = N_PROMPT // ATT_BLOCK
    tq = pl.BlockSpec((ATT_BLOCK, HEAD_DIM), lambda b, g, qb: (qb, 0))
    tk = pl.BlockSpec((DEC_SEQ, HEAD_DIM), lambda b, g, qb: (0, 0))
    ctx = pl.BlockSpec((None, None, None, PAST_LEN, HEAD_DIM), lambda b, g, qb: (layer, b, g, 0, 0))
    cos, sin_a, sin_b = rope
    return pl.pallas_call(
        _attn_lat_kernel,
        grid=(DEC_BATCH, ATT_KV_HEADS, nqb),
        in_specs=[
            pl.BlockSpec(memory_space=pltpu.SMEM),
            pl.BlockSpec((ATT_BLOCK, Q_PER_KV * HEAD_DIM), lambda b, g, qb: (qrow0 + b * nqb + qb, g)),
            pl.BlockSpec((DEC_SEQ, HEAD_DIM), lambda b, g, qb: (row0 + b, COL_K + g)),
            pl.BlockSpec((DEC_SEQ, HEAD_DIM), lambda b, g, qb: (row0 + b, COL_V + g)),
            ctx, ctx, tq, tq, tq, tk, tk, tk, ANY_SPEC,
        ],
        out_specs=pl.BlockSpec((ATT_BLOCK, Q_PER_KV * HEAD_DIM),
                               lambda b, g, qb: (qrow0 + b * nqb + qb, g)),
        out_shape=jax.ShapeDtypeStruct((N_TOK, D_MODEL), BF16),
        scratch_shapes=[pltpu.VMEM((DEC_SEQ, HEAD_DIM), BF16), pltpu.VMEM((DEC_SEQ, HEAD_DIM), BF16)],
        input_output_aliases={12: 0},
        compiler_params=_params(("arbitrary", "arbitrary", "arbitrary")),
        name="attn_lat",
    )(sink, p, p, p, kc, vc, cos, sin_a, sin_b, cos, sin_a, sin_b, mix)


def _rope_tables():
    rows = DEC_SEQ // GRID_W
    row = np.repeat(np.arange(rows), GRID_W).astype(np.float32)
    col = np.tile(np.arange(GRID_W), rows).astype(np.float32)
    axis_dim = HEAD_DIM // 2
    inv = (ROPE_BASE ** (-np.arange(0, axis_dim, 2, dtype=np.float32) / axis_dim)).astype(np.float32)
    lane = np.arange(HEAD_DIM)
    pos = np.where((lane // axis_dim)[None, :] == 0, row[:, None], col[:, None])
    ang = pos * inv[lane % (axis_dim // 2)][None, :]
    first = ((lane % axis_dim) < axis_dim // 2)[None, :]
    cos = np.cos(ang).astype(np.float32)
    sin = np.sin(ang).astype(np.float32)
    sin_a = np.where(first, -sin, 0.0).astype(np.float32)
    sin_b = np.where(first, 0.0, sin).astype(np.float32)
    return jnp.asarray(cos), jnp.asarray(sin_a), jnp.asarray(sin_b)


def _level_tables():
    t = np.arange(CHUNK)
    x = t[:, None] ^ t[None, :]
    lvl = np.where(x > 0, np.floor(np.log2(np.maximum(x, 1))), -1.0).astype(np.float32)
    tri = (t[:, None] >= t[None, :]).astype(np.float32)
    return jnp.asarray(lvl), jnp.asarray(np.stack([tri, tri.T]), dtype=BF16)


def _forget_gate(z, log_lb, log1m_lb):
    t = jnp.log(1.0 + jnp.exp(-jnp.abs(z)))
    c = log1m_lb + jnp.minimum(z, 0.0) - t
    log_f = jnp.maximum(log_lb, c) + jnp.log(1.0 + jnp.exp(-jnp.abs(log_lb - c)))
    return log_f, jnp.exp(log1m_lb - jnp.maximum(z, 0.0) - t)


def _running_log2_decay(g, tri):
    g1 = g.astype(BF16)
    g2 = (g - g1.astype(F32)).astype(BF16)
    dot = functools.partial(jnp.dot, preferred_element_type=F32)
    return (dot(tri, g1) + dot(tri, g2)) * LOG2E


def _intra_chunk_weights(qs, kfs, kbs, bfs, bbs, lvl):
    n = len(qs)
    acc = []
    for c in range(n):
        gram = _dot_nt(qs[c].astype(BF16), (kfs[c] + kbs[c]).astype(BF16))
        acc.append(jnp.where(lvl == -1.0, gram, 0.0))
    shape8 = (CHUNK // 8, 8, HG_DIM)
    sub = lax.broadcasted_iota(jnp.int32, shape8, 1)
    for j in range(N_LEVELS):
        half = 1 << j
        for c in range(n):
            q, kf, kb, bf, bb = qs[c], kfs[c], kbs[c], bfs[c], bbs[c]
            if j < 3:
                q, kf, kb, bf, bb = (a.reshape(shape8) for a in (q, kf, kb, bf, bb))
                upper = (sub & half) != 0
                if j == 0:
                    lhs = q * jnp.where(upper, 1.0 - kf, 1.0 - kb)
                    rhs = jnp.where(upper, kb, kf)
                else:
                    if j == 1:
                        ref_f = jnp.where(sub < 4, bf[:, 1:2, :], bf[:, 5:6, :])
                        ref_b = jnp.where(sub < 4, bb[:, 2:3, :], bb[:, 6:7, :])
                    else:
                        ref_f, ref_b = bf[:, 3:4, :], bb[:, 4:5, :]
                    lhs = q * jnp.exp2(jnp.where(upper, bf - ref_f, bb - ref_b))
                    rhs = jnp.where(upper, kb, kf) * jnp.exp2(jnp.where(upper, ref_b - bb, ref_f - bf))
            else:
                shape = (CHUNK // (2 * half), 2 * half, HG_DIM)
                q, kf, kb, bf, bb = (a.reshape(shape) for a in (q, kf, kb, bf, bb))
                ref_f, ref_b = bf[:, half - 1:half, :], bb[:, half:half + 1, :]
                lo, up = slice(0, half), slice(half, 2 * half)
                lhs = q * jnp.exp2(jnp.concatenate([bb[:, lo] - ref_b, bf[:, up] - ref_f], axis=1))
                rhs = (jnp.concatenate([kf[:, lo], kb[:, up]], axis=1)
                       * jnp.exp2(jnp.concatenate([ref_f - bf[:, lo], ref_b - bb[:, up]], axis=1)))
            gram = _dot_nt(lhs.reshape(CHUNK, HG_DIM).astype(BF16), rhs.reshape(CHUNK, HG_DIM).astype(BF16))
            acc[c] = jnp.where(lvl == float(j), gram, acc[c])
    return acc


def _hgrn_kernel(*refs, seq_len, has_init, has_sbuf):
    refs = list(refs)
    hq_ref, hf_ref, hb_ref, hi_ref, hg_ref, lb_ref, gain_ref, tri_ref, lvl_ref = refs[:9]
    refs = refs[9:]
    init_ref = refs.pop(0) if has_init else None
    refs.pop(0)
    if has_sbuf:
        refs.pop(0)
    out_ref = refs.pop(0)
    fin_ref = None if has_init else refs.pop(0)
    q_s, vb_s, op_s, qdb_s, ub_s, dcb_s, sf_s = refs
    n_chunks = seq_len // CHUNK
    pair = 2
    dot = functools.partial(jnp.dot, preferred_element_type=F32)

    q_s[...] = _silu(hq_ref[...])
    vb_s[...] = hi_ref[...].astype(BF16)

    def chunk_slice(ci):
        return pl.ds(pl.multiple_of(ci * CHUNK, CHUNK), CHUNK)

    def fwd_step(i, carry):
        sls = [chunk_slice(pair * i + c) for c in range(pair)]
        lvl = lvl_ref[...]
        gf = [_forget_gate(hf_ref[sl, :], lb_ref[0:1, :], lb_ref[1:2, :]) for sl in sls]
        gb = [_forget_gate(hb_ref[sl, :], lb_ref[2:3, :], lb_ref[3:4, :]) for sl in sls]
        bfs = [_running_log2_decay(g, tri_ref[0]) for g, _ in gf]
        bbs = [_running_log2_decay(g, tri_ref[1]) for g, _ in gb]
        qs = [q_s[sl, :] for sl in sls]
        kfs = [k for _, k in gf]
        kbs = [k for _, k in gb]
        acc = _intra_chunk_weights(qs, kfs, kbs, bfs, bbs, lvl)
        for c in range(pair):
            sl, q, bf, bb = sls[c], qs[c], bfs[c], bbs[c]
            vb = vb_s[sl, :]
            s_f = sf_s[...]
            qd = (q * jnp.exp2(bf)).astype(BF16)
            op_s[sl, :] = dot(jnp.concatenate([qd, acc[c].astype(BF16)], axis=1),
                              jnp.concatenate([s_f.astype(BF16), vb], axis=0))
            tot_f = bf[CHUNK - 1:CHUNK]
            kd = (kfs[c] * jnp.exp2(tot_f - bf)).astype(BF16)
            dcol = jnp.transpose(jnp.broadcast_to(jnp.exp2(tot_f), (HG_DIM, HG_DIM)))
            sf_s[...] = s_f * dcol + _dot_tn(kd, vb)
            tot_b = bb[0:1]
            qdb_s[sl, :] = (q * jnp.exp2(bb)).astype(BF16)
            kd = (kbs[c] * jnp.exp2(tot_b - bb)).astype(BF16)
            ub_s[pair * i + c] = _dot_tn(kd, vb)
            dcb_s[pair * i + c] = jnp.transpose(jnp.broadcast_to(jnp.exp2(tot_b), (HG_DIM, HG_DIM)))
        return carry

    zero = jnp.zeros((HG_DIM, HG_DIM), F32)
    sf_s[...] = init_ref[0] if has_init else zero
    lax.fori_loop(0, n_chunks // pair, fwd_step, 0)

    def bwd_step(i, s_b):
        outs = []
        for c in range(pair):
            ci = n_chunks - 1 - (pair * i + c)
            sl = chunk_slice(ci)
            outs.append((sl, op_s[sl, :] + dot(qdb_s[sl, :], s_b.astype(BF16))))
            s_b = s_b * dcb_s[ci] + ub_s[ci]
        for sl, o in outs:
            o = o * lax.rsqrt(jnp.mean(o * o, axis=-1, keepdims=True) + NORM_EPS) * gain_ref[...]
            out_ref[sl, :] = (o * _silu(hg_ref[sl, :])).astype(out_ref.dtype)
        return s_b

    s_b = lax.fori_loop(0, n_chunks // pair, bwd_step, init_ref[1] if has_init else zero)
    if not has_init:
        fin_ref[0] = sf_s[...]
        fin_ref[1] = s_b


def _hgrn(p, lb_tab, gain, tables, seq_len, n_seq, row0, layer, mix, init=None, sbuf=None):
    lvl, tri = tables
    has_init = init is not None

    def col(c):
        return pl.BlockSpec((seq_len, HG_DIM), lambda b, h: (row0 + b, c + h))

    in_specs = [col(COL_HQ), col(COL_HF), col(COL_HB), col(COL_HI), col(COL_HG),
                pl.BlockSpec((None, 4, HG_DIM), lambda b, h: (layer, 0, h)),
                pl.BlockSpec((1, HG_DIM), lambda b, h: (0, 0)),
                pl.BlockSpec((2, CHUNK, CHUNK), lambda b, h: (0, 0, 0)),
                pl.BlockSpec((CHUNK, CHUNK), lambda b, h: (0, 0))]
    args = [p, p, p, p, p, lb_tab, gain.reshape(1, HG_DIM), tri, lvl]
    if has_init:
        in_specs.append(pl.BlockSpec((None, None, 2, None, HG_DIM, HG_DIM),
                                     lambda b, h: (b, layer, 0, h, 0, 0)))
        args.append(init)
    in_specs.append(ANY_SPEC)
    args.append(mix)
    aliases = {len(args) - 1: 0}
    out_specs = [pl.BlockSpec((seq_len, HG_DIM), lambda b, h: (row0 + b, MIX_HG + h))]
    out_shape = [jax.ShapeDtypeStruct((N_TOK, D_MODEL), BF16)]
    if not has_init:
        in_specs.append(ANY_SPEC)
        args.append(sbuf)
        aliases[len(args) - 1] = 1
        out_specs.append(pl.BlockSpec((None, None, 2, None, HG_DIM, HG_DIM),
                                      lambda b, h: (b, layer, 0, h, 0, 0)))
        out_shape.append(jax.ShapeDtypeStruct((n_seq, DEPTH, 2, HG_HEADS, HG_DIM, HG_DIM), F32))

    def rows(dtype):
        return pltpu.VMEM((seq_len, HG_DIM), dtype)

    return pl.pallas_call(
        functools.partial(_hgrn_kernel, seq_len=seq_len, has_init=has_init, has_sbuf=not has_init),
        grid=(n_seq, HG_HEADS),
        in_specs=in_specs,
        out_specs=out_specs,
        out_shape=out_shape,
        scratch_shapes=[rows(F32), rows(BF16), rows(F32), rows(BF16),
                        pltpu.VMEM((seq_len // CHUNK, HG_DIM, HG_DIM), F32),
                        pltpu.VMEM((seq_len // CHUNK, HG_DIM, HG_DIM), F32),
                        pltpu.VMEM((HG_DIM, HG_DIM), F32)],
        input_output_aliases=aliases,
        compiler_params=_params(("arbitrary", "arbitrary")),
        name="hgrn_lat" if has_init else "hgrn_ctx",
    )(*args)


def _router_kernel(h_ref, r_ref, o_ref):
    lg = jnp.dot(h_ref[...], r_ref[...], preferred_element_type=F32)
    lane = lax.broadcasted_iota(jnp.int32, lg.shape, 1)
    lg = jnp.where(lane < N_EXPERTS, lg, -jnp.inf)
    m1 = jnp.max(lg, axis=-1, keepdims=True)
    i1 = jnp.min(jnp.where(lg == m1, lane, HEAD_DIM), axis=-1, keepdims=True)
    lg2 = jnp.where(lane == i1, -jnp.inf, lg)
    m2 = jnp.max(lg2, axis=-1, keepdims=True)
    i2 = jnp.min(jnp.where(lg2 == m2, lane, HEAD_DIM), axis=-1, keepdims=True)
    e2 = jnp.exp(m2 - m1)
    den = 1.0 + e2
    o_ref[...] = jnp.where(lane == 0, i1.astype(F32),
                           jnp.where(lane == 1, i2.astype(F32),
                                     jnp.where(lane == 2, 1.0 / den,
                                               jnp.where(lane == 3, e2 / den, 0.0))))


def _route(h, router, slab0):
    tm = 512
    rp = jnp.pad(router, ((0, 0), (0, HEAD_DIM - N_EXPERTS))).astype(BF16)
    out = pl.pallas_call(
        _router_kernel,
        grid=(N_TOK // tm,),
        in_specs=[pl.BlockSpec((tm, D_MODEL), lambda i: (i, 0)),
                  pl.BlockSpec((D_MODEL, HEAD_DIM), lambda i: (0, 0))],
        out_specs=pl.BlockSpec((tm, HEAD_DIM), lambda i: (i, 0)),
        out_shape=jax.ShapeDtypeStruct((N_TOK, HEAD_DIM), F32),
        compiler_params=_params(("arbitrary",)),
        name="router",
    )(h, rp)
    top_i = out[:, 0:2].astype(jnp.int32)
    gates = out[:, 2:4]
    e_flat = top_i.reshape(-1)
    onehot = (e_flat[:, None] == jnp.arange(N_EXPERTS)[None, :]).astype(jnp.int32)
    csum = jnp.cumsum(onehot, axis=0)
    counts = csum[-1]
    padded = ((counts + MOE_TM - 1) // MOE_TM) * MOE_TM
    ends = jnp.cumsum(padded)
    starts = ends - padded
    pos = jnp.sum(onehot * (csum - 1 + starts[None, :]), axis=1)
    row_token = jnp.zeros((MOE_ROWS,), jnp.int32).at[pos].set(jnp.arange(2 * N_TOK, dtype=jnp.int32) // 2)
    tile_start = jnp.arange(MOE_TILES, dtype=jnp.int32) * MOE_TM
    tile_expert = jnp.minimum(jnp.sum(tile_start[:, None] >= ends[None, :], axis=1), N_EXPERTS - 1)
    tile_valid = (tile_start < ends[-1]).astype(jnp.int32)
    return gates, pos.reshape(N_TOK, 2), row_token, (slab0 + tile_expert).astype(jnp.int32), tile_valid


def _rows(table, idx):
    return table.at[idx].get(mode="promise_in_bounds")


def kernel(x_prompt, x_sample, cache_k, cache_v, state_hgrn, c, c_ctx, w_mod, b_mod, norm_gains,
           w_in, w_out, attn_sink, hg_lb_logits, hg_norm_gain, ffn_w1, ffn_w3, ffn_w2,
           moe_router, moe_w1, moe_w3, moe_w2):
    x = jnp.concatenate([x_prompt.reshape(N_PROMPT, D_MODEL), x_sample.reshape(N_SAMPLE, D_MODEL)], axis=0)

    cvec = jnp.concatenate([c_ctx[None], c, jnp.zeros((3, D_MODEL), F32)], axis=0)
    mod = _modulation(cvec, w_mod, b_mod).reshape(DEPTH, 8, 6, D_MODEL)
    mod = jnp.pad(mod, ((0, 0), (0, 0), (0, 2), (0, 0)))
    group_src = np.array([0] * (N_PROMPT // MOD_GROUP) + list(range(1, 1 + DEC_BATCH)))
    mod = mod[:, group_src]

    lb_cum = jnp.cumsum(jax.nn.softmax(hg_lb_logits.astype(F32), axis=0), axis=0)
    lb = lb_cum - lb_cum[0:1]
    lb_tab = jnp.stack([jnp.log(lb[:, 0]), jnp.log1p(-lb[:, 0]),
                        jnp.log(lb[:, 1]), jnp.log1p(-lb[:, 1])], axis=1)

    rope = _rope_tables()
    tables = _level_tables()
    kc = cache_k.transpose(1, 0, 3, 2, 4).astype(BF16)
    vc = cache_v.transpose(1, 0, 3, 2, 4).astype(BF16)
    ffn_w = [ffn_w1, ffn_w3, ffn_w2]
    moe_w = [w.reshape((-1,) + w.shape[2:]) for w in (moe_w1, moe_w3, moe_w2)]

    (h,) = _resnorm(x, [], None, None, 0, norm_gains[0, 0], mod[0], 1, 0)
    mix = jnp.zeros((N_TOK, D_MODEL), BF16)
    kbuf = jnp.zeros((BATCH, DEPTH, SEQ, ATT_KV_HEADS * HEAD_DIM), F32)
    vbuf = jnp.zeros((BATCH, DEPTH, SEQ, ATT_KV_HEADS * HEAD_DIM), F32)
    sbuf = jnp.zeros((BATCH, DEPTH, 2, HG_HEADS, HG_DIM, HG_DIM), F32)
    for l in range(DEPTH):
        i = l // 2
        moe = l % 2 == 1
        p = _dense(h, [w_in], l, 512, 1664, F32, "proj_in")
        mix, kbuf, vbuf = _attn_ctx(p, attn_sink[l], l, mix, kbuf, vbuf)
        mix = _attn_lat(p, attn_sink[l], kc, vc, l, rope, mix)
        mix, sbuf = _hgrn(p, lb_tab, hg_norm_gain[l], tables, SEQ, BATCH, 0, l, mix, sbuf=sbuf)
        (mix,) = _hgrn(p, lb_tab, hg_norm_gain[l], tables, DEC_SEQ, DEC_BATCH, N_PROMPT // DEC_SEQ,
                       l, mix, init=state_hgrn)
        m = _dense(mix, [w_out], l, 512, 1024, F32, "proj_out")
        outs = _resnorm(x, [m], norm_gains[l, 1], mod[l], 2, norm_gains[l, 2], mod[l], 4, 3, wide_h=moe)
        x, h = outs[0], outs[1]
        if not moe:
            act = _dense(h, ffn_w[:2], i, 256, 1408, BF16, "ffn_up")
            f = [_dense(act, ffn_w[2:], i, 256, 1024, F32, "ffn_down")]
            gates = None
        else:
            gates, pos, row_token, tile_slab, tile_valid = _route(h, moe_router[i], i * N_EXPERTS)
            xs = _rows(outs[2], row_token)
            act = _gmm(xs, moe_w[:2], tile_slab, tile_valid, MOE_TM, 1408, BF16, "moe_up")
            ys = _gmm(act, moe_w[2:], tile_slab, tile_valid, MOE_TM, 1024, F32, "moe_down")
            f = [_rows(ys, pos[:, 0]), _rows(ys, pos[:, 1])]
        if l + 1 < DEPTH:
            x, h = _resnorm(x, f, norm_gains[l, 3], mod[l], 5, norm_gains[l + 1, 0], mod[l + 1], 1, 0,
                            gates=gates)
        else:
            (x,) = _resnorm(x, f, norm_gains[l, 3], mod[l], 5, None, None, 0, 0, gates=gates)
    kv_shape = (BATCH, DEPTH, SEQ, ATT_KV_HEADS, HEAD_DIM)
    return (x[:N_PROMPT].reshape(BATCH, SEQ, D_MODEL), x[N_PROMPT:].reshape(DEC_BATCH, DEC_SEQ, D_MODEL),
            kbuf.reshape(kv_shape), vbuf.reshape(kv_shape), sbuf)
```

```python
import functools

import numpy as np
import jax
import jax.numpy as jnp
from jax import lax
from jax.experimental import pallas as pl
from jax.experimental.pallas import tpu as pltpu

F32 = jnp.float32
BF16 = jnp.bfloat16

D_MODEL = 2048
BATCH = 16
SEQ = 256
DEPTH = 4
DEC_BATCH = 4
DEC_SEQ = 1024
PAST_LEN = 256
GRID_W = 64
HEAD_DIM = 128
ATT_HEADS = 8
ATT_KV_HEADS = 2
Q_PER_KV = 4
ATT_WIDTH = 1024
WINDOW = 128
ATT_BLOCK = 128
ROPE_BASE = 10000.0
HG_WIDTH = 1024
HG_HEADS = 8
HG_DIM = 128
N_EXPERTS = 8
D_FF = 5632
D_FF_EXPERT = 2816
NORM_EPS = 1e-6
IN_COLS = 6656

N_PROMPT = BATCH * SEQ
N_SAMPLE = DEC_BATCH * DEC_SEQ
N_TOK = N_PROMPT + N_SAMPLE
MOD_GROUP = 1024
N_GROUPS = N_TOK // MOD_GROUP

COL_K = 8
COL_V = 10
COL_HQ = 12
COL_HF = 20
COL_HB = 28
COL_HI = 36
COL_HG = 44
MIX_HG = ATT_WIDTH // HG_DIM

CHUNK = 128
N_LEVELS = 7
LOG2E = 1.4426950408889634

VMEM_LIMIT = 60 * 1024 * 1024

TILES = {
    "proj_in": (1024, 1664),
    "proj_out": (512, 2048),
    "ffn_up": (512, 1408),
    "ffn_down": (512, 1024),
    "moe_up": (256, 1408),
    "moe_down": (256, 2048),
}

MOE_TM = TILES["moe_up"][0]
MOE_ROWS = 2 * N_TOK + N_EXPERTS * MOE_TM
MOE_TILES = MOE_ROWS // MOE_TM

ANY_SPEC = pl.BlockSpec(memory_space=pl.ANY)


def _params(sem):
    return pltpu.CompilerParams(dimension_semantics=sem, vmem_limit_bytes=VMEM_LIMIT)


def _silu(x):
    return x / (1.0 + jnp.exp(-x))


def _dot_nt(a, b):
    return lax.dot_general(a, b, (((1,), (1,)), ((), ())), preferred_element_type=F32)


def _dot_tn(a, b):
    return lax.dot_general(a, b, (((0,), (0,)), ((), ())), preferred_element_type=F32)


def _mod_kernel(c_ref, w_ref, b_ref, o_ref):
    s = _silu(c_ref[...]).astype(BF16)
    o_ref[...] = jnp.dot(s, w_ref[...].astype(BF16), preferred_element_type=F32) + b_ref[...]


def _modulation(cvec, w_mod, b_mod):
    tn = 1024
    n = 6 * D_MODEL
    return pl.pallas_call(
        _mod_kernel,
        grid=(DEPTH, n // tn),
        in_specs=[
            pl.BlockSpec((8, D_MODEL), lambda l, j: (0, 0)),
            pl.BlockSpec((None, D_MODEL, tn), lambda l, j: (l, 0, j)),
            pl.BlockSpec((None, 1, tn), lambda l, j: (l, 0, j)),
        ],
        out_specs=pl.BlockSpec((None, 8, tn), lambda l, j: (l, 0, j)),
        out_shape=jax.ShapeDtypeStruct((DEPTH, 8, n), F32),
        compiler_params=_params(("arbitrary", "arbitrary")),
        name="modulation",
    )(cvec, w_mod, b_mod.reshape(DEPTH, 1, n))


def _rms(x, gain):
    return x * lax.rsqrt(jnp.mean(x * x, axis=-1, keepdims=True) + NORM_EPS) * gain


def _resnorm_kernel(*refs, n_m, gate_row, has_h, sc_row, sh_row, wide_h):
    refs = list(refs)
    x_ref = refs.pop(0)
    x = x_ref[...]
    if n_m == 1:
        m = refs.pop(0)[...]
    elif n_m == 2:
        y0 = refs.pop(0)[...]
        y1 = refs.pop(0)[...]
        gt = refs.pop(0)[...]
        m = gt[:, 0:1] * y0 + gt[:, 1:2] * y1
    if n_m:
        ga = refs.pop(0)[...]
        modg = refs.pop(0)
        x = x + modg[gate_row:gate_row + 1, :] * _rms(m, ga)
    if has_h:
        gb = refs.pop(0)[...]
        modh = refs.pop(0)
        h = _rms(x, gb) * (1.0 + modh[sc_row:sc_row + 1, :]) + modh[sh_row:sh_row + 1, :]
    if n_m:
        refs.pop(0)[...] = x
    if has_h:
        refs.pop(0)[...] = h.astype(BF16)
        if wide_h:
            refs.pop(0)[...] = h


def _resnorm(x, ms, gain_m, mod_gate, gate_row, gain_h, mod_h, sc_row, sh_row, gates=None, wide_h=False):
    tm = 256
    per = MOD_GROUP // tm
    n_m = len(ms)
    has_h = gain_h is not None
    row = pl.BlockSpec((tm, D_MODEL), lambda i: (i, 0))
    vec = pl.BlockSpec((1, D_MODEL), lambda i: (0, 0))
    mod = pl.BlockSpec((None, 8, D_MODEL), lambda i: (i // per, 0, 0))
    args, specs = [x], [row]
    for m in ms:
        args.append(m)
        specs.append(row)
    if n_m == 2:
        args.append(gates)
        specs.append(pl.BlockSpec((tm, 2), lambda i: (i, 0)))
    if n_m:
        args += [gain_m.reshape(1, D_MODEL), mod_gate]
        specs += [vec, mod]
    if has_h:
        args += [gain_h.reshape(1, D_MODEL), mod_h]
        specs += [vec, mod]
    out_shape, out_specs = [], []
    if n_m:
        out_shape.append(jax.ShapeDtypeStruct((N_TOK, D_MODEL), F32))
        out_specs.append(row)
    if has_h:
        out_shape.append(jax.ShapeDtypeStruct((N_TOK, D_MODEL), BF16))
        out_specs.append(row)
        if wide_h:
            out_shape.append(jax.ShapeDtypeStruct((N_TOK, D_MODEL), F32))
            out_specs.append(row)
    outs = pl.pallas_call(
        functools.partial(_resnorm_kernel, n_m=n_m, gate_row=gate_row, has_h=has_h,
                          sc_row=sc_row, sh_row=sh_row, wide_h=wide_h),
        grid=(N_TOK // tm,),
        in_specs=specs,
        out_specs=out_specs,
        out_shape=out_shape,
        compiler_params=_params(("arbitrary",)),
        name="resnorm",
    )(*args)
    return outs


CAST_ROWS = 256


def _gmm_kernel(load_ref, slab_ref, col_ref, more_ref, nslab_ref, ncol_ref, valid_ref, x_ref, *rest,
                nw, tn, k_rows):
    w_hbm = rest[:nw]
    o_ref = rest[nw]
    stage = rest[nw + 1:2 * nw + 1]
    wb = rest[2 * nw + 1:3 * nw + 1]
    sem = rest[3 * nw + 1]
    t = pl.program_id(0) * pl.num_programs(1) + pl.program_id(1)

    def copies(slab, col):
        cols = pl.ds(pl.multiple_of(col * tn, 128), tn)
        return [pltpu.make_async_copy(w_hbm[k].at[slab, :, cols], stage[k], sem.at[k]) for k in range(nw)]

    @pl.when(t == 0)
    def _():
        for cp in copies(slab_ref[0], col_ref[0]):
            cp.start()

    @pl.when(load_ref[t] != 0)
    def _():
        for cp in copies(slab_ref[t], col_ref[t]):
            cp.wait()

        def cast(r, carry):
            rows = pl.ds(pl.multiple_of(r * CAST_ROWS, CAST_ROWS), CAST_ROWS)
            for k in range(nw):
                wb[k][rows, :] = stage[k][rows, :].astype(BF16)
            return carry

        lax.fori_loop(0, k_rows // CAST_ROWS, cast, 0)

        @pl.when(more_ref[t] != 0)
        def _():
            for cp in copies(nslab_ref[t], ncol_ref[t]):
                cp.start()

    valid = valid_ref[pl.program_id(1)] != 0

    @pl.when(valid)
    def _():
        x = x_ref[...].astype(BF16)
        a = jnp.dot(x, wb[0][...], preferred_element_type=F32)
        if nw == 2:
            a = _silu(a) * jnp.dot(x, wb[1][...], preferred_element_type=F32)
        o_ref[...] = a.astype(o_ref.dtype)

    @pl.when(jnp.logical_not(valid))
    def _():
        o_ref[...] = jnp.zeros_like(o_ref)


def _weight_schedule(tile_slab, tile_valid, n_col):
    n_row = tile_slab.shape[0]
    idx = jnp.arange(n_row, dtype=jnp.int32)
    keep = lax.cummax(jnp.where(tile_valid != 0, idx, 0), axis=0)
    slab = jnp.tile(tile_slab[keep], n_col)
    col = jnp.repeat(jnp.arange(n_col, dtype=jnp.int32), n_row)
    steps = n_row * n_col
    key = slab * n_col + col
    load = jnp.concatenate([jnp.ones((1,), bool), key[1:] != key[:-1]])
    load_at = jnp.where(load, jnp.arange(steps, dtype=jnp.int32), steps)
    first_from = lax.cummin(load_at, axis=0, reverse=True)
    nxt = jnp.concatenate([first_from[1:], jnp.full((1,), steps, jnp.int32)])
    more = nxt < steps
    nxt = jnp.minimum(nxt, steps - 1)
    return (load.astype(jnp.int32), slab.astype(jnp.int32), col, more.astype(jnp.int32),
            slab[nxt].astype(jnp.int32), col[nxt])


def _gmm(x, ws, tile_slab, tile_valid, out_dtype, name):
    tm, tn = TILES[name]
    m, k = x.shape
    n = ws[0].shape[-1]
    nw = len(ws)
    sched = _weight_schedule(tile_slab, tile_valid, n // tn)
    return pl.pallas_call(
        functools.partial(_gmm_kernel, nw=nw, tn=tn, k_rows=k),
        grid_spec=pltpu.PrefetchScalarGridSpec(
            num_scalar_prefetch=7,
            grid=(n // tn, m // tm),
            in_specs=[pl.BlockSpec((tm, k), lambda j, i, *_: (i, 0))] + [ANY_SPEC] * nw,
            out_specs=pl.BlockSpec((tm, tn), lambda j, i, *_: (i, j)),
            scratch_shapes=([pltpu.VMEM((k, tn), F32)] * nw + [pltpu.VMEM((k, tn), BF16)] * nw
                            + [pltpu.SemaphoreType.DMA((nw,))]),
        ),
        out_shape=jax.ShapeDtypeStruct((m, n), out_dtype),
        compiler_params=_params(("arbitrary", "arbitrary")),
        name=name,
    )(*sched, tile_valid, x, *ws)


def _dense(x, ws, slab, out_dtype, name):
    tiles = x.shape[0] // TILES[name][0]
    return _gmm(x, ws, jnp.full((tiles,), slab, jnp.int32), jnp.ones((tiles,), jnp.int32),
                out_dtype, name)


def _sink_rows(sink_ref, g, rows, per):
    r = lax.broadcasted_iota(jnp.int32, (rows, 1), 0)
    sk = jnp.full((rows, 1), sink_ref[g * Q_PER_KV + Q_PER_KV - 1], F32)
    for h in range(Q_PER_KV - 2, -1, -1):
        sk = jnp.where(r < (h + 1) * per, sink_ref[g * Q_PER_KV + h], sk)
    return sk


def _attn_ctx_kernel(*refs):
    sink_ref, q_ref, k_ref, v_ref = refs[:4]
    o_ref, ko_ref, vo_ref = refs[-3:]
    g = pl.program_id(1)
    scale = HEAD_DIM ** -0.5
    k = k_ref[...]
    v = v_ref[...]
    ko_ref[...] = k
    vo_ref[...] = v
    q = q_ref[...]
    qs = jnp.concatenate([q[:, r * HEAD_DIM:(r + 1) * HEAD_DIM] for r in range(Q_PER_KV)], axis=0)
    s = _dot_nt(qs.astype(BF16), k.astype(BF16)) * scale
    sk = _sink_rows(sink_ref, g, Q_PER_KV * SEQ, SEQ)
    m = jnp.maximum(jnp.max(s, axis=-1, keepdims=True), sk)
    e = jnp.exp(s - m)
    den = jnp.sum(e, axis=-1, keepdims=True) + jnp.exp(sk - m)
    o = jnp.dot(e.astype(BF16), v.astype(BF16), preferred_element_type=F32) / den
    o_ref[...] = jnp.concatenate([o[r * SEQ:(r + 1) * SEQ] for r in range(Q_PER_KV)],
                                 axis=1).astype(o_ref.dtype)


def _attn_ctx(p, sink, layer, mix, kbuf, vbuf):
    kv_spec = pl.BlockSpec((None, None, SEQ, HEAD_DIM), lambda b, g: (b, layer, 0, g))
    kv_shape = jax.ShapeDtypeStruct((BATCH, DEPTH, SEQ, ATT_KV_HEADS * HEAD_DIM), F32)
    in_specs = [
        pl.BlockSpec(memory_space=pltpu.SMEM),
        pl.BlockSpec((SEQ, Q_PER_KV * HEAD_DIM), lambda b, g: (b, g)),
        pl.BlockSpec((SEQ, HEAD_DIM), lambda b, g: (b, COL_K + g)),
        pl.BlockSpec((SEQ, HEAD_DIM), lambda b, g: (b, COL_V + g)),
    ]
    in_specs += [ANY_SPEC] * 3
    args = [sink, p, p, p, mix, kbuf, vbuf]
    aliases = {4: 0, 5: 1, 6: 2}
    return pl.pallas_call(
        _attn_ctx_kernel,
        grid=(BATCH, ATT_KV_HEADS),
        in_specs=in_specs,
        out_specs=[pl.BlockSpec((SEQ, Q_PER_KV * HEAD_DIM), lambda b, g: (b, g)), kv_spec, kv_spec],
        out_shape=[jax.ShapeDtypeStruct((N_TOK, D_MODEL), BF16), kv_shape, kv_shape],
        input_output_aliases=aliases,
        compiler_params=_params(("arbitrary", "arbitrary")),
        name="attn_ctx",
    )(*args)


def _rope(x, c, sa, sb):
    return x * c + pltpu.roll(x, 96, 1) * sa + pltpu.roll(x, 32, 1) * sb


def _attn_lat_kernel(sink_ref, q_ref, k_ref, v_ref, kc_ref, vc_ref, cq_ref, sqa_ref, sqb_ref,
                     ck_ref, ska_ref, skb_ref, mix_ref, o_ref, kr_s, vb_s):
    del mix_ref
    g = pl.program_id(1)
    qb = pl.program_id(2)
    scale = HEAD_DIM ** -0.5
    band = 3 * ATT_BLOCK

    @pl.when(qb == 0)
    def _():
        kr_s[...] = _rope(k_ref[...], ck_ref[...], ska_ref[...], skb_ref[...]).astype(BF16)
        vb_s[...] = v_ref[...].astype(BF16)

    q = q_ref[...]
    cq, sqa, sqb = cq_ref[...], sqa_ref[...], sqb_ref[...]
    qs = jnp.concatenate(
        [_rope(q[:, r * HEAD_DIM:(r + 1) * HEAD_DIM], cq, sqa, sqb) for r in range(Q_PER_KV)],
        axis=0).astype(BF16)
    rows = Q_PER_KV * ATT_BLOCK
    start = pl.multiple_of(jnp.clip(qb - 1, 0, DEC_SEQ // ATT_BLOCK - 3) * ATT_BLOCK, ATT_BLOCK)
    kb = kr_s[pl.ds(start, band), :]
    vb = vb_s[pl.ds(start, band), :]
    s_loc = _dot_nt(qs, kb) * scale
    kpos = start + lax.broadcasted_iota(jnp.int32, (rows, band), 1)
    qpos = qb * ATT_BLOCK + (lax.broadcasted_iota(jnp.int32, (rows, band), 0) & (ATT_BLOCK - 1))
    s_loc = jnp.where(jnp.abs(kpos - qpos) <= WINDOW, s_loc, -jnp.inf)
    s_ctx = _dot_nt(qs, kc_ref[...]) * scale
    sk = _sink_rows(sink_ref, g, rows, ATT_BLOCK)
    m = jnp.maximum(jnp.maximum(jnp.max(s_loc, axis=-1, keepdims=True),
                                jnp.max(s_ctx, axis=-1, keepdims=True)), sk)
    e_loc = jnp.exp(s_loc - m)
    e_ctx = jnp.exp(s_ctx - m)
    den = (jnp.sum(e_loc, axis=-1, keepdims=True) + jnp.sum(e_ctx, axis=-1, keepdims=True)
           + jnp.exp(sk - m))
    o = (jnp.dot(e_loc.astype(BF16), vb, preferred_element_type=F32)
         + jnp.dot(e_ctx.astype(BF16), vc_ref[...], preferred_element_type=F32)) / den
    o_ref[...] = jnp.concatenate([o[r * ATT_BLOCK:(r + 1) * ATT_BLOCK] for r in range(Q_PER_KV)],
                                 axis=1).astype(o_ref.dtype)


def _attn_lat(p, sink, kc, vc, layer, rope, mix):
    nqb = DEC_SEQ // ATT_BLOCK
    row0 = N_PROMPT // DEC_SEQ
    qrow0 = N_PROMPT // ATT_BLOCK
    tq = pl.BlockSpec((ATT_BLOCK, HEAD_DIM), lambda b, g, qb: (qb, 0))
    tk = pl.BlockSpec((DEC_SEQ, HEAD_DIM), lambda b, g, qb: (0, 0))
    ctx = pl.BlockSpec((None, None, None, PAST_LEN, HEAD_DIM), lambda b, g, qb: (layer, b, g, 0, 0))
    cos, sin_a, sin_b = rope
    return pl.pallas_call(
        _attn_lat_kernel,
        grid=(DEC_BATCH, ATT_KV_HEADS, nqb),
        in_specs=[
            pl.BlockSpec(memory_space=pltpu.SMEM),
            pl.BlockSpec((ATT_BLOCK, Q_PER_KV * HEAD_DIM), lambda b, g, qb: (qrow0 + b * nqb + qb, g)),
            pl.BlockSpec((DEC_SEQ, HEAD_DIM), lambda b, g, qb: (row0 + b, COL_K + g)),
            pl.BlockSpec((DEC_SEQ, HEAD_DIM), lambda b, g, qb: (row0 + b, COL_V + g)),
            ctx, ctx, tq, tq, tq, tk, tk, tk, ANY_SPEC,
        ],
        out_specs=pl.BlockSpec((ATT_BLOCK, Q_PER_KV * HEAD_DIM),
                               lambda b, g, qb: (qrow0 + b * nqb + qb, g)),
        out_shape=jax.ShapeDtypeStruct((N_TOK, D_MODEL), BF16),
        scratch_shapes=[pltpu.VMEM((DEC_SEQ, HEAD_DIM), BF16), pltpu.VMEM((DEC_SEQ, HEAD_DIM), BF16)],
        input_output_aliases={12: 0},
        compiler_params=_params(("arbitrary", "arbitrary", "arbitrary")),
        name="attn_lat",
    )(sink, p, p, p, kc, vc, cos, sin_a, sin_b, cos, sin_a, sin_b, mix)


def _rope_tables():
    rows = DEC_SEQ // GRID_W
    row = np.repeat(np.arange(rows), GRID_W).astype(np.float32)
    col = np.tile(np.arange(GRID_W), rows).astype(np.float32)
    axis_dim = HEAD_DIM // 2
    inv = (ROPE_BASE ** (-np.arange(0, axis_dim, 2, dtype=np.float32) / axis_dim)).astype(np.float32)
    lane = np.arange(HEAD_DIM)
    pos = np.where((lane // axis_dim)[None, :] == 0, row[:, None], col[:, None])
    ang = pos * inv[lane % (axis_dim // 2)][None, :]
    first = ((lane % axis_dim) < axis_dim // 2)[None, :]
    cos = np.cos(ang).astype(np.float32)
    sin = np.sin(ang).astype(np.float32)
    sin_a = np.where(first, -sin, 0.0).astype(np.float32)
    sin_b = np.where(first, 0.0, sin).astype(np.float32)
    return jnp.asarray(cos), jnp.asarray(sin_a), jnp.asarray(sin_b)


def _level_tables():
    t = np.arange(CHUNK)
    x = t[:, None] ^ t[None, :]
    lvl = np.where(x > 0, np.floor(np.log2(np.maximum(x, 1))), -1.0).astype(np.float32)
    tri = (t[:, None] >= t[None, :]).astype(np.float32)
    return jnp.asarray(lvl), jnp.asarray(np.stack([tri, tri.T]), dtype=BF16)


def _forget_gate(z, log_lb, log1m_lb):
    t = jnp.log(1.0 + jnp.exp(-jnp.abs(z)))
    c = log1m_lb + jnp.minimum(z, 0.0) - t
    log_f = jnp.maximum(log_lb, c) + jnp.log(1.0 + jnp.exp(-jnp.abs(log_lb - c)))
    return log_f, jnp.exp(log1m_lb - jnp.maximum(z, 0.0) - t)


def _running_log2_decay(g, tri):
    g1 = g.astype(BF16)
    g2 = (g - g1.astype(F32)).astype(BF16)
    dot = functools.partial(jnp.dot, preferred_element_type=F32)
    return (dot(tri, g1) + dot(tri, g2)) * LOG2E


def _intra_chunk_weights(qs, kfs, kbs, bfs, bbs, lvl):
    n = len(qs)
    acc = []
    for c in range(n):
        gram = _dot_nt(qs[c].astype(BF16), (kfs[c] + kbs[c]).astype(BF16))
        acc.append(jnp.where(lvl == -1.0, gram, 0.0))
    shape8 = (CHUNK // 8, 8, HG_DIM)
    sub = lax.broadcasted_iota(jnp.int32, shape8, 1)
    for j in range(N_LEVELS):
        half = 1 << j
        for c in range(n):
            q, kf, kb, bf, bb = qs[c], kfs[c], kbs[c], bfs[c], bbs[c]
            if j < 3:
                q, kf, kb, bf, bb = (a.reshape(shape8) for a in (q, kf, kb, bf, bb))
                upper = (sub & half) != 0
                if j == 0:
                    lhs = q * jnp.where(upper, 1.0 - kf, 1.0 - kb)
                    rhs = jnp.where(upper, kb, kf)
                else:
                    if j == 1:
                        ref_f = jnp.where(sub < 4, bf[:, 1:2, :], bf[:, 5:6, :])
                        ref_b = jnp.where(sub < 4, bb[:, 2:3, :], bb[:, 6:7, :])
                    else:
                        ref_f, ref_b = bf[:, 3:4, :], bb[:, 4:5, :]
                    lhs = q * jnp.exp2(jnp.where(upper, bf - ref_f, bb - ref_b))
                    rhs = jnp.where(upper, kb, kf) * jnp.exp2(jnp.where(upper, ref_b - bb, ref_f - bf))
            else:
                shape = (CHUNK // (2 * half), 2 * half, HG_DIM)
                q, kf, kb, bf, bb = (a.reshape(shape) for a in (q, kf, kb, bf, bb))
                ref_f, ref_b = bf[:, half - 1:half, :], bb[:, half:half + 1, :]
                lo, up = slice(0, half), slice(half, 2 * half)
                lhs = q * jnp.exp2(jnp.concatenate([bb[:, lo] - ref_b, bf[:, up] - ref_f], axis=1))
                rhs = (jnp.concatenate([kf[:, lo], kb[:, up]], axis=1)
                       * jnp.exp2(jnp.concatenate([ref_f - bf[:, lo], ref_b - bb[:, up]], axis=1)))
            gram = _dot_nt(lhs.reshape(CHUNK, HG_DIM).astype(BF16), rhs.reshape(CHUNK, HG_DIM).astype(BF16))
            acc[c] = jnp.where(lvl == float(j), gram, acc[c])
    return acc


def _hgrn_kernel(*refs, seq_len, has_init, has_sbuf):
    refs = list(refs)
    hq_ref, hf_ref, hb_ref, hi_ref, hg_ref, lb_ref, gain_ref, tri_ref, lvl_ref = refs[:9]
    refs = refs[9:]
    init_ref = refs.pop(0) if has_init else None
    refs.pop(0)
    if has_sbuf:
        refs.pop(0)
    out_ref = refs.pop(0)
    fin_ref = None if has_init else refs.pop(0)
    q_s, vb_s, op_s, qdb_s, ub_s, dcb_s, sf_s = refs
    n_chunks = seq_len // CHUNK
    pair = 2
    dot = functools.partial(jnp.dot, preferred_element_type=F32)

    q_s[...] = _silu(hq_ref[...])
    vb_s[...] = hi_ref[...].astype(BF16)

    def chunk_slice(ci):
        return pl.ds(pl.multiple_of(ci * CHUNK, CHUNK), CHUNK)

    def fwd_step(i, carry):
        sls = [chunk_slice(pair * i + c) for c in range(pair)]
        lvl = lvl_ref[...]
        gf = [_forget_gate(hf_ref[sl, :], lb_ref[0:1, :], lb_ref[1:2, :]) for sl in sls]
        gb = [_forget_gate(hb_ref[sl, :], lb_ref[2:3, :], lb_ref[3:4, :]) for sl in sls]
        bfs = [_running_log2_decay(g, tri_ref[0]) for g, _ in gf]
        bbs = [_running_log2_decay(g, tri_ref[1]) for g, _ in gb]
        qs = [q_s[sl, :] for sl in sls]
        kfs = [k for _, k in gf]
        kbs = [k for _, k in gb]
        acc = _intra_chunk_weights(qs, kfs, kbs, bfs, bbs, lvl)
        for c in range(pair):
            sl, q, bf, bb = sls[c], qs[c], bfs[c], bbs[c]
            vb = vb_s[sl, :]
            s_f = sf_s[...]
            qd = (q * jnp.exp2(bf)).astype(BF16)
            op_s[sl, :] = dot(jnp.concatenate([qd, acc[c].astype(BF16)], axis=1),
                              jnp.concatenate([s_f.astype(BF16), vb], axis=0))
            tot_f = bf[CHUNK - 1:CHUNK]
            kd = (kfs[c] * jnp.exp2(tot_f - bf)).astype(BF16)
            dcol = jnp.transpose(jnp.broadcast_to(jnp.exp2(tot_f), (HG_DIM, HG_DIM)))
            sf_s[...] = s_f * dcol + _dot_tn(kd, vb)
            tot_b = bb[0:1]
            qdb_s[sl, :] = (q * jnp.exp2(bb)).astype(BF16)
            kd = (kbs[c] * jnp.exp2(tot_b - bb)).astype(BF16)
            ub_s[pair * i + c] = _dot_tn(kd, vb)
            dcb_s[pair * i + c] = jnp.transpose(jnp.broadcast_to(jnp.exp2(tot_b), (HG_DIM, HG_DIM)))
        return carry

    zero = jnp.zeros((HG_DIM, HG_DIM), F32)
    sf_s[...] = init_ref[0] if has_init else zero
    lax.fori_loop(0, n_chunks // pair, fwd_step, 0)

    def bwd_step(i, s_b):
        outs = []
        for c in range(pair):
            ci = n_chunks - 1 - (pair * i + c)
            sl = chunk_slice(ci)
            outs.append((sl, op_s[sl, :] + dot(qdb_s[sl, :], s_b.astype(BF16))))
            s_b = s_b * dcb_s[ci] + ub_s[ci]
        for sl, o in outs:
            o = o * lax.rsqrt(jnp.mean(o * o, axis=-1, keepdims=True) + NORM_EPS) * gain_ref[...]
            out_ref[sl, :] = (o * _silu(hg_ref[sl, :])).astype(out_ref.dtype)
        return s_b

    s_b = lax.fori_loop(0, n_chunks // pair, bwd_step, init_ref[1] if has_init else zero)
    if not has_init:
        fin_ref[0] = sf_s[...]
        fin_ref[1] = s_b


def _hgrn(p, lb_tab, gain, tables, seq_len, n_seq, row0, layer, mix, init=None, sbuf=None):
    lvl, tri = tables
    has_init = init is not None

    def col(c):
        return pl.BlockSpec((seq_len, HG_DIM), lambda b, h: (row0 + b, c + h))

    in_specs = [col(COL_HQ), col(COL_HF), col(COL_HB), col(COL_HI), col(COL_HG),
                pl.BlockSpec((None, 4, HG_DIM), lambda b, h: (layer, 0, h)),
                pl.BlockSpec((1, HG_DIM), lambda b, h: (0, 0)),
                pl.BlockSpec((2, CHUNK, CHUNK), lambda b, h: (0, 0, 0)),
                pl.BlockSpec((CHUNK, CHUNK), lambda b, h: (0, 0))]
    args = [p, p, p, p, p, lb_tab, gain.reshape(1, HG_DIM), tri, lvl]
    if has_init:
        in_specs.append(pl.BlockSpec((None, None, 2, None, HG_DIM, HG_DIM),
                                     lambda b, h: (b, layer, 0, h, 0, 0)))
        args.append(init)
    in_specs.append(ANY_SPEC)
    args.append(mix)
    aliases = {len(args) - 1: 0}
    out_specs = [pl.BlockSpec((seq_len, HG_DIM), lambda b, h: (row0 + b, MIX_HG + h))]
    out_shape = [jax.ShapeDtypeStruct((N_TOK, D_MODEL), BF16)]
    if not has_init:
        in_specs.append(ANY_SPEC)
        args.append(sbuf)
        aliases[len(args) - 1] = 1
        out_specs.append(pl.BlockSpec((None, None, 2, None, HG_DIM, HG_DIM),
                                      lambda b, h: (b, layer, 0, h, 0, 0)))
        out_shape.append(jax.ShapeDtypeStruct((n_seq, DEPTH, 2, HG_HEADS, HG_DIM, HG_DIM), F32))

    def rows(dtype):
        return pltpu.VMEM((seq_len, HG_DIM), dtype)

    return pl.pallas_call(
        functools.partial(_hgrn_kernel, seq_len=seq_len, has_init=has_init, has_sbuf=not has_init),
        grid=(n_seq, HG_HEADS),
        in_specs=in_specs,
        out_specs=out_specs,
        out_shape=out_shape,
        scratch_shapes=[rows(F32), rows(BF16), rows(F32), rows(BF16),
                        pltpu.VMEM((seq_len // CHUNK, HG_DIM, HG_DIM), F32),
                        pltpu.VMEM((seq_len // CHUNK, HG_DIM, HG_DIM), F32),
                        pltpu.VMEM((HG_DIM, HG_DIM), F32)],
        input_output_aliases=aliases,
        compiler_params=_params(("arbitrary", "arbitrary")),
        name="hgrn_lat" if has_init else "hgrn_ctx",
    )(*args)


def _router_kernel(h_ref, r_ref, o_ref):
    lg = jnp.dot(h_ref[...], r_ref[...], preferred_element_type=F32)
    lane = lax.broadcasted_iota(jnp.int32, lg.shape, 1)
    lg = jnp.where(lane < N_EXPERTS, lg, -jnp.inf)
    m1 = jnp.max(lg, axis=-1, keepdims=True)
    i1 = jnp.min(jnp.where(lg == m1, lane, HEAD_DIM), axis=-1, keepdims=True)
    lg2 = jnp.where(lane == i1, -jnp.inf, lg)
    m2 = jnp.max(lg2, axis=-1, keepdims=True)
    i2 = jnp.min(jnp.where(lg2 == m2, lane, HEAD_DIM), axis=-1, keepdims=True)
    e2 = jnp.exp(m2 - m1)
    den = 1.0 + e2
    o_ref[...] = jnp.where(lane == 0, i1.astype(F32),
                           jnp.where(lane == 1, i2.astype(F32),
                                     jnp.where(lane == 2, 1.0 / den,
                                               jnp.where(lane == 3, e2 / den, 0.0))))


def _route(h, router, slab0):
    tm = 512
    rp = jnp.pad(router, ((0, 0), (0, HEAD_DIM - N_EXPERTS))).astype(BF16)
    out = pl.pallas_call(
        _router_kernel,
        grid=(N_TOK // tm,),
        in_specs=[pl.BlockSpec((tm, D_MODEL), lambda i: (i, 0)),
                  pl.BlockSpec((D_MODEL, HEAD_DIM), lambda i: (0, 0))],
        out_specs=pl.BlockSpec((tm, HEAD_DIM), lambda i: (i, 0)),
        out_shape=jax.ShapeDtypeStruct((N_TOK, HEAD_DIM), F32),
        compiler_params=_params(("arbitrary",)),
        name="router",
    )(h, rp)
    top_i = out[:, 0:2].astype(jnp.int32)
    gates = out[:, 2:4]
    e_flat = top_i.reshape(-1)
    onehot = (e_flat[:, None] == jnp.arange(N_EXPERTS)[None, :]).astype(jnp.int32)
    csum = jnp.cumsum(onehot, axis=0)
    counts = csum[-1]
    padded = ((counts + MOE_TM - 1) // MOE_TM) * MOE_TM
    ends = jnp.cumsum(padded)
    starts = ends - padded
    pos = jnp.sum(onehot * (csum - 1 + starts[None, :]), axis=1)
    row_token = jnp.zeros((MOE_ROWS,), jnp.int32).at[pos].set(jnp.arange(2 * N_TOK, dtype=jnp.int32) // 2)
    tile_start = jnp.arange(MOE_TILES, dtype=jnp.int32) * MOE_TM
    tile_expert = jnp.minimum(jnp.sum(tile_start[:, None] >= ends[None, :], axis=1), N_EXPERTS - 1)
    tile_valid = (tile_start < ends[-1]).astype(jnp.int32)
    return gates, pos.reshape(N_TOK, 2), row_token, (slab0 + tile_expert).astype(jnp.int32), tile_valid


def _rows(table, idx):
    return table.at[idx].get(mode="promise_in_bounds")


def kernel(x_prompt, x_sample, cache_k, cache_v, state_hgrn, c, c_ctx, w_mod, b_mod, norm_gains,
           w_in, w_out, attn_sink, hg_lb_logits, hg_norm_gain, ffn_w1, ffn_w3, ffn_w2,
           moe_router, moe_w1, moe_w3, moe_w2):
    x = jnp.concatenate([x_prompt.reshape(N_PROMPT, D_MODEL), x_sample.reshape(N_SAMPLE, D_MODEL)], axis=0)

    cvec = jnp.concatenate([c_ctx[None], c, jnp.zeros((3, D_MODEL), F32)], axis=0)
    mod = _modulation(cvec, w_mod, b_mod).reshape(DEPTH, 8, 6, D_MODEL)
    mod = jnp.pad(mod, ((0, 0), (0, 0), (0, 2), (0, 0)))
    group_src = np.array([0] * (N_PROMPT // MOD_GROUP) + list(range(1, 1 + DEC_BATCH)))
    mod = mod[:, group_src]

    lb_cum = jnp.cumsum(jax.nn.softmax(hg_lb_logits.astype(F32), axis=0), axis=0)
    lb = lb_cum - lb_cum[0:1]
    lb_tab = jnp.stack([jnp.log(lb[:, 0]), jnp.log1p(-lb[:, 0]),
                        jnp.log(lb[:, 1]), jnp.log1p(-lb[:, 1])], axis=1)

    rope = _rope_tables()
    tables = _level_tables()
    kc = cache_k.transpose(1, 0, 3, 2, 4).astype(BF16)
    vc = cache_v.transpose(1, 0, 3, 2, 4).astype(BF16)
    ffn_w = [ffn_w1, ffn_w3, ffn_w2]
    moe_w = [w.reshape((-1,) + w.shape[2:]) for w in (moe_w1, moe_w3, moe_w2)]

    (h,) = _resnorm(x, [], None, None, 0, norm_gains[0, 0], mod[0], 1, 0)
    mix = jnp.zeros((N_TOK, D_MODEL), BF16)
    kbuf = jnp.zeros((BATCH, DEPTH, SEQ, ATT_KV_HEADS * HEAD_DIM), F32)
    vbuf = jnp.zeros((BATCH, DEPTH, SEQ, ATT_KV_HEADS * HEAD_DIM), F32)
    sbuf = jnp.zeros((BATCH, DEPTH, 2, HG_HEADS, HG_DIM, HG_DIM), F32)
    for l in range(DEPTH):
        i = l // 2
        moe = l % 2 == 1
        p = _dense(h, [w_in], l, F32, "proj_in")
        mix, kbuf, vbuf = _attn_ctx(p, attn_sink[l], l, mix, kbuf, vbuf)
        mix = _attn_lat(p, attn_sink[l], kc, vc, l, rope, mix)
        mix, sbuf = _hgrn(p, lb_tab, hg_norm_gain[l], tables, SEQ, BATCH, 0, l, mix, sbuf=sbuf)
        (mix,) = _hgrn(p, lb_tab, hg_norm_gain[l], tables, DEC_SEQ, DEC_BATCH, N_PROMPT // DEC_SEQ,
                       l, mix, init=state_hgrn)
        m = _dense(mix, [w_out], l, F32, "proj_out")
        outs = _resnorm(x, [m], norm_gains[l, 1], mod[l], 2, norm_gains[l, 2], mod[l], 4, 3, wide_h=moe)
        x, h = outs[0], outs[1]
        if not moe:
            act = _dense(h, ffn_w[:2], i, BF16, "ffn_up")
            f = [_dense(act, ffn_w[2:], i, F32, "ffn_down")]
            gates = None
        else:
            gates, pos, row_token, tile_slab, tile_valid = _route(h, moe_router[i], i * N_EXPERTS)
            xs = _rows(outs[2], row_token)
            act = _gmm(xs, moe_w[:2], tile_slab, tile_valid, BF16, "moe_up")
            ys = _gmm(act, moe_w[2:], tile_slab, tile_valid, F32, "moe_down")
            f = [_rows(ys, pos[:, 0]), _rows(ys, pos[:, 1])]
        if l + 1 < DEPTH:
            x, h = _resnorm(x, f, norm_gains[l, 3], mod[l], 5, norm_gains[l + 1, 0], mod[l + 1], 1, 0,
                            gates=gates)
        else:
            (x,) = _resnorm(x, f, norm_gains[l, 3], mod[l], 5, None, None, 0, 0, gates=gates)
    kv_shape = (BATCH, DEPTH, SEQ, ATT_KV_HEADS, HEAD_DIM)
    return (x[:N_PROMPT].reshape(BATCH, SEQ, D_MODEL), x[N_PROMPT:].reshape(DEC_BATCH, DEC_SEQ, D_MODEL),
            kbuf.reshape(kv_shape), vbuf.reshape(kv_shape), sbuf)
```

```python
import functools

import numpy as np
import jax
import jax.numpy as jnp
from jax import lax
from jax.experimental import pallas as pl
from jax.experimental.pallas import tpu as pltpu

F32 = jnp.float32
BF16 = jnp.bfloat16

D_MODEL = 2048
BATCH = 16
SEQ = 256
DEPTH = 4
DEC_BATCH = 4
DEC_SEQ = 1024
PAST_LEN = 256
GRID_W = 64
HEAD_DIM = 128
ATT_HEADS = 8
ATT_KV_HEADS = 2
Q_PER_KV = 4
ATT_WIDTH = 1024
WINDOW = 128
ATT_BLOCK = 128
ROPE_BASE = 10000.0
HG_WIDTH = 1024
HG_HEADS = 8
HG_DIM = 128
N_EXPERTS = 8
D_FF = 5632
D_FF_EXPERT = 2816
NORM_EPS = 1e-6
IN_COLS = 6656

N_PROMPT = BATCH * SEQ
N_SAMPLE = DEC_BATCH * DEC_SEQ
N_TOK = N_PROMPT + N_SAMPLE
MOD_GROUP = 1024
N_GROUPS = N_TOK // MOD_GROUP

COL_K = 8
COL_V = 10
COL_HQ = 12
COL_HF = 20
COL_HB = 28
COL_HI = 36
COL_HG = 44
MIX_HG = ATT_WIDTH // HG_DIM

CHUNK = 128
N_LEVELS = 7
LOG2E = 1.4426950408889634

VMEM_LIMIT = 60 * 1024 * 1024

TILES = {
    "proj_in": (1024, 1664),
    "proj_out": (512, 2048),
    "ffn_up": (512, 1408),
    "ffn_down": (512, 1024),
    "moe_up": (256, 1408),
    "moe_down": (256, 2048),
}

MOE_TM = TILES["moe_up"][0]
MOE_ROWS = 2 * N_TOK + N_EXPERTS * MOE_TM
MOE_TILES = MOE_ROWS // MOE_TM

ANY_SPEC = pl.BlockSpec(memory_space=pl.ANY)


def _params(sem):
    return pltpu.CompilerParams(dimension_semantics=sem, vmem_limit_bytes=VMEM_LIMIT)


def _silu(x):
    return x / (1.0 + jnp.exp(-x))


def _dot_nt(a, b):
    return lax.dot_general(a, b, (((1,), (1,)), ((), ())), preferred_element_type=F32)


def _dot_tn(a, b):
    return lax.dot_general(a, b, (((0,), (0,)), ((), ())), preferred_element_type=F32)


def _mod_kernel(c_ref, w_ref, b_ref, o_ref):
    s = _silu(c_ref[...]).astype(BF16)
    o_ref[...] = jnp.dot(s, w_ref[...].astype(BF16), preferred_element_type=F32) + b_ref[...]


def _modulation(cvec, w_mod, b_mod):
    tn = 1024
    n = 6 * D_MODEL
    return pl.pallas_call(
        _mod_kernel,
        grid=(DEPTH, n // tn),
        in_specs=[
            pl.BlockSpec((8, D_MODEL), lambda l, j: (0, 0)),
            pl.BlockSpec((None, D_MODEL, tn), lambda l, j: (l, 0, j)),
            pl.BlockSpec((None, 1, tn), lambda l, j: (l, 0, j)),
        ],
        out_specs=pl.BlockSpec((None, 8, tn), lambda l, j: (l, 0, j)),
        out_shape=jax.ShapeDtypeStruct((DEPTH, 8, n), F32),
        compiler_params=_params(("arbitrary", "arbitrary")),
        name="modulation",
    )(cvec, w_mod, b_mod.reshape(DEPTH, 1, n))


def _rms(x, gain):
    return x * lax.rsqrt(jnp.mean(x * x, axis=-1, keepdims=True) + NORM_EPS) * gain


def _resnorm_kernel(*refs, n_m, gate_row, has_h, sc_row, sh_row, wide_h):
    refs = list(refs)
    x_ref = refs.pop(0)
    x = x_ref[...]
    if n_m == 1:
        m = refs.pop(0)[...]
    elif n_m == 2:
        y0 = refs.pop(0)[...]
        y1 = refs.pop(0)[...]
        gt = refs.pop(0)[...]
        m = gt[:, 0:1] * y0 + gt[:, 1:2] * y1
    if n_m:
        ga = refs.pop(0)[...]
        modg = refs.pop(0)
        x = x + modg[gate_row:gate_row + 1, :] * _rms(m, ga)
    if has_h:
        gb = refs.pop(0)[...]
        modh = refs.pop(0)
        h = _rms(x, gb) * (1.0 + modh[sc_row:sc_row + 1, :]) + modh[sh_row:sh_row + 1, :]
    if n_m:
        refs.pop(0)[...] = x
    if has_h:
        refs.pop(0)[...] = h.astype(BF16)
        if wide_h:
            refs.pop(0)[...] = h


def _resnorm(x, ms, gain_m, mod_gate, gate_row, gain_h, mod_h, sc_row, sh_row, gates=None, wide_h=False):
    tm = 256
    per = MOD_GROUP // tm
    n_m = len(ms)
    has_h = gain_h is not None
    row = pl.BlockSpec((tm, D_MODEL), lambda i: (i, 0))
    vec = pl.BlockSpec((1, D_MODEL), lambda i: (0, 0))
    mod = pl.BlockSpec((None, 8, D_MODEL), lambda i: (i // per, 0, 0))
    args, specs = [x], [row]
    for m in ms:
        args.append(m)
        specs.append(row)
    if n_m == 2:
        args.append(gates)
        specs.append(pl.BlockSpec((tm, 2), lambda i: (i, 0)))
    if n_m:
        args += [gain_m.reshape(1, D_MODEL), mod_gate]
        specs += [vec, mod]
    if has_h:
        args += [gain_h.reshape(1, D_MODEL), mod_h]
        specs += [vec, mod]
    out_shape, out_specs = [], []
    if n_m:
        out_shape.append(jax.ShapeDtypeStruct((N_TOK, D_MODEL), F32))
        out_specs.append(row)
    if has_h:
        out_shape.append(jax.ShapeDtypeStruct((N_TOK, D_MODEL), BF16))
        out_specs.append(row)
        if wide_h:
            out_shape.append(jax.ShapeDtypeStruct((N_TOK, D_MODEL), F32))
            out_specs.append(row)
    outs = pl.pallas_call(
        functools.partial(_resnorm_kernel, n_m=n_m, gate_row=gate_row, has_h=has_h,
                          sc_row=sc_row, sh_row=sh_row, wide_h=wide_h),
        grid=(N_TOK // tm,),
        in_specs=specs,
        out_specs=out_specs,
        out_shape=out_shape,
        compiler_params=_params(("arbitrary",)),
        name="resnorm",
    )(*args)
    return outs


CAST_ROWS = 256


def _gmm_kernel(load_ref, slab_ref, col_ref, more_ref, nslab_ref, ncol_ref, valid_ref, x_ref, *rest,
                nw, tn, k_rows):
    w_hbm = rest[:nw]
    o_ref = rest[nw]
    stage = rest[nw + 1:2 * nw + 1]
    wb = rest[2 * nw + 1:3 * nw + 1]
    sem = rest[3 * nw + 1]
    t = pl.program_id(0) * pl.num_programs(1) + pl.program_id(1)

    def copies(slab, col):
        cols = pl.ds(pl.multiple_of(col * tn, 128), tn)
        return [pltpu.make_async_copy(w_hbm[k].at[slab, :, cols], stage[k], sem.at[k]) for k in range(nw)]

    @pl.when(t == 0)
    def _():
        for cp in copies(slab_ref[0], col_ref[0]):
            cp.start()

    @pl.when(load_ref[t] != 0)
    def _():
        for cp in copies(slab_ref[t], col_ref[t]):
            cp.wait()

        def cast(r, carry):
            rows = pl.ds(pl.multiple_of(r * CAST_ROWS, CAST_ROWS), CAST_ROWS)
            for k in range(nw):
                wb[k][rows, :] = stage[k][rows, :].astype(BF16)
            return carry

        lax.fori_loop(0, k_rows // CAST_ROWS, cast, 0)

        @pl.when(more_ref[t] != 0)
        def _():
            for cp in copies(nslab_ref[t], ncol_ref[t]):
                cp.start()

    valid = valid_ref[pl.program_id(1)] != 0

    @pl.when(valid)
    def _():
        x = x_ref[...].astype(BF16)
        a = jnp.dot(x, wb[0][...], preferred_element_type=F32)
        if nw == 2:
            a = _silu(a) * jnp.dot(x, wb[1][...], preferred_element_type=F32)
        o_ref[...] = a.astype(o_ref.dtype)

    @pl.when(jnp.logical_not(valid))
    def _():
        o_ref[...] = jnp.zeros_like(o_ref)


def _weight_schedule(tile_slab, tile_valid, n_col):
    n_row = tile_slab.shape[0]
    idx = jnp.arange(n_row, dtype=jnp.int32)
    keep = lax.cummax(jnp.where(tile_valid != 0, idx, 0), axis=0)
    slab = jnp.tile(tile_slab[keep], n_col)
    col = jnp.repeat(jnp.arange(n_col, dtype=jnp.int32), n_row)
    steps = n_row * n_col
    key = slab * n_col + col
    load = jnp.concatenate([jnp.ones((1,), bool), key[1:] != key[:-1]])
    load_at = jnp.where(load, jnp.arange(steps, dtype=jnp.int32), steps)
    first_from = lax.cummin(load_at, axis=0, reverse=True)
    nxt = jnp.concatenate([first_from[1:], jnp.full((1,), steps, jnp.int32)])
    more = nxt < steps
    nxt = jnp.minimum(nxt, steps - 1)
    return (load.astype(jnp.int32), slab.astype(jnp.int32), col, more.astype(jnp.int32),
            slab[nxt].astype(jnp.int32), col[nxt])


def _gmm(x, ws, tile_slab, tile_valid, out_dtype, name):
    tm, tn = TILES[name]
    m, k = x.shape
    n = ws[0].shape[-1]
    nw = len(ws)
    sched = _weight_schedule(tile_slab, tile_valid, n // tn)
    return pl.pallas_call(
        functools.partial(_gmm_kernel, nw=nw, tn=tn, k_rows=k),
        grid_spec=pltpu.PrefetchScalarGridSpec(
            num_scalar_prefetch=7,
            grid=(n // tn, m // tm),
            in_specs=[pl.BlockSpec((tm, k), lambda j, i, *_: (i, 0))] + [ANY_SPEC] * nw,
            out_specs=pl.BlockSpec((tm, tn), lambda j, i, *_: (i, j)),
            scratch_shapes=([pltpu.VMEM((k, tn), F32)] * nw + [pltpu.VMEM((k, tn), BF16)] * nw
                            + [pltpu.SemaphoreType.DMA((nw,))]),
        ),
        out_shape=jax.ShapeDtypeStruct((m, n), out_dtype),
        compiler_params=_params(("arbitrary", "arbitrary")),
        name=name,
    )(*sched, tile_valid, x, *ws)


def _dense(x, ws, slab, out_dtype, name):
    tiles = x.shape[0] // TILES[name][0]
    return _gmm(x, ws, jnp.full((tiles,), slab, jnp.int32), jnp.ones((tiles,), jnp.int32),
                out_dtype, name)


def _sink_rows(sink_ref, g, rows, per):
    r = lax.broadcasted_iota(jnp.int32, (rows, 1), 0)
    sk = jnp.full((rows, 1), sink_ref[g * Q_PER_KV + Q_PER_KV - 1], F32)
    for h in range(Q_PER_KV - 2, -1, -1):
        sk = jnp.where(r < (h + 1) * per, sink_ref[g * Q_PER_KV + h], sk)
    return sk


def _attn_ctx_kernel(*refs):
    sink_ref, q_ref, k_ref, v_ref = refs[:4]
    o_ref, ko_ref, vo_ref = refs[-3:]
    g = pl.program_id(1)
    scale = HEAD_DIM ** -0.5
    k = k_ref[...]
    v = v_ref[...]
    ko_ref[...] = k
    vo_ref[...] = v
    q = q_ref[...]
    qs = jnp.concatenate([q[:, r * HEAD_DIM:(r + 1) * HEAD_DIM] for r in range(Q_PER_KV)], axis=0)
    s = _dot_nt(qs.astype(BF16), k.astype(BF16)) * scale
    sk = _sink_rows(sink_ref, g, Q_PER_KV * SEQ, SEQ)
    m = jnp.maximum(jnp.max(s, axis=-1, keepdims=True), sk)
    e = jnp.exp(s - m)
    den = jnp.sum(e, axis=-1, keepdims=True) + jnp.exp(sk - m)
    o = jnp.dot(e.astype(BF16), v.astype(BF16), preferred_element_type=F32) / den
    o_ref[...] = jnp.concatenate([o[r * SEQ:(r + 1) * SEQ] for r in range(Q_PER_KV)],
                                 axis=1).astype(o_ref.dtype)


def _attn_ctx(p, sink, layer, mix, kbuf, vbuf):
    kv_spec = pl.BlockSpec((None, None, SEQ, HEAD_DIM), lambda b, g: (b, layer, 0, g))
    kv_shape = jax.ShapeDtypeStruct((BATCH, DEPTH, SEQ, ATT_KV_HEADS * HEAD_DIM), F32)
    in_specs = [
        pl.BlockSpec(memory_space=pltpu.SMEM),
        pl.BlockSpec((SEQ, Q_PER_KV * HEAD_DIM), lambda b, g: (b, g)),
        pl.BlockSpec((SEQ, HEAD_DIM), lambda b, g: (b, COL_K + g)),
        pl.BlockSpec((SEQ, HEAD_DIM), lambda b, g: (b, COL_V + g)),
    ]
    in_specs += [ANY_SPEC] * 3
    args = [sink, p, p, p, mix, kbuf, vbuf]
    aliases = {4: 0, 5: 1, 6: 2}
    return pl.pallas_call(
        _attn_ctx_kernel,
        grid=(BATCH, ATT_KV_HEADS),
        in_specs=in_specs,
        out_specs=[pl.BlockSpec((SEQ, Q_PER_KV * HEAD_DIM), lambda b, g: (b, g)), kv_spec, kv_spec],
        out_shape=[jax.ShapeDtypeStruct((N_TOK, D_MODEL), BF16), kv_shape, kv_shape],
        input_output_aliases=aliases,
        compiler_params=_params(("arbitrary", "arbitrary")),
        name="attn_ctx",
    )(*args)


def _rope(x, c, sa, sb):
    return x * c + pltpu.roll(x, 96, 1) * sa + pltpu.roll(x, 32, 1) * sb


def _attn_lat_kernel(sink_ref, q_ref, k_ref, v_ref, kc_ref, vc_ref, cq_ref, sqa_ref, sqb_ref,
                     ck_ref, ska_ref, skb_ref, mix_ref, o_ref, kr_s, vb_s):
    del mix_ref
    g = pl.program_id(1)
    qb = pl.program_id(2)
    scale = HEAD_DIM ** -0.5
    band = 3 * ATT_BLOCK

    @pl.when(qb == 0)
    def _():
        kr_s[...] = _rope(k_ref[...], ck_ref[...], ska_ref[...], skb_ref[...]).astype(BF16)
        vb_s[...] = v_ref[...].astype(BF16)

    q = q_ref[...]
    cq, sqa, sqb = cq_ref[...], sqa_ref[...], sqb_ref[...]
    qs = jnp.concatenate(
        [_rope(q[:, r * HEAD_DIM:(r + 1) * HEAD_DIM], cq, sqa, sqb) for r in range(Q_PER_KV)],
        axis=0).astype(BF16)
    rows = Q_PER_KV * ATT_BLOCK
    start = pl.multiple_of(jnp.clip(qb - 1, 0, DEC_SEQ // ATT_BLOCK - 3) * ATT_BLOCK, ATT_BLOCK)
    kb = kr_s[pl.ds(start, band), :]
    vb = vb_s[pl.ds(start, band), :]
    s_loc = _dot_nt(qs, kb) * scale
    kpos = start + lax.broadcasted_iota(jnp.int32, (rows, band), 1)
    qpos = qb * ATT_BLOCK + (lax.broadcasted_iota(jnp.int32, (rows, band), 0) & (ATT_BLOCK - 1))
    s_loc = jnp.where(jnp.abs(kpos - qpos) <= WINDOW, s_loc, -jnp.inf)
    s_ctx = _dot_nt(qs, kc_ref[...]) * scale
    sk = _sink_rows(sink_ref, g, rows, ATT_BLOCK)
    m = jnp.maximum(jnp.maximum(jnp.max(s_loc, axis=-1, keepdims=True),
                                jnp.max(s_ctx, axis=-1, keepdims=True)), sk)
    e_loc = jnp.exp(s_loc - m)
    e_ctx = jnp.exp(s_ctx - m)
    den = (jnp.sum(e_loc, axis=-1, keepdims=True) + jnp.sum(e_ctx, axis=-1, keepdims=True)
           + jnp.exp(sk - m))
    o = (jnp.dot(e_loc.astype(BF16), vb, preferred_element_type=F32)
         + jnp.dot(e_ctx.astype(BF16), vc_ref[...], preferred_element_type=F32)) / den
    o_ref[...] = jnp.concatenate([o[r * ATT_BLOCK:(r + 1) * ATT_BLOCK] for r in range(Q_PER_KV)],
                                 axis=1).astype(o_ref.dtype)


def _attn_lat(p, sink, kc, vc, layer, rope, mix):
    nqb = DEC_SEQ // ATT_BLOCK
    row0 = N_PROMPT // DEC_SEQ
    qrow0 = N_PROMPT // ATT_BLOCK
    tq = pl.BlockSpec((ATT_BLOCK, HEAD_DIM), lambda b, g, qb: (qb, 0))
    tk = pl.BlockSpec((DEC_SEQ, HEAD_DIM), lambda b, g, qb: (0, 0))
    ctx = pl.BlockSpec((None, None, None, PAST_LEN, HEAD_DIM), lambda b, g, qb: (layer, b, g, 0, 0))
    cos, sin_a, sin_b = rope
    return pl.pallas_call(
        _attn_lat_kernel,
        grid=(DEC_BATCH, ATT_KV_HEADS, nqb),
        in_specs=[
            pl.BlockSpec(memory_space=pltpu.SMEM),
            pl.BlockSpec((ATT_BLOCK, Q_PER_KV * HEAD_DIM), lambda b, g, qb: (qrow0 + b * nqb + qb, g)),
            pl.BlockSpec((DEC_SEQ, HEAD_DIM), lambda b, g, qb: (row0 + b, COL_K + g)),
            pl.BlockSpec((DEC_SEQ, HEAD_DIM), lambda b, g, qb: (row0 + b, COL_V + g)),
            ctx, ctx, tq, tq, tq, tk, tk, tk, ANY_SPEC,
        ],
        out_specs=pl.BlockSpec((ATT_BLOCK, Q_PER_KV * HEAD_DIM),
                               lambda b, g, qb: (qrow0 + b * nqb + qb, g)),
        out_shape=jax.ShapeDtypeStruct((N_TOK, D_MODEL), BF16),
        scratch_shapes=[pltpu.VMEM((DEC_SEQ, HEAD_DIM), BF16), pltpu.VMEM((DEC_SEQ, HEAD_DIM), BF16)],
        input_output_aliases={12: 0},
        compiler_params=_params(("arbitrary", "arbitrary", "arbitrary")),
        name="attn_lat",
    )(sink, p, p, p, kc, vc, cos, sin_a, sin_b, cos, sin_a, sin_b, mix)


def _rope_tables():
    rows = DEC_SEQ // GRID_W
    row = np.repeat(np.arange(rows), GRID_W).astype(np.float32)
    col = np.tile(np.arange(GRID_W), rows).astype(np.float32)
    axis_dim = HEAD_DIM // 2
    inv = (ROPE_BASE ** (-np.arange(0, axis_dim, 2, dtype=np.float32) / axis_dim)).astype(np.float32)
    lane = np.arange(HEAD_DIM)
    pos = np.where((lane // axis_dim)[None, :] == 0, row[:, None], col[:, None])
    ang = pos * inv[lane % (axis_dim // 2)][None, :]
    first = ((lane % axis_dim) < axis_dim // 2)[None, :]
    cos = np.cos(ang).astype(np.float32)
    sin = np.sin(ang).astype(np.float32)
    sin_a = np.where(first, -sin, 0.0).astype(np.float32)
    sin_b = np.where(first, 0.0, sin).astype(np.float32)
    return jnp.asarray(cos), jnp.asarray(sin_a), jnp.asarray(sin_b)


def _level_tables():
    t = np.arange(CHUNK)
    x = t[:, None] ^ t[None, :]
    lvl = np.where(x > 0, np.floor(np.log2(np.maximum(x, 1))), -1.0).astype(np.float32)
    tri = (t[:, None] >= t[None, :]).astype(np.float32)
    return jnp.asarray(lvl), jnp.asarray(np.stack([tri, tri.T]), dtype=BF16)


def _forget_gate(z, log_lb, log1m_lb):
    t = jnp.log(1.0 + jnp.exp(-jnp.abs(z)))
    c = log1m_lb + jnp.minimum(z, 0.0) - t
    log_f = jnp.maximum(log_lb, c) + jnp.log(1.0 + jnp.exp(-jnp.abs(log_lb - c)))
    return log_f, jnp.exp(log1m_lb - jnp.maximum(z, 0.0) - t)


def _running_log2_decay(g, tri):
    g1 = g.astype(BF16)
    g2 = (g - g1.astype(F32)).astype(BF16)
    dot = functools.partial(jnp.dot, preferred_element_type=F32)
    return (dot(tri, g1) + dot(tri, g2)) * LOG2E


def _intra_chunk_weights(qs, kfs, kbs, bfs, bbs, lvl):
    n = len(qs)
    acc = []
    for c in range(n):
        gram = _dot_nt(qs[c].astype(BF16), (kfs[c] + kbs[c]).astype(BF16))
        acc.append(jnp.where(lvl == -1.0, gram, 0.0))
    shape8 = (CHUNK // 8, 8, HG_DIM)
    sub = lax.broadcasted_iota(jnp.int32, shape8, 1)
    for j in range(N_LEVELS):
        half = 1 << j
        for c in range(n):
            q, kf, kb, bf, bb = qs[c], kfs[c], kbs[c], bfs[c], bbs[c]
            if j < 3:
                q, kf, kb, bf, bb = (a.reshape(shape8) for a in (q, kf, kb, bf, bb))
                upper = (sub & half) != 0
                if j == 0:
                    lhs = q * jnp.where(upper, 1.0 - kf, 1.0 - kb)
                    rhs = jnp.where(upper, kb, kf)
                else:
                    if j == 1:
                        ref_f = jnp.where(sub < 4, bf[:, 1:2, :], bf[:, 5:6, :])
                        ref_b = jnp.where(sub < 4, bb[:, 2:3, :], bb[:, 6:7, :])
                    else:
                        ref_f, ref_b = bf[:, 3:4, :], bb[:, 4:5, :]
                    lhs = q * jnp.exp2(jnp.where(upper, bf - ref_f, bb - ref_b))
                    rhs = jnp.where(upper, kb, kf) * jnp.exp2(jnp.where(upper, ref_b - bb, ref_f - bf))
            else:
                shape = (CHUNK // (2 * half), 2 * half, HG_DIM)
                q, kf, kb, bf, bb = (a.reshape(shape) for a in (q, kf, kb, bf, bb))
                ref_f, ref_b = bf[:, half - 1:half, :], bb[:, half:half + 1, :]
                lo, up = slice(0, half), slice(half, 2 * half)
                lhs = q * jnp.exp2(jnp.concatenate([bb[:, lo] - ref_b, bf[:, up] - ref_f], axis=1))
                rhs = (jnp.concatenate([kf[:, lo], kb[:, up]], axis=1)
                       * jnp.exp2(jnp.concatenate([ref_f - bf[:, lo], ref_b - bb[:, up]], axis=1)))
            gram = _dot_nt(lhs.reshape(CHUNK, HG_DIM).astype(BF16), rhs.reshape(CHUNK, HG_DIM).astype(BF16))
            acc[c] = jnp.where(lvl == float(j), gram, acc[c])
    return acc


HGRN_INTERLEAVE = 4


def _hgrn_kernel(*refs, seq_len, n_sub, has_init):
    refs = list(refs)
    hq_ref, hf_ref, hb_ref, hi_ref, hg_ref, lb_ref, gain_ref, tri_ref, lvl_ref = refs[:9]
    refs = refs[9:]
    init_ref = refs.pop(0) if has_init else None
    refs.pop(0)
    if not has_init:
        refs.pop(0)
    out_ref = refs.pop(0)
    fin_ref = None if has_init else refs.pop(0)
    q_s, vb_s, op_s, qdb_s, ub_s, dcb_s, sf_s = refs
    n_chunks = seq_len // CHUNK
    per = min(HGRN_INTERLEAVE // n_sub, n_chunks)
    slots = [(s, c) for s in range(n_sub) for c in range(per)]
    dot = functools.partial(jnp.dot, preferred_element_type=F32)

    q_s[...] = _silu(hq_ref[...])
    vb_s[...] = hi_ref[...].astype(BF16)

    def chunk_slice(s, ci):
        return pl.ds(pl.multiple_of(s * seq_len + ci * CHUNK, CHUNK), CHUNK)

    def fwd_step(i, carry):
        sls = [chunk_slice(s, per * i + c) for s, c in slots]
        lvl = lvl_ref[...]
        gf = [_forget_gate(hf_ref[sl, :], lb_ref[0:1, :], lb_ref[1:2, :]) for sl in sls]
        gb = [_forget_gate(hb_ref[sl, :], lb_ref[2:3, :], lb_ref[3:4, :]) for sl in sls]
        bfs = [_running_log2_decay(g, tri_ref[0]) for g, _ in gf]
        bbs = [_running_log2_decay(g, tri_ref[1]) for g, _ in gb]
        qs = [q_s[sl, :] for sl in sls]
        kfs = [k for _, k in gf]
        kbs = [k for _, k in gb]
        acc = _intra_chunk_weights(qs, kfs, kbs, bfs, bbs, lvl)
        for n, (s, c) in enumerate(slots):
            sl, q, bf, bb = sls[n], qs[n], bfs[n], bbs[n]
            ci = s * n_chunks + per * i + c
            vb = vb_s[sl, :]
            s_f = sf_s[s]
            qd = (q * jnp.exp2(bf)).astype(BF16)
            op_s[sl, :] = dot(jnp.concatenate([qd, acc[n].astype(BF16)], axis=1),
                              jnp.concatenate([s_f.astype(BF16), vb], axis=0))
            tot_f = bf[CHUNK - 1:CHUNK]
            kd = (kfs[n] * jnp.exp2(tot_f - bf)).astype(BF16)
            dcol = jnp.transpose(jnp.broadcast_to(jnp.exp2(tot_f), (HG_DIM, HG_DIM)))
            sf_s[s] = s_f * dcol + _dot_tn(kd, vb)
            tot_b = bb[0:1]
            qdb_s[sl, :] = (q * jnp.exp2(bb)).astype(BF16)
            kd = (kbs[n] * jnp.exp2(tot_b - bb)).astype(BF16)
            ub_s[ci] = _dot_tn(kd, vb)
            dcb_s[ci] = jnp.transpose(jnp.broadcast_to(jnp.exp2(tot_b), (HG_DIM, HG_DIM)))
        return carry

    zero = jnp.zeros((HG_DIM, HG_DIM), F32)
    for s in range(n_sub):
        sf_s[s] = init_ref[s, 0] if has_init else zero
    lax.fori_loop(0, n_chunks // per, fwd_step, 0)

    def bwd_step(i, states):
        states = list(states)
        outs = []
        for s, c in slots:
            ci = n_chunks - 1 - (per * i + c)
            sl = chunk_slice(s, ci)
            outs.append((sl, op_s[sl, :] + dot(qdb_s[sl, :], states[s].astype(BF16))))
            states[s] = states[s] * dcb_s[s * n_chunks + ci] + ub_s[s * n_chunks + ci]
        for sl, o in outs:
            o = o * lax.rsqrt(jnp.mean(o * o, axis=-1, keepdims=True) + NORM_EPS) * gain_ref[...]
            out_ref[sl, :] = (o * _silu(hg_ref[sl, :])).astype(out_ref.dtype)
        return tuple(states)

    s_b = lax.fori_loop(0, n_chunks // per, bwd_step,
                        tuple(init_ref[s, 1] if has_init else zero for s in range(n_sub)))
    if not has_init:
        for s in range(n_sub):
            fin_ref[s, 0] = sf_s[s]
            fin_ref[s, 1] = s_b[s]


def _hgrn(p, lb_tab, gain, tables, seq_len, n_seq, n_sub, row0, layer, mix, init=None, sbuf=None):
    lvl, tri = tables
    has_init = init is not None
    rows = n_sub * seq_len
    n_chunks = seq_len // CHUNK

    def col(c):
        return pl.BlockSpec((rows, HG_DIM), lambda b, h: (row0 + b, c + h))

    state_spec = pl.BlockSpec((n_sub, None, 2, None, HG_DIM, HG_DIM), lambda b, h: (b, layer, 0, h, 0, 0))
    in_specs = [col(COL_HQ), col(COL_HF), col(COL_HB), col(COL_HI), col(COL_HG),
                pl.BlockSpec((None, 4, HG_DIM), lambda b, h: (layer, 0, h)),
                pl.BlockSpec((1, HG_DIM), lambda b, h: (0, 0)),
                pl.BlockSpec((2, CHUNK, CHUNK), lambda b, h: (0, 0, 0)),
                pl.BlockSpec((CHUNK, CHUNK), lambda b, h: (0, 0))]
    args = [p, p, p, p, p, lb_tab, gain.reshape(1, HG_DIM), tri, lvl]
    if has_init:
        in_specs.append(state_spec)
        args.append(init)
    in_specs.append(ANY_SPEC)
    args.append(mix)
    aliases = {len(args) - 1: 0}
    out_specs = [pl.BlockSpec((rows, HG_DIM), lambda b, h: (row0 + b, MIX_HG + h))]
    out_shape = [jax.ShapeDtypeStruct((N_TOK, D_MODEL), BF16)]
    if not has_init:
        in_specs.append(ANY_SPEC)
        args.append(sbuf)
        aliases[len(args) - 1] = 1
        out_specs.append(state_spec)
        out_shape.append(jax.ShapeDtypeStruct((n_seq, DEPTH, 2, HG_HEADS, HG_DIM, HG_DIM), F32))

    def row_buf(dtype):
        return pltpu.VMEM((rows, HG_DIM), dtype)

    def state_buf(n):
        return pltpu.VMEM((n, HG_DIM, HG_DIM), F32)

    return pl.pallas_call(
        functools.partial(_hgrn_kernel, seq_len=seq_len, n_sub=n_sub, has_init=has_init),
        grid=(n_seq // n_sub, HG_HEADS),
        in_specs=in_specs,
        out_specs=out_specs,
        out_shape=out_shape,
        scratch_shapes=[row_buf(F32), row_buf(BF16), row_buf(F32), row_buf(BF16),
                        state_buf(n_sub * n_chunks), state_buf(n_sub * n_chunks), state_buf(n_sub)],
        input_output_aliases=aliases,
        compiler_params=_params(("arbitrary", "arbitrary")),
        name="hgrn_lat" if has_init else "hgrn_ctx",
    )(*args)


def _router_kernel(h_ref, r_ref, o_ref):
    lg = jnp.dot(h_ref[...], r_ref[...], preferred_element_type=F32)
    lane = lax.broadcasted_iota(jnp.int32, lg.shape, 1)
    lg = jnp.where(lane < N_EXPERTS, lg, -jnp.inf)
    m1 = jnp.max(lg, axis=-1, keepdims=True)
    i1 = jnp.min(jnp.where(lg == m1, lane, HEAD_DIM), axis=-1, keepdims=True)
    lg2 = jnp.where(lane == i1, -jnp.inf, lg)
    m2 = jnp.max(lg2, axis=-1, keepdims=True)
    i2 = jnp.min(jnp.where(lg2 == m2, lane, HEAD_DIM), axis=-1, keepdims=True)
    e2 = jnp.exp(m2 - m1)
    den = 1.0 + e2
    o_ref[...] = jnp.where(lane == 0, i1.astype(F32),
                           jnp.where(lane == 1, i2.astype(F32),
                                     jnp.where(lane == 2, 1.0 / den,
                                               jnp.where(lane == 3, e2 / den, 0.0))))


def _route(h, router, slab0):
    tm = 512
    rp = jnp.pad(router, ((0, 0), (0, HEAD_DIM - N_EXPERTS))).astype(BF16)
    out = pl.pallas_call(
        _router_kernel,
        grid=(N_TOK // tm,),
        in_specs=[pl.BlockSpec((tm, D_MODEL), lambda i: (i, 0)),
                  pl.BlockSpec((D_MODEL, HEAD_DIM), lambda i: (0, 0))],
        out_specs=pl.BlockSpec((tm, HEAD_DIM), lambda i: (i, 0)),
        out_shape=jax.ShapeDtypeStruct((N_TOK, HEAD_DIM), F32),
        compiler_params=_params(("arbitrary",)),
        name="router",
    )(h, rp)
    top_i = out[:, 0:2].astype(jnp.int32)
    gates = out[:, 2:4]
    e_flat = top_i.reshape(-1)
    onehot = (e_flat[:, None] == jnp.arange(N_EXPERTS)[None, :]).astype(jnp.int32)
    csum = jnp.cumsum(onehot, axis=0)
    counts = csum[-1]
    padded = ((counts + MOE_TM - 1) // MOE_TM) * MOE_TM
    ends = jnp.cumsum(padded)
    starts = ends - padded
    pos = jnp.sum(onehot * (csum - 1 + starts[None, :]), axis=1)
    row_token = jnp.zeros((MOE_ROWS,), jnp.int32).at[pos].set(jnp.arange(2 * N_TOK, dtype=jnp.int32) // 2)
    tile_start = jnp.arange(MOE_TILES, dtype=jnp.int32) * MOE_TM
    tile_expert = jnp.minimum(jnp.sum(tile_start[:, None] >= ends[None, :], axis=1), N_EXPERTS - 1)
    tile_valid = (tile_start < ends[-1]).astype(jnp.int32)
    return gates, pos.reshape(N_TOK, 2), row_token, (slab0 + tile_expert).astype(jnp.int32), tile_valid


def _rows(table, idx):
    return table.at[idx].get(mode="promise_in_bounds")


def kernel(x_prompt, x_sample, cache_k, cache_v, state_hgrn, c, c_ctx, w_mod, b_mod, norm_gains,
           w_in, w_out, attn_sink, hg_lb_logits, hg_norm_gain, ffn_w1, ffn_w3, ffn_w2,
           moe_router, moe_w1, moe_w3, moe_w2):
    x = jnp.concatenate([x_prompt.reshape(N_PROMPT, D_MODEL), x_sample.reshape(N_SAMPLE, D_MODEL)], axis=0)

    cvec = jnp.concatenate([c_ctx[None], c, jnp.zeros((3, D_MODEL), F32)], axis=0)
    mod = _modulation(cvec, w_mod, b_mod).reshape(DEPTH, 8, 6, D_MODEL)
    mod = jnp.pad(mod, ((0, 0), (0, 0), (0, 2), (0, 0)))
    group_src = np.array([0] * (N_PROMPT // MOD_GROUP) + list(range(1, 1 + DEC_BATCH)))
    mod = mod[:, group_src]

    lb_cum = jnp.cumsum(jax.nn.softmax(hg_lb_logits.astype(F32), axis=0), axis=0)
    lb = lb_cum - lb_cum[0:1]
    lb_tab = jnp.stack([jnp.log(lb[:, 0]), jnp.log1p(-lb[:, 0]),
                        jnp.log(lb[:, 1]), jnp.log1p(-lb[:, 1])], axis=1)

    rope = _rope_tables()
    tables = _level_tables()
    kc = cache_k.transpose(1, 0, 3, 2, 4).astype(BF16)
    vc = cache_v.transpose(1, 0, 3, 2, 4).astype(BF16)
    ffn_w = [ffn_w1, ffn_w3, ffn_w2]
    moe_w = [w.reshape((-1,) + w.shape[2:]) for w in (moe_w1, moe_w3, moe_w2)]

    (h,) = _resnorm(x, [], None, None, 0, norm_gains[0, 0], mod[0], 1, 0)
    mix = jnp.zeros((N_TOK, D_MODEL), BF16)
    kbuf = jnp.zeros((BATCH, DEPTH, SEQ, ATT_KV_HEADS * HEAD_DIM), F32)
    vbuf = jnp.zeros((BATCH, DEPTH, SEQ, ATT_KV_HEADS * HEAD_DIM), F32)
    sbuf = jnp.zeros((BATCH, DEPTH, 2, HG_HEADS, HG_DIM, HG_DIM), F32)
    for l in range(DEPTH):
        i = l // 2
        moe = l % 2 == 1
        p = _dense(h, [w_in], l, F32, "proj_in")
        mix, kbuf, vbuf = _attn_ctx(p, attn_sink[l], l, mix, kbuf, vbuf)
        mix = _attn_lat(p, attn_sink[l], kc, vc, l, rope, mix)
        mix, sbuf = _hgrn(p, lb_tab, hg_norm_gain[l], tables, SEQ, BATCH, 2, 0, l, mix, sbuf=sbuf)
        (mix,) = _hgrn(p, lb_tab, hg_norm_gain[l], tables, DEC_SEQ, DEC_BATCH, 1, N_PROMPT // DEC_SEQ,
                       l, mix, init=state_hgrn)
        m = _dense(mix, [w_out], l, F32, "proj_out")
        outs = _resnorm(x, [m], norm_gains[l, 1], mod[l], 2, norm_gains[l, 2], mod[l], 4, 3, wide_h=moe)
        x, h = outs[0], outs[1]
        if not moe:
            act = _dense(h, ffn_w[:2], i, BF16, "ffn_up")
            f = [_dense(act, ffn_w[2:], i, F32, "ffn_down")]
            gates = None
        else:
            gates, pos, row_token, tile_slab, tile_valid = _route(h, moe_router[i], i * N_EXPERTS)
            xs = _rows(outs[2], row_token)
            act = _gmm(xs, moe_w[:2], tile_slab, tile_valid, BF16, "moe_up")
            ys = _gmm(act, moe_w[2:], tile_slab, tile_valid, F32, "moe_down")
            f = [_rows(ys, pos[:, 0]), _rows(ys, pos[:, 1])]
        if l + 1 < DEPTH:
            x, h = _resnorm(x, f, norm_gains[l, 3], mod[l], 5, norm_gains[l + 1, 0], mod[l + 1], 1, 0,
                            gates=gates)
        else:
            (x,) = _resnorm(x, f, norm_gains[l, 3], mod[l], 5, None, None, 0, 0, gates=gates)
    kv_shape = (BATCH, DEPTH, SEQ, ATT_KV_HEADS, HEAD_DIM)
    return (x[:N_PROMPT].reshape(BATCH, SEQ, D_MODEL), x[N_PROMPT:].reshape(DEC_BATCH, DEC_SEQ, D_MODEL),
            kbuf.reshape(kv_shape), vbuf.reshape(kv_shape), sbuf)
```

```python
import functools

import numpy as np
import jax
import jax.numpy as jnp
from jax import lax
from jax.experimental import pallas as pl
from jax.experimental.pallas import tpu as pltpu

F32 = jnp.float32
BF16 = jnp.bfloat16

D_MODEL = 2048
BATCH = 16
SEQ = 256
DEPTH = 4
DEC_BATCH = 4
DEC_SEQ = 1024
PAST_LEN = 256
GRID_W = 64
HEAD_DIM = 128
ATT_HEADS = 8
ATT_KV_HEADS = 2
Q_PER_KV = 4
ATT_WIDTH = 1024
WINDOW = 128
ATT_BLOCK = 128
ROPE_BASE = 10000.0
HG_WIDTH = 1024
HG_HEADS = 8
HG_DIM = 128
N_EXPERTS = 8
D_FF = 5632
D_FF_EXPERT = 2816
NORM_EPS = 1e-6
IN_COLS = 6656

N_PROMPT = BATCH * SEQ
N_SAMPLE = DEC_BATCH * DEC_SEQ
N_TOK = N_PROMPT + N_SAMPLE
MOD_GROUP = 1024
N_GROUPS = N_TOK // MOD_GROUP

COL_K = 8
COL_V = 10
COL_HQ = 12
COL_HF = 20
COL_HB = 28
COL_HI = 36
COL_HG = 44
MIX_HG = ATT_WIDTH // HG_DIM

CHUNK = 128
N_LEVELS = 7
LOG2E = 1.4426950408889634

VMEM_LIMIT = 60 * 1024 * 1024

TILES = {
    "proj_in": (1024, 1664),
    "proj_out": (512, 2048),
    "ffn_up": (512, 1408),
    "ffn_down": (512, 1024),
    "moe_up": (256, 1408),
    "moe_down": (256, 2048),
}

MOE_TM = TILES["moe_up"][0]
MOE_ROWS = 2 * N_TOK + N_EXPERTS * MOE_TM
MOE_TILES = MOE_ROWS // MOE_TM

ANY_SPEC = pl.BlockSpec(memory_space=pl.ANY)


def _params(sem):
    return pltpu.CompilerParams(dimension_semantics=sem, vmem_limit_bytes=VMEM_LIMIT)


def _silu(x):
    return x / (1.0 + jnp.exp(-x))


def _dot_nt(a, b):
    return lax.dot_general(a, b, (((1,), (1,)), ((), ())), preferred_element_type=F32)


def _dot_tn(a, b):
    return lax.dot_general(a, b, (((0,), (0,)), ((), ())), preferred_element_type=F32)


def _mod_kernel(c_ref, w_ref, b_ref, o_ref):
    s = _silu(c_ref[...]).astype(BF16)
    o_ref[...] = jnp.dot(s, w_ref[...].astype(BF16), preferred_element_type=F32) + b_ref[...]


def _modulation(cvec, w_mod, b_mod):
    tn = 1024
    n = 6 * D_MODEL
    return pl.pallas_call(
        _mod_kernel,
        grid=(DEPTH, n // tn),
        in_specs=[
            pl.BlockSpec((8, D_MODEL), lambda l, j: (0, 0)),
            pl.BlockSpec((None, D_MODEL, tn), lambda l, j: (l, 0, j)),
            pl.BlockSpec((None, 1, tn), lambda l, j: (l, 0, j)),
        ],
        out_specs=pl.BlockSpec((None, 8, tn), lambda l, j: (l, 0, j)),
        out_shape=jax.ShapeDtypeStruct((DEPTH, 8, n), F32),
        compiler_params=_params(("arbitrary", "arbitrary")),
        name="modulation",
    )(cvec, w_mod, b_mod.reshape(DEPTH, 1, n))


def _rms(x, gain):
    return x * lax.rsqrt(jnp.mean(x * x, axis=-1, keepdims=True) + NORM_EPS) * gain


def _resnorm_kernel(*refs, n_m, gate_row, has_h, sc_row, sh_row, wide_h):
    refs = list(refs)
    x_ref = refs.pop(0)
    x = x_ref[...]
    if n_m == 1:
        m = refs.pop(0)[...]
    elif n_m == 2:
        y0 = refs.pop(0)[...]
        y1 = refs.pop(0)[...]
        gt = refs.pop(0)[...]
        m = gt[:, 0:1] * y0 + gt[:, 1:2] * y1
    if n_m:
        ga = refs.pop(0)[...]
        modg = refs.pop(0)
        x = x + modg[gate_row:gate_row + 1, :] * _rms(m, ga)
    if has_h:
        gb = refs.pop(0)[...]
        modh = refs.pop(0)
        h = _rms(x, gb) * (1.0 + modh[sc_row:sc_row + 1, :]) + modh[sh_row:sh_row + 1, :]
    if n_m:
        refs.pop(0)[...] = x
    if has_h:
        refs.pop(0)[...] = h.astype(BF16)
        if wide_h:
            refs.pop(0)[...] = h


def _resnorm(x, ms, gain_m, mod_gate, gate_row, gain_h, mod_h, sc_row, sh_row, gates=None, wide_h=False):
    tm = 256
    per = MOD_GROUP // tm
    n_m = len(ms)
    has_h = gain_h is not None
    row = pl.BlockSpec((tm, D_MODEL), lambda i: (i, 0))
    vec = pl.BlockSpec((1, D_MODEL), lambda i: (0, 0))
    mod = pl.BlockSpec((None, 8, D_MODEL), lambda i: (i // per, 0, 0))
    args, specs = [x], [row]
    for m in ms:
        args.append(m)
        specs.append(row)
    if n_m == 2:
        args.append(gates)
        specs.append(pl.BlockSpec((tm, 2), lambda i: (i, 0)))
    if n_m:
        args += [gain_m.reshape(1, D_MODEL), mod_gate]
        specs += [vec, mod]
    if has_h:
        args += [gain_h.reshape(1, D_MODEL), mod_h]
        specs += [vec, mod]
    out_shape, out_specs = [], []
    if n_m:
        out_shape.append(jax.ShapeDtypeStruct((N_TOK, D_MODEL), F32))
        out_specs.append(row)
    if has_h:
        out_shape.append(jax.ShapeDtypeStruct((N_TOK, D_MODEL), BF16))
        out_specs.append(row)
        if wide_h:
            out_shape.append(jax.ShapeDtypeStruct((N_TOK, D_MODEL), F32))
            out_specs.append(row)
    outs = pl.pallas_call(
        functools.partial(_resnorm_kernel, n_m=n_m, gate_row=gate_row, has_h=has_h,
                          sc_row=sc_row, sh_row=sh_row, wide_h=wide_h),
        grid=(N_TOK // tm,),
        in_specs=specs,
        out_specs=out_specs,
        out_shape=out_shape,
        compiler_params=_params(("arbitrary",)),
        name="resnorm",
    )(*args)
    return outs


CAST_ROWS = 256


def _gmm_kernel(load_ref, slab_ref, col_ref, more_ref, nslab_ref, ncol_ref, valid_ref, x_ref, *rest,
                nw, tn, k_rows):
    w_hbm = rest[:nw]
    o_ref = rest[nw]
    stage = rest[nw + 1:2 * nw + 1]
    wb = rest[2 * nw + 1:3 * nw + 1]
    sem = rest[3 * nw + 1]
    t = pl.program_id(0) * pl.num_programs(1) + pl.program_id(1)

    def copies(slab, col):
        cols = pl.ds(pl.multiple_of(col * tn, 128), tn)
        return [pltpu.make_async_copy(w_hbm[k].at[slab, :, cols], stage[k], sem.at[k]) for k in range(nw)]

    @pl.when(t == 0)
    def _():
        for cp in copies(slab_ref[0], col_ref[0]):
            cp.start()

    @pl.when(load_ref[t] != 0)
    def _():
        for cp in copies(slab_ref[t], col_ref[t]):
            cp.wait()

        def cast(r, carry):
            rows = pl.ds(pl.multiple_of(r * CAST_ROWS, CAST_ROWS), CAST_ROWS)
            for k in range(nw):
                wb[k][rows, :] = stage[k][rows, :].astype(BF16)
            return carry

        lax.fori_loop(0, k_rows // CAST_ROWS, cast, 0)

        @pl.when(more_ref[t] != 0)
        def _():
            for cp in copies(nslab_ref[t], ncol_ref[t]):
                cp.start()

    valid = valid_ref[pl.program_id(1)] != 0

    @pl.when(valid)
    def _():
        x = x_ref[...].astype(BF16)
        a = jnp.dot(x, wb[0][...], preferred_element_type=F32)
        if nw == 2:
            a = _silu(a) * jnp.dot(x, wb[1][...], preferred_element_type=F32)
        o_ref[...] = a.astype(o_ref.dtype)

    @pl.when(jnp.logical_not(valid))
    def _():
        o_ref[...] = jnp.zeros_like(o_ref)


def _weight_schedule(tile_slab, tile_valid, n_col):
    n_row = tile_slab.shape[0]
    idx = jnp.arange(n_row, dtype=jnp.int32)
    keep = lax.cummax(jnp.where(tile_valid != 0, idx, 0), axis=0)
    slab = jnp.tile(tile_slab[keep], n_col)
    col = jnp.repeat(jnp.arange(n_col, dtype=jnp.int32), n_row)
    steps = n_row * n_col
    key = slab * n_col + col
    load = jnp.concatenate([jnp.ones((1,), bool), key[1:] != key[:-1]])
    load_at = jnp.where(load, jnp.arange(steps, dtype=jnp.int32), steps)
    first_from = lax.cummin(load_at, axis=0, reverse=True)
    nxt = jnp.concatenate([first_from[1:], jnp.full((1,), steps, jnp.int32)])
    more = nxt < steps
    nxt = jnp.minimum(nxt, steps - 1)
    return (load.astype(jnp.int32), slab.astype(jnp.int32), col, more.astype(jnp.int32),
            slab[nxt].astype(jnp.int32), col[nxt])


def _gmm(x, ws, tile_slab, tile_valid, out_dtype, name):
    tm, tn = TILES[name]
    m, k = x.shape
    n = ws[0].shape[-1]
    nw = len(ws)
    sched = _weight_schedule(tile_slab, tile_valid, n // tn)
    return pl.pallas_call(
        functools.partial(_gmm_kernel, nw=nw, tn=tn, k_rows=k),
        grid_spec=pltpu.PrefetchScalarGridSpec(
            num_scalar_prefetch=7,
            grid=(n // tn, m // tm),
            in_specs=[pl.BlockSpec((tm, k), lambda j, i, *_: (i, 0))] + [ANY_SPEC] * nw,
            out_specs=pl.BlockSpec((tm, tn), lambda j, i, *_: (i, j)),
            scratch_shapes=([pltpu.VMEM((k, tn), F32)] * nw + [pltpu.VMEM((k, tn), BF16)] * nw
                            + [pltpu.SemaphoreType.DMA((nw,))]),
        ),
        out_shape=jax.ShapeDtypeStruct((m, n), out_dtype),
        compiler_params=_params(("arbitrary", "arbitrary")),
        name=name,
    )(*sched, tile_valid, x, *ws)


def _dense(x, ws, slab, out_dtype, name):
    tiles = x.shape[0] // TILES[name][0]
    return _gmm(x, ws, jnp.full((tiles,), slab, jnp.int32), jnp.ones((tiles,), jnp.int32),
                out_dtype, name)


def _sink_rows(sink_ref, g, rows, per):
    r = lax.broadcasted_iota(jnp.int32, (rows, 1), 0)
    sk = jnp.full((rows, 1), sink_ref[g * Q_PER_KV + Q_PER_KV - 1], F32)
    for h in range(Q_PER_KV - 2, -1, -1):
        sk = jnp.where(r < (h + 1) * per, sink_ref[g * Q_PER_KV + h], sk)
    return sk


def _attn_ctx_kernel(*refs):
    sink_ref, q_ref, k_ref, v_ref = refs[:4]
    o_ref, ko_ref, vo_ref = refs[-3:]
    g = pl.program_id(1)
    scale = HEAD_DIM ** -0.5
    k = k_ref[...]
    v = v_ref[...]
    ko_ref[...] = k
    vo_ref[...] = v
    q = q_ref[...]
    qs = jnp.concatenate([q[:, r * HEAD_DIM:(r + 1) * HEAD_DIM] for r in range(Q_PER_KV)], axis=0) * scale
    s = _dot_nt(qs.astype(BF16), k.astype(BF16))
    sk = _sink_rows(sink_ref, g, Q_PER_KV * SEQ, SEQ)
    m = jnp.maximum(jnp.max(s, axis=-1, keepdims=True), sk)
    e = jnp.exp(s - m)
    den = jnp.sum(e, axis=-1, keepdims=True) + jnp.exp(sk - m)
    o = jnp.dot(e.astype(BF16), v.astype(BF16), preferred_element_type=F32) / den
    o_ref[...] = jnp.concatenate([o[r * SEQ:(r + 1) * SEQ] for r in range(Q_PER_KV)],
                                 axis=1).astype(o_ref.dtype)


def _attn_ctx(p, sink, layer, mix, kbuf, vbuf):
    kv_spec = pl.BlockSpec((None, None, SEQ, HEAD_DIM), lambda b, g: (b, layer, 0, g))
    kv_shape = jax.ShapeDtypeStruct((BATCH, DEPTH, SEQ, ATT_KV_HEADS * HEAD_DIM), F32)
    in_specs = [
        pl.BlockSpec(memory_space=pltpu.SMEM),
        pl.BlockSpec((SEQ, Q_PER_KV * HEAD_DIM), lambda b, g: (b, g)),
        pl.BlockSpec((SEQ, HEAD_DIM), lambda b, g: (b, COL_K + g)),
        pl.BlockSpec((SEQ, HEAD_DIM), lambda b, g: (b, COL_V + g)),
    ]
    in_specs += [ANY_SPEC] * 3
    args = [sink, p, p, p, mix, kbuf, vbuf]
    aliases = {4: 0, 5: 1, 6: 2}
    return pl.pallas_call(
        _attn_ctx_kernel,
        grid=(BATCH, ATT_KV_HEADS),
        in_specs=in_specs,
        out_specs=[pl.BlockSpec((SEQ, Q_PER_KV * HEAD_DIM), lambda b, g: (b, g)), kv_spec, kv_spec],
        out_shape=[jax.ShapeDtypeStruct((N_TOK, D_MODEL), BF16), kv_shape, kv_shape],
        input_output_aliases=aliases,
        compiler_params=_params(("arbitrary", "arbitrary")),
        name="attn_ctx",
    )(*args)


def _rope(x, c, sa, sb):
    return x * c + pltpu.roll(x, 96, 1) * sa + pltpu.roll(x, 32, 1) * sb


def _attn_lat_kernel(sink_ref, q_ref, k_ref, v_ref, kc_ref, vc_ref, cq_ref, sqa_ref, sqb_ref,
                     ck_ref, ska_ref, skb_ref, bias_ref, mix_ref, o_ref, kr_s, vb_s):
    del mix_ref
    g = pl.program_id(1)
    qb = pl.program_id(2)
    scale = HEAD_DIM ** -0.5
    band = 3 * ATT_BLOCK

    @pl.when(qb == 0)
    def _():
        kr_s[...] = _rope(k_ref[...], ck_ref[...], ska_ref[...], skb_ref[...]).astype(BF16)
        vb_s[...] = v_ref[...].astype(BF16)

    q = q_ref[...]
    cq, sqa, sqb = cq_ref[...], sqa_ref[...], sqb_ref[...]
    qs = jnp.concatenate(
        [_rope(q[:, r * HEAD_DIM:(r + 1) * HEAD_DIM], cq, sqa, sqb) * scale for r in range(Q_PER_KV)],
        axis=0).astype(BF16)
    rows = Q_PER_KV * ATT_BLOCK
    start = pl.multiple_of(jnp.clip(qb - 1, 0, DEC_SEQ // ATT_BLOCK - 3) * ATT_BLOCK, ATT_BLOCK)
    kb = kr_s[pl.ds(start, band), :]
    vb = vb_s[pl.ds(start, band), :]
    s_loc = _dot_nt(qs, kb) + bias_ref[...]
    s_ctx = _dot_nt(qs, kc_ref[...])
    sk = _sink_rows(sink_ref, g, rows, ATT_BLOCK)
    m = jnp.maximum(jnp.maximum(jnp.max(s_loc, axis=-1, keepdims=True),
                                jnp.max(s_ctx, axis=-1, keepdims=True)), sk)
    e_loc = jnp.exp(s_loc - m)
    e_ctx = jnp.exp(s_ctx - m)
    den = (jnp.sum(e_loc, axis=-1, keepdims=True) + jnp.sum(e_ctx, axis=-1, keepdims=True)
           + jnp.exp(sk - m))
    o = (jnp.dot(e_loc.astype(BF16), vb, preferred_element_type=F32)
         + jnp.dot(e_ctx.astype(BF16), vc_ref[...], preferred_element_type=F32)) / den
    o_ref[...] = jnp.concatenate([o[r * ATT_BLOCK:(r + 1) * ATT_BLOCK] for r in range(Q_PER_KV)],
                                 axis=1).astype(o_ref.dtype)


def _attn_lat(p, sink, kc, vc, layer, rope, mix):
    nqb = DEC_SEQ // ATT_BLOCK
    row0 = N_PROMPT // DEC_SEQ
    qrow0 = N_PROMPT // ATT_BLOCK
    tq = pl.BlockSpec((ATT_BLOCK, HEAD_DIM), lambda b, g, qb: (qb, 0))
    tk = pl.BlockSpec((DEC_SEQ, HEAD_DIM), lambda b, g, qb: (0, 0))
    ctx = pl.BlockSpec((None, None, None, PAST_LEN, HEAD_DIM), lambda b, g, qb: (layer, b, g, 0, 0))
    cos, sin_a, sin_b = rope
    bias = pl.BlockSpec((None, Q_PER_KV * ATT_BLOCK, 3 * ATT_BLOCK),
                        lambda b, g, qb: (jnp.where(qb == 0, 0, jnp.where(qb == nqb - 1, 2, 1)), 0, 0))
    return pl.pallas_call(
        _attn_lat_kernel,
        grid=(DEC_BATCH, ATT_KV_HEADS, nqb),
        in_specs=[
            pl.BlockSpec(memory_space=pltpu.SMEM),
            pl.BlockSpec((ATT_BLOCK, Q_PER_KV * HEAD_DIM), lambda b, g, qb: (qrow0 + b * nqb + qb, g)),
            pl.BlockSpec((DEC_SEQ, HEAD_DIM), lambda b, g, qb: (row0 + b, COL_K + g)),
            pl.BlockSpec((DEC_SEQ, HEAD_DIM), lambda b, g, qb: (row0 + b, COL_V + g)),
            ctx, ctx, tq, tq, tq, tk, tk, tk, bias, ANY_SPEC,
        ],
        out_specs=pl.BlockSpec((ATT_BLOCK, Q_PER_KV * HEAD_DIM),
                               lambda b, g, qb: (qrow0 + b * nqb + qb, g)),
        out_shape=jax.ShapeDtypeStruct((N_TOK, D_MODEL), BF16),
        scratch_shapes=[pltpu.VMEM((DEC_SEQ, HEAD_DIM), BF16), pltpu.VMEM((DEC_SEQ, HEAD_DIM), BF16)],
        input_output_aliases={13: 0},
        compiler_params=_params(("arbitrary", "arbitrary", "arbitrary")),
        name="attn_lat",
    )(sink, p, p, p, kc, vc, cos, sin_a, sin_b, cos, sin_a, sin_b, _band_bias(), mix)


def _band_bias():
    row = np.arange(Q_PER_KV * ATT_BLOCK)[:, None] % ATT_BLOCK
    lane = np.arange(3 * ATT_BLOCK)[None, :]
    cases = [lane - k * ATT_BLOCK - row for k in range(3)]
    return jnp.asarray(np.stack([np.where(np.abs(d) <= WINDOW, 0.0, -np.inf) for d in cases]).astype(np.float32))


def _rope_tables():
    rows = DEC_SEQ // GRID_W
    row = np.repeat(np.arange(rows), GRID_W).astype(np.float32)
    col = np.tile(np.arange(GRID_W), rows).astype(np.float32)
    axis_dim = HEAD_DIM // 2
    inv = (ROPE_BASE ** (-np.arange(0, axis_dim, 2, dtype=np.float32) / axis_dim)).astype(np.float32)
    lane = np.arange(HEAD_DIM)
    pos = np.where((lane // axis_dim)[None, :] == 0, row[:, None], col[:, None])
    ang = pos * inv[lane % (axis_dim // 2)][None, :]
    first = ((lane % axis_dim) < axis_dim // 2)[None, :]
    cos = np.cos(ang).astype(np.float32)
    sin = np.sin(ang).astype(np.float32)
    sin_a = np.where(first, -sin, 0.0).astype(np.float32)
    sin_b = np.where(first, 0.0, sin).astype(np.float32)
    return jnp.asarray(cos), jnp.asarray(sin_a), jnp.asarray(sin_b)


def _level_tables():
    t = np.arange(CHUNK)
    x = t[:, None] ^ t[None, :]
    lvl = np.where(x > 0, np.floor(np.log2(np.maximum(x, 1))), -1.0).astype(np.float32)
    tri = (t[:, None] >= t[None, :]).astype(np.float32)
    return jnp.asarray(lvl), jnp.asarray(np.stack([tri, tri.T]), dtype=BF16)


def _forget_gate(z, log_lb, log1m_lb):
    t = jnp.log(1.0 + jnp.exp(-jnp.abs(z)))
    c = log1m_lb + jnp.minimum(z, 0.0) - t
    log_f = jnp.maximum(log_lb, c) + jnp.log(1.0 + jnp.exp(-jnp.abs(log_lb - c)))
    return log_f, jnp.exp(log1m_lb - jnp.maximum(z, 0.0) - t)


def _running_log2_decay(g, tri):
    g1 = g.astype(BF16)
    g2 = (g - g1.astype(F32)).astype(BF16)
    dot = functools.partial(jnp.dot, preferred_element_type=F32)
    return (dot(tri, g1) + dot(tri, g2)) * LOG2E


def _intra_chunk_weights(qs, kfs, kbs, bfs, bbs, lvl):
    n = len(qs)
    acc = []
    for c in range(n):
        gram = _dot_nt(qs[c].astype(BF16), (kfs[c] + kbs[c]).astype(BF16))
        acc.append(jnp.where(lvl == -1.0, gram, 0.0))
    shape8 = (CHUNK // 8, 8, HG_DIM)
    sub = lax.broadcasted_iota(jnp.int32, shape8, 1)
    for j in range(N_LEVELS):
        half = 1 << j
        for c in range(n):
            q, kf, kb, bf, bb = qs[c], kfs[c], kbs[c], bfs[c], bbs[c]
            if j < 3:
                q, kf, kb, bf, bb = (a.reshape(shape8) for a in (q, kf, kb, bf, bb))
                upper = (sub & half) != 0
                if j == 0:
                    lhs = q * jnp.where(upper, 1.0 - kf, 1.0 - kb)
                    rhs = jnp.where(upper, kb, kf)
                else:
                    if j == 1:
                        ref_f = jnp.where(sub < 4, bf[:, 1:2, :], bf[:, 5:6, :])
                        ref_b = jnp.where(sub < 4, bb[:, 2:3, :], bb[:, 6:7, :])
                    else:
                        ref_f, ref_b = bf[:, 3:4, :], bb[:, 4:5, :]
                    lhs = q * jnp.exp2(jnp.where(upper, bf - ref_f, bb - ref_b))
                    rhs = jnp.where(upper, kb, kf) * jnp.exp2(jnp.where(upper, ref_b - bb, ref_f - bf))
            else:
                shape = (CHUNK // (2 * half), 2 * half, HG_DIM)
                q, kf, kb, bf, bb = (a.reshape(shape) for a in (q, kf, kb, bf, bb))
                ref_f, ref_b = bf[:, half - 1:half, :], bb[:, half:half + 1, :]
                lo, up = slice(0, half), slice(half, 2 * half)
                lhs = q * jnp.exp2(jnp.concatenate([bb[:, lo] - ref_b, bf[:, up] - ref_f], axis=1))
                rhs = (jnp.concatenate([kf[:, lo], kb[:, up]], axis=1)
                       * jnp.exp2(jnp.concatenate([ref_f - bf[:, lo], ref_b - bb[:, up]], axis=1)))
            gram = _dot_nt(lhs.reshape(CHUNK, HG_DIM).astype(BF16), rhs.reshape(CHUNK, HG_DIM).astype(BF16))
            acc[c] = jnp.where(lvl == float(j), gram, acc[c])
    return acc


HGRN_INTERLEAVE = 8


def _hgrn_kernel(*refs, seq_len, n_sub, has_init):
    refs = list(refs)
    hq_ref, hf_ref, hb_ref, hi_ref, hg_ref, lb_ref, gain_ref, tri_ref, lvl_ref = refs[:9]
    refs = refs[9:]
    init_ref = refs.pop(0) if has_init else None
    refs.pop(0)
    if not has_init:
        refs.pop(0)
    out_ref = refs.pop(0)
    fin_ref = None if has_init else refs.pop(0)
    q_s, vb_s, op_s, qdb_s, ub_s, dcb_s, sf_s = refs
    n_chunks = seq_len // CHUNK
    per = min(HGRN_INTERLEAVE // n_sub, n_chunks)
    slots = [(s, c) for s in range(n_sub) for c in range(per)]
    dot = functools.partial(jnp.dot, preferred_element_type=F32)

    q_s[...] = _silu(hq_ref[...])
    vb_s[...] = hi_ref[...].astype(BF16)

    def chunk_slice(s, ci):
        return pl.ds(pl.multiple_of(s * seq_len + ci * CHUNK, CHUNK), CHUNK)

    def fwd_step(i, carry):
        sls = [chunk_slice(s, per * i + c) for s, c in slots]
        lvl = lvl_ref[...]
        gf = [_forget_gate(hf_ref[sl, :], lb_ref[0:1, :], lb_ref[1:2, :]) for sl in sls]
        gb = [_forget_gate(hb_ref[sl, :], lb_ref[2:3, :], lb_ref[3:4, :]) for sl in sls]
        bfs = [_running_log2_decay(g, tri_ref[0]) for g, _ in gf]
        bbs = [_running_log2_decay(g, tri_ref[1]) for g, _ in gb]
        qs = [q_s[sl, :] for sl in sls]
        kfs = [k for _, k in gf]
        kbs = [k for _, k in gb]
        acc = _intra_chunk_weights(qs, kfs, kbs, bfs, bbs, lvl)
        for n, (s, c) in enumerate(slots):
            sl, q, bf, bb = sls[n], qs[n], bfs[n], bbs[n]
            ci = s * n_chunks + per * i + c
            vb = vb_s[sl, :]
            s_f = sf_s[s]
            qd = (q * jnp.exp2(bf)).astype(BF16)
            op_s[sl, :] = dot(jnp.concatenate([qd, acc[n].astype(BF16)], axis=1),
                              jnp.concatenate([s_f.astype(BF16), vb], axis=0))
            tot_f = bf[CHUNK - 1:CHUNK]
            kd = (kfs[n] * jnp.exp2(tot_f - bf)).astype(BF16)
            dcol = jnp.transpose(jnp.broadcast_to(jnp.exp2(tot_f), (HG_DIM, HG_DIM)))
            sf_s[s] = s_f * dcol + _dot_tn(kd, vb)
            tot_b = bb[0:1]
            qdb_s[sl, :] = (q * jnp.exp2(bb)).astype(BF16)
            kd = (kbs[n] * jnp.exp2(tot_b - bb)).astype(BF16)
            ub_s[ci] = _dot_tn(kd, vb)
            dcb_s[ci] = jnp.transpose(jnp.broadcast_to(jnp.exp2(tot_b), (HG_DIM, HG_DIM)))
        return carry

    zero = jnp.zeros((HG_DIM, HG_DIM), F32)
    for s in range(n_sub):
        sf_s[s] = init_ref[s, 0] if has_init else zero
    lax.fori_loop(0, n_chunks // per, fwd_step, 0)

    def bwd_step(i, states):
        states = list(states)
        outs = []
        for s, c in slots:
            ci = n_chunks - 1 - (per * i + c)
            sl = chunk_slice(s, ci)
            outs.append((sl, op_s[sl, :] + dot(qdb_s[sl, :], states[s].astype(BF16))))
            states[s] = states[s] * dcb_s[s * n_chunks + ci] + ub_s[s * n_chunks + ci]
        for sl, o in outs:
            o = o * lax.rsqrt(jnp.mean(o * o, axis=-1, keepdims=True) + NORM_EPS) * gain_ref[...]
            out_ref[sl, :] = (o * _silu(hg_ref[sl, :])).astype(out_ref.dtype)
        return tuple(states)

    s_b = lax.fori_loop(0, n_chunks // per, bwd_step,
                        tuple(init_ref[s, 1] if has_init else zero for s in range(n_sub)))
    if not has_init:
        for s in range(n_sub):
            fin_ref[s, 0] = sf_s[s]
            fin_ref[s, 1] = s_b[s]


def _hgrn(p, lb_tab, gain, tables, seq_len, n_seq, n_sub, row0, layer, mix, init=None, sbuf=None):
    lvl, tri = tables
    has_init = init is not None
    rows = n_sub * seq_len
    n_chunks = seq_len // CHUNK

    def col(c):
        return pl.BlockSpec((rows, HG_DIM), lambda b, h: (row0 + b, c + h))

    state_spec = pl.BlockSpec((n_sub, None, 2, None, HG_DIM, HG_DIM), lambda b, h: (b, layer, 0, h, 0, 0))
    in_specs = [col(COL_HQ), col(COL_HF), col(COL_HB), col(COL_HI), col(COL_HG),
                pl.BlockSpec((None, 4, HG_DIM), lambda b, h: (layer, 0, h)),
                pl.BlockSpec((1, HG_DIM), lambda b, h: (0, 0)),
                pl.BlockSpec((2, CHUNK, CHUNK), lambda b, h: (0, 0, 0)),
                pl.BlockSpec((CHUNK, CHUNK), lambda b, h: (0, 0))]
    args = [p, p, p, p, p, lb_tab, gain.reshape(1, HG_DIM), tri, lvl]
    if has_init:
        in_specs.append(state_spec)
        args.append(init)
    in_specs.append(ANY_SPEC)
    args.append(mix)
    aliases = {len(args) - 1: 0}
    out_specs = [pl.BlockSpec((rows, HG_DIM), lambda b, h: (row0 + b, MIX_HG + h))]
    out_shape = [jax.ShapeDtypeStruct((N_TOK, D_MODEL), BF16)]
    if not has_init:
        in_specs.append(ANY_SPEC)
        args.append(sbuf)
        aliases[len(args) - 1] = 1
        out_specs.append(state_spec)
        out_shape.append(jax.ShapeDtypeStruct((n_seq, DEPTH, 2, HG_HEADS, HG_DIM, HG_DIM), F32))

    def row_buf(dtype):
        return pltpu.VMEM((rows, HG_DIM), dtype)

    def state_buf(n):
        return pltpu.VMEM((n, HG_DIM, HG_DIM), F32)

    return pl.pallas_call(
        functools.partial(_hgrn_kernel, seq_len=seq_len, n_sub=n_sub, has_init=has_init),
        grid=(n_seq // n_sub, HG_HEADS),
        in_specs=in_specs,
        out_specs=out_specs,
        out_shape=out_shape,
        scratch_shapes=[row_buf(F32), row_buf(BF16), row_buf(F32), row_buf(BF16),
                        state_buf(n_sub * n_chunks), state_buf(n_sub * n_chunks), state_buf(n_sub)],
        input_output_aliases=aliases,
        compiler_params=_params(("arbitrary", "arbitrary")),
        name="hgrn_lat" if has_init else "hgrn_ctx",
    )(*args)


def _router_kernel(h_ref, r_ref, o_ref):
    lg = jnp.dot(h_ref[...], r_ref[...], preferred_element_type=F32)
    lane = lax.broadcasted_iota(jnp.int32, lg.shape, 1)
    lg = jnp.where(lane < N_EXPERTS, lg, -jnp.inf)
    m1 = jnp.max(lg, axis=-1, keepdims=True)
    i1 = jnp.min(jnp.where(lg == m1, lane, HEAD_DIM), axis=-1, keepdims=True)
    lg2 = jnp.where(lane == i1, -jnp.inf, lg)
    m2 = jnp.max(lg2, axis=-1, keepdims=True)
    i2 = jnp.min(jnp.where(lg2 == m2, lane, HEAD_DIM), axis=-1, keepdims=True)
    e2 = jnp.exp(m2 - m1)
    den = 1.0 + e2
    o_ref[...] = jnp.where(lane == 0, i1.astype(F32),
                           jnp.where(lane == 1, i2.astype(F32),
                                     jnp.where(lane == 2, 1.0 / den,
                                               jnp.where(lane == 3, e2 / den, 0.0))))


def _route(h, router, slab0):
    tm = 512
    rp = jnp.pad(router, ((0, 0), (0, HEAD_DIM - N_EXPERTS))).astype(BF16)
    out = pl.pallas_call(
        _router_kernel,
        grid=(N_TOK // tm,),
        in_specs=[pl.BlockSpec((tm, D_MODEL), lambda i: (i, 0)),
                  pl.BlockSpec((D_MODEL, HEAD_DIM), lambda i: (0, 0))],
        out_specs=pl.BlockSpec((tm, HEAD_DIM), lambda i: (i, 0)),
        out_shape=jax.ShapeDtypeStruct((N_TOK, HEAD_DIM), F32),
        compiler_params=_params(("arbitrary",)),
        name="router",
    )(h, rp)
    top_i = out[:, 0:2].astype(jnp.int32)
    gates = out[:, 2:4]
    e_flat = top_i.reshape(-1)
    onehot = (e_flat[:, None] == jnp.arange(N_EXPERTS)[None, :]).astype(jnp.int32)
    csum = jnp.cumsum(onehot, axis=0)
    counts = csum[-1]
    padded = ((counts + MOE_TM - 1) // MOE_TM) * MOE_TM
    ends = jnp.cumsum(padded)
    starts = ends - padded
    pos = jnp.sum(onehot * (csum - 1 + starts[None, :]), axis=1)
    row_token = jnp.zeros((MOE_ROWS,), jnp.int32).at[pos].set(jnp.arange(2 * N_TOK, dtype=jnp.int32) // 2)
    tile_start = jnp.arange(MOE_TILES, dtype=jnp.int32) * MOE_TM
    tile_expert = jnp.minimum(jnp.sum(tile_start[:, None] >= ends[None, :], axis=1), N_EXPERTS - 1)
    tile_valid = (tile_start < ends[-1]).astype(jnp.int32)
    return gates, pos.reshape(N_TOK, 2), row_token, (slab0 + tile_expert).astype(jnp.int32), tile_valid


def _rows(table, idx):
    return table.at[idx].get(mode="promise_in_bounds")


def kernel(x_prompt, x_sample, cache_k, cache_v, state_hgrn, c, c_ctx, w_mod, b_mod, norm_gains,
           w_in, w_out, attn_sink, hg_lb_logits, hg_norm_gain, ffn_w1, ffn_w3, ffn_w2,
           moe_router, moe_w1, moe_w3, moe_w2):
    x = jnp.concatenate([x_prompt.reshape(N_PROMPT, D_MODEL), x_sample.reshape(N_SAMPLE, D_MODEL)], axis=0)

    cvec = jnp.concatenate([c_ctx[None], c, jnp.zeros((3, D_MODEL), F32)], axis=0)
    mod = _modulation(cvec, w_mod, b_mod).reshape(DEPTH, 8, 6, D_MODEL)
    mod = jnp.pad(mod, ((0, 0), (0, 0), (0, 2), (0, 0)))
    group_src = np.array([0] * (N_PROMPT // MOD_GROUP) + list(range(1, 1 + DEC_BATCH)))
    mod = mod[:, group_src]

    lb_cum = jnp.cumsum(jax.nn.softmax(hg_lb_logits.astype(F32), axis=0), axis=0)
    lb = lb_cum - lb_cum[0:1]
    lb_tab = jnp.stack([jnp.log(lb[:, 0]), jnp.log1p(-lb[:, 0]),
                        jnp.log(lb[:, 1]), jnp.log1p(-lb[:, 1])], axis=1)

    rope = _rope_tables()
    tables = _level_tables()
    kc = cache_k.transpose(1, 0, 3, 2, 4).astype(BF16)
    vc = cache_v.transpose(1, 0, 3, 2, 4).astype(BF16)
    ffn_w = [ffn_w1, ffn_w3, ffn_w2]
    moe_w = [w.reshape((-1,) + w.shape[2:]) for w in (moe_w1, moe_w3, moe_w2)]

    (h,) = _resnorm(x, [], None, None, 0, norm_gains[0, 0], mod[0], 1, 0)
    mix = jnp.zeros((N_TOK, D_MODEL), BF16)
    kbuf = jnp.zeros((BATCH, DEPTH, SEQ, ATT_KV_HEADS * HEAD_DIM), F32)
    vbuf = jnp.zeros((BATCH, DEPTH, SEQ, ATT_KV_HEADS * HEAD_DIM), F32)
    sbuf = jnp.zeros((BATCH, DEPTH, 2, HG_HEADS, HG_DIM, HG_DIM), F32)
    for l in range(DEPTH):
        i = l // 2
        moe = l % 2 == 1
        p = _dense(h, [w_in], l, F32, "proj_in")
        mix, kbuf, vbuf = _attn_ctx(p, attn_sink[l], l, mix, kbuf, vbuf)
        mix = _attn_lat(p, attn_sink[l], kc, vc, l, rope, mix)
        mix, sbuf = _hgrn(p, lb_tab, hg_norm_gain[l], tables, SEQ, BATCH, 4, 0, l, mix, sbuf=sbuf)
        (mix,) = _hgrn(p, lb_tab, hg_norm_gain[l], tables, DEC_SEQ, DEC_BATCH, 1, N_PROMPT // DEC_SEQ,
                       l, mix, init=state_hgrn)
        m = _dense(mix, [w_out], l, F32, "proj_out")
        outs = _resnorm(x, [m], norm_gains[l, 1], mod[l], 2, norm_gains[l, 2], mod[l], 4, 3, wide_h=moe)
        x, h = outs[0], outs[1]
        if not moe:
            act = _dense(h, ffn_w[:2], i, BF16, "ffn_up")
            f = [_dense(act, ffn_w[2:], i, F32, "ffn_down")]
            gates = None
        else:
            gates, pos, row_token, tile_slab, tile_valid = _route(h, moe_router[i], i * N_EXPERTS)
            xs = _rows(outs[2], row_token)
            act = _gmm(xs, moe_w[:2], tile_slab, tile_valid, BF16, "moe_up")
            ys = _gmm(act, moe_w[2:], tile_slab, tile_valid, F32, "moe_down")
            f = [_rows(ys, pos[:, 0]), _rows(ys, pos[:, 1])]
        if l + 1 < DEPTH:
            x, h = _resnorm(x, f, norm_gains[l, 3], mod[l], 5, norm_gains[l + 1, 0], mod[l + 1], 1, 0,
                            gates=gates)
        else:
            (x,) = _resnorm(x, f, norm_gains[l, 3], mod[l], 5, None, None, 0, 0, gates=gates)
    kv_shape = (BATCH, DEPTH, SEQ, ATT_KV_HEADS, HEAD_DIM)
    return (x[:N_PROMPT].reshape(BATCH, SEQ, D_MODEL), x[N_PROMPT:].reshape(DEC_BATCH, DEC_SEQ, D_MODEL),
            kbuf.reshape(kv_shape), vbuf.reshape(kv_shape), sbuf)
```

```python
import functools

import numpy as np
import jax
import jax.numpy as jnp
from jax import lax
from jax.experimental import pallas as pl
from jax.experimental.pallas import tpu as pltpu

F32 = jnp.float32
BF16 = jnp.bfloat16

D_MODEL = 2048
BATCH = 16
SEQ = 256
DEPTH = 4
DEC_BATCH = 4
DEC_SEQ = 1024
PAST_LEN = 256
GRID_W = 64
HEAD_DIM = 128
ATT_HEADS = 8
ATT_KV_HEADS = 2
Q_PER_KV = 4
ATT_WIDTH = 1024
WINDOW = 128
ATT_BLOCK = 128
ROPE_BASE = 10000.0
HG_WIDTH = 1024
HG_HEADS = 8
HG_DIM = 128
N_EXPERTS = 8
D_FF = 5632
D_FF_EXPERT = 2816
NORM_EPS = 1e-6
IN_COLS = 6656

N_PROMPT = BATCH * SEQ
N_SAMPLE = DEC_BATCH * DEC_SEQ
N_TOK = N_PROMPT + N_SAMPLE
MOD_GROUP = 1024
N_GROUPS = N_TOK // MOD_GROUP

COL_K = 8
COL_V = 10
COL_HQ = 12
COL_HF = 20
COL_HB = 28
COL_HI = 36
COL_HG = 44
MIX_HG = ATT_WIDTH // HG_DIM

CHUNK = 128
N_LEVELS = 7
LOG2E = 1.4426950408889634

VMEM_LIMIT = 60 * 1024 * 1024

TILES = {
    "proj_in": (1024, 1664),
    "proj_out": (512, 2048),
    "ffn_up": (512, 1408),
    "ffn_down": (512, 1024),
    "moe_up": (256, 1408),
    "moe_down": (256, 2048),
}

MOE_TM = TILES["moe_up"][0]
MOE_ROWS = 2 * N_TOK + N_EXPERTS * MOE_TM
MOE_TILES = MOE_ROWS // MOE_TM

ANY_SPEC = pl.BlockSpec(memory_space=pl.ANY)


def _params(sem):
    return pltpu.CompilerParams(dimension_semantics=sem, vmem_limit_bytes=VMEM_LIMIT)


def _silu(x):
    return x / (1.0 + jnp.exp(-x))


def _dot_nt(a, b):
    return lax.dot_general(a, b, (((1,), (1,)), ((), ())), preferred_element_type=F32)


def _dot_tn(a, b):
    return lax.dot_general(a, b, (((0,), (0,)), ((), ())), preferred_element_type=F32)


def _mod_kernel(c_ref, w_ref, b_ref, o_ref):
    s = _silu(c_ref[...]).astype(BF16)
    o_ref[...] = jnp.dot(s, w_ref[...].astype(BF16), preferred_element_type=F32) + b_ref[...]


def _modulation(cvec, w_mod, b_mod):
    tn = 1024
    n = 6 * D_MODEL
    return pl.pallas_call(
        _mod_kernel,
        grid=(DEPTH, n // tn),
        in_specs=[
            pl.BlockSpec((8, D_MODEL), lambda l, j: (0, 0)),
            pl.BlockSpec((None, D_MODEL, tn), lambda l, j: (l, 0, j)),
            pl.BlockSpec((None, 1, tn), lambda l, j: (l, 0, j)),
        ],
        out_specs=pl.BlockSpec((None, 8, tn), lambda l, j: (l, 0, j)),
        out_shape=jax.ShapeDtypeStruct((DEPTH, 8, n), F32),
        compiler_params=_params(("arbitrary", "arbitrary")),
        name="modulation",
    )(cvec, w_mod, b_mod.reshape(DEPTH, 1, n))


def _rms(x, gain):
    return x * lax.rsqrt(jnp.mean(x * x, axis=-1, keepdims=True) + NORM_EPS) * gain


def _top2_gates(lg):
    lane = lax.broadcasted_iota(jnp.int32, lg.shape, 1)
    lg = jnp.where(lane < N_EXPERTS, lg, -jnp.inf)
    m1 = jnp.max(lg, axis=-1, keepdims=True)
    i1 = jnp.min(jnp.where(lg == m1, lane, HEAD_DIM), axis=-1, keepdims=True)
    lg2 = jnp.where(lane == i1, -jnp.inf, lg)
    m2 = jnp.max(lg2, axis=-1, keepdims=True)
    i2 = jnp.min(jnp.where(lg2 == m2, lane, HEAD_DIM), axis=-1, keepdims=True)
    e2 = jnp.exp(m2 - m1)
    den = 1.0 + e2
    return jnp.where(lane == 0, i1.astype(F32),
                     jnp.where(lane == 1, i2.astype(F32),
                               jnp.where(lane == 2, 1.0 / den,
                                         jnp.where(lane == 3, e2 / den, 0.0))))


def _resnorm_kernel(*refs, n_m, gate_row, has_h, sc_row, sh_row, routed):
    refs = list(refs)
    x_ref = refs.pop(0)
    x = x_ref[...]
    if n_m == 1:
        m = refs.pop(0)[...]
    elif n_m == 2:
        pair = refs.pop(0)
        gt = refs.pop(0)[...]
        m = gt[:, 0:1] * pair[:, :D_MODEL] + gt[:, 1:2] * pair[:, D_MODEL:]
    if n_m:
        ga = refs.pop(0)[...]
        modg = refs.pop(0)
        x = x + modg[gate_row:gate_row + 1, :] * _rms(m, ga)
    if has_h:
        gb = refs.pop(0)[...]
        modh = refs.pop(0)
        h = _rms(x, gb) * (1.0 + modh[sc_row:sc_row + 1, :]) + modh[sh_row:sh_row + 1, :]
    if routed:
        router = refs.pop(0)[...]
    if n_m:
        refs.pop(0)[...] = x
    if routed:
        refs.pop(0)[...] = h
        refs.pop(0)[...] = _top2_gates(jnp.dot(h.astype(BF16), router, preferred_element_type=F32))
    elif has_h:
        refs.pop(0)[...] = h.astype(BF16)


def _resnorm(x, ms, gain_m, mod_gate, gate_row, gain_h, mod_h, sc_row, sh_row, gates=None, router=None):
    tm = 256
    per = MOD_GROUP // tm
    n_m = len(ms) if gates is None else 2
    has_h = gain_h is not None
    routed = router is not None
    row = pl.BlockSpec((tm, D_MODEL), lambda i: (i, 0))
    vec = pl.BlockSpec((1, D_MODEL), lambda i: (0, 0))
    mod = pl.BlockSpec((None, 8, D_MODEL), lambda i: (i // per, 0, 0))
    args, specs = [x], [row]
    if n_m == 1:
        args.append(ms[0])
        specs.append(row)
    elif n_m == 2:
        args += [ms[0], gates]
        specs += [pl.BlockSpec((tm, 2 * D_MODEL), lambda i: (i, 0)), pl.BlockSpec((tm, 2), lambda i: (i, 0))]
    if n_m:
        args += [gain_m.reshape(1, D_MODEL), mod_gate]
        specs += [vec, mod]
    if has_h:
        args += [gain_h.reshape(1, D_MODEL), mod_h]
        specs += [vec, mod]
    if routed:
        args.append(router)
        specs.append(pl.BlockSpec((D_MODEL, HEAD_DIM), lambda i: (0, 0)))
    out_shape, out_specs = [], []
    if n_m:
        out_shape.append(jax.ShapeDtypeStruct((N_TOK, D_MODEL), F32))
        out_specs.append(row)
    if routed:
        out_shape += [jax.ShapeDtypeStruct((N_TOK, D_MODEL), F32), jax.ShapeDtypeStruct((N_TOK, HEAD_DIM), F32)]
        out_specs += [row, pl.BlockSpec((tm, HEAD_DIM), lambda i: (i, 0))]
    elif has_h:
        out_shape.append(jax.ShapeDtypeStruct((N_TOK, D_MODEL), BF16))
        out_specs.append(row)
    outs = pl.pallas_call(
        functools.partial(_resnorm_kernel, n_m=n_m, gate_row=gate_row, has_h=has_h,
                          sc_row=sc_row, sh_row=sh_row, routed=routed),
        grid=(N_TOK // tm,),
        in_specs=specs,
        out_specs=out_specs,
        out_shape=out_shape,
        compiler_params=_params(("arbitrary",)),
        name="resnorm",
    )(*args)
    return outs


CAST_ROWS = 256


def _gmm_kernel(load_ref, slab_ref, col_ref, more_ref, nslab_ref, ncol_ref, valid_ref, x_ref, *rest,
                nw, tn, k_rows):
    w_hbm = rest[:nw]
    o_ref = rest[nw]
    stage = rest[nw + 1:2 * nw + 1]
    wb = rest[2 * nw + 1:3 * nw + 1]
    sem = rest[3 * nw + 1]
    t = pl.program_id(0) * pl.num_programs(1) + pl.program_id(1)

    def copies(slab, col):
        cols = pl.ds(pl.multiple_of(col * tn, 128), tn)
        return [pltpu.make_async_copy(w_hbm[k].at[slab, :, cols], stage[k], sem.at[k]) for k in range(nw)]

    @pl.when(t == 0)
    def _():
        for cp in copies(slab_ref[0], col_ref[0]):
            cp.start()

    @pl.when(load_ref[t] != 0)
    def _():
        for cp in copies(slab_ref[t], col_ref[t]):
            cp.wait()

        def cast(r, carry):
            rows = pl.ds(pl.multiple_of(r * CAST_ROWS, CAST_ROWS), CAST_ROWS)
            for k in range(nw):
                wb[k][rows, :] = stage[k][rows, :].astype(BF16)
            return carry

        lax.fori_loop(0, k_rows // CAST_ROWS, cast, 0)

        @pl.when(more_ref[t] != 0)
        def _():
            for cp in copies(nslab_ref[t], ncol_ref[t]):
                cp.start()

    valid = valid_ref[pl.program_id(1)] != 0

    @pl.when(valid)
    def _():
        x = x_ref[...].astype(BF16)
        a = jnp.dot(x, wb[0][...], preferred_element_type=F32)
        if nw == 2:
            a = _silu(a) * jnp.dot(x, wb[1][...], preferred_element_type=F32)
        o_ref[...] = a.astype(o_ref.dtype)

    @pl.when(jnp.logical_not(valid))
    def _():
        o_ref[...] = jnp.zeros_like(o_ref)


def _weight_schedule(tile_slab, tile_valid, n_col):
    n_row = tile_slab.shape[0]
    idx = jnp.arange(n_row, dtype=jnp.int32)
    keep = lax.cummax(jnp.where(tile_valid != 0, idx, 0), axis=0)
    slab = jnp.tile(tile_slab[keep], n_col)
    col = jnp.repeat(jnp.arange(n_col, dtype=jnp.int32), n_row)
    steps = n_row * n_col
    key = slab * n_col + col
    load = jnp.concatenate([jnp.ones((1,), bool), key[1:] != key[:-1]])
    load_at = jnp.where(load, jnp.arange(steps, dtype=jnp.int32), steps)
    first_from = lax.cummin(load_at, axis=0, reverse=True)
    nxt = jnp.concatenate([first_from[1:], jnp.full((1,), steps, jnp.int32)])
    more = nxt < steps
    nxt = jnp.minimum(nxt, steps - 1)
    return (load.astype(jnp.int32), slab.astype(jnp.int32), col, more.astype(jnp.int32),
            slab[nxt].astype(jnp.int32), col[nxt])


def _gmm(x, ws, tile_slab, tile_valid, out_dtype, name):
    tm, tn = TILES[name]
    m, k = x.shape
    n = ws[0].shape[-1]
    nw = len(ws)
    sched = _weight_schedule(tile_slab, tile_valid, n // tn)
    return pl.pallas_call(
        functools.partial(_gmm_kernel, nw=nw, tn=tn, k_rows=k),
        grid_spec=pltpu.PrefetchScalarGridSpec(
            num_scalar_prefetch=7,
            grid=(n // tn, m // tm),
            in_specs=[pl.BlockSpec((tm, k), lambda j, i, *_: (i, 0))] + [ANY_SPEC] * nw,
            out_specs=pl.BlockSpec((tm, tn), lambda j, i, *_: (i, j)),
            scratch_shapes=([pltpu.VMEM((k, tn), F32)] * nw + [pltpu.VMEM((k, tn), BF16)] * nw
                            + [pltpu.SemaphoreType.DMA((nw,))]),
        ),
        out_shape=jax.ShapeDtypeStruct((m, n), out_dtype),
        compiler_params=_params(("arbitrary", "arbitrary")),
        name=name,
    )(*sched, tile_valid, x, *ws)


def _dense(x, ws, slab, out_dtype, name):
    tiles = x.shape[0] // TILES[name][0]
    return _gmm(x, ws, jnp.full((tiles,), slab, jnp.int32), jnp.ones((tiles,), jnp.int32),
                out_dtype, name)


def _sink_rows(sink_ref, g, rows, per):
    r = lax.broadcasted_iota(jnp.int32, (rows, 1), 0)
    sk = jnp.full((rows, 1), sink_ref[g * Q_PER_KV + Q_PER_KV - 1], F32)
    for h in range(Q_PER_KV - 2, -1, -1):
        sk = jnp.where(r < (h + 1) * per, sink_ref[g * Q_PER_KV + h], sk)
    return sk


def _attn_ctx_kernel(*refs):
    sink_ref, q_ref, k_ref, v_ref = refs[:4]
    o_ref, ko_ref, vo_ref = refs[-3:]
    g = pl.program_id(1)
    scale = HEAD_DIM ** -0.5
    k = k_ref[...]
    v = v_ref[...]
    for head in range(ATT_KV_HEADS):
        @pl.when(g == head)
        def _(head=head):
            ko_ref[:, head, :] = k
            vo_ref[:, head, :] = v
    q = q_ref[...]
    qs = jnp.concatenate([q[:, r * HEAD_DIM:(r + 1) * HEAD_DIM] for r in range(Q_PER_KV)], axis=0) * scale
    s = _dot_nt(qs.astype(BF16), k.astype(BF16))
    sk = _sink_rows(sink_ref, g, Q_PER_KV * SEQ, SEQ)
    m = jnp.maximum(jnp.max(s, axis=-1, keepdims=True), sk)
    e = jnp.exp(s - m)
    den = jnp.sum(e, axis=-1, keepdims=True) + jnp.exp(sk - m)
    o = jnp.dot(e.astype(BF16), v.astype(BF16), preferred_element_type=F32) / den
    o_ref[...] = jnp.concatenate([o[r * SEQ:(r + 1) * SEQ] for r in range(Q_PER_KV)],
                                 axis=1).astype(o_ref.dtype)


def _attn_ctx(p, sink, layer, mix, kbuf, vbuf):
    kv_spec = pl.BlockSpec((None, None, SEQ, ATT_KV_HEADS, HEAD_DIM), lambda b, g: (b, layer, 0, 0, 0))
    kv_shape = jax.ShapeDtypeStruct((BATCH, DEPTH, SEQ, ATT_KV_HEADS, HEAD_DIM), F32)
    in_specs = [
        pl.BlockSpec(memory_space=pltpu.SMEM),
        pl.BlockSpec((SEQ, Q_PER_KV * HEAD_DIM), lambda b, g: (b, g)),
        pl.BlockSpec((SEQ, HEAD_DIM), lambda b, g: (b, COL_K + g)),
        pl.BlockSpec((SEQ, HEAD_DIM), lambda b, g: (b, COL_V + g)),
    ]
    in_specs += [ANY_SPEC] * 3
    args = [sink, p, p, p, mix, kbuf, vbuf]
    aliases = {4: 0, 5: 1, 6: 2}
    return pl.pallas_call(
        _attn_ctx_kernel,
        grid=(BATCH, ATT_KV_HEADS),
        in_specs=in_specs,
        out_specs=[pl.BlockSpec((SEQ, Q_PER_KV * HEAD_DIM), lambda b, g: (b, g)), kv_spec, kv_spec],
        out_shape=[jax.ShapeDtypeStruct((N_TOK, D_MODEL), BF16), kv_shape, kv_shape],
        input_output_aliases=aliases,
        compiler_params=_params(("arbitrary", "arbitrary")),
        name="attn_ctx",
    )(*args)


def _rope(x, c, sa, sb):
    return x * c + pltpu.roll(x, 96, 1) * sa + pltpu.roll(x, 32, 1) * sb


def _attn_lat_kernel(sink_ref, q_ref, k_ref, v_ref, kc_ref, vc_ref, cq_ref, sqa_ref, sqb_ref,
                     ck_ref, ska_ref, skb_ref, bias_ref, mix_ref, o_ref, kr_s, vb_s):
    del mix_ref
    g = pl.program_id(1)
    qb = pl.program_id(2)
    scale = HEAD_DIM ** -0.5
    band = 3 * ATT_BLOCK

    @pl.when(qb == 0)
    def _():
        kr_s[...] = _rope(k_ref[...], ck_ref[...], ska_ref[...], skb_ref[...]).astype(BF16)
        vb_s[...] = v_ref[...].astype(BF16)

    q = q_ref[...]
    cq, sqa, sqb = cq_ref[...], sqa_ref[...], sqb_ref[...]
    qs = jnp.concatenate(
        [_rope(q[:, r * HEAD_DIM:(r + 1) * HEAD_DIM], cq, sqa, sqb) * scale for r in range(Q_PER_KV)],
        axis=0).astype(BF16)
    rows = Q_PER_KV * ATT_BLOCK
    start = pl.multiple_of(jnp.clip(qb - 1, 0, DEC_SEQ // ATT_BLOCK - 3) * ATT_BLOCK, ATT_BLOCK)
    kb = kr_s[pl.ds(start, band), :]
    vb = vb_s[pl.ds(start, band), :]
    s_loc = _dot_nt(qs, kb) + bias_ref[...]
    s_ctx = _dot_nt(qs, kc_ref[...])
    sk = _sink_rows(sink_ref, g, rows, ATT_BLOCK)
    m = jnp.maximum(jnp.maximum(jnp.max(s_loc, axis=-1, keepdims=True),
                                jnp.max(s_ctx, axis=-1, keepdims=True)), sk)
    e_loc = jnp.exp(s_loc - m)
    e_ctx = jnp.exp(s_ctx - m)
    den = (jnp.sum(e_loc, axis=-1, keepdims=True) + jnp.sum(e_ctx, axis=-1, keepdims=True)
           + jnp.exp(sk - m))
    o = (jnp.dot(e_loc.astype(BF16), vb, preferred_element_type=F32)
         + jnp.dot(e_ctx.astype(BF16), vc_ref[...], preferred_element_type=F32)) / den
    o_ref[...] = jnp.concatenate([o[r * ATT_BLOCK:(r + 1) * ATT_BLOCK] for r in range(Q_PER_KV)],
                                 axis=1).astype(o_ref.dtype)


def _attn_lat(p, sink, kc, vc, layer, rope, mix):
    nqb = DEC_SEQ // ATT_BLOCK
    row0 = N_PROMPT // DEC_SEQ
    qrow0 = N_PROMPT // ATT_BLOCK
    tq = pl.BlockSpec((ATT_BLOCK, HEAD_DIM), lambda b, g, qb: (qb, 0))
    tk = pl.BlockSpec((DEC_SEQ, HEAD_DIM), lambda b, g, qb: (0, 0))
    ctx = pl.BlockSpec((None, None, None, PAST_LEN, HEAD_DIM), lambda b, g, qb: (layer, b, g, 0, 0))
    cos, sin_a, sin_b = rope
    bias = pl.BlockSpec((None, Q_PER_KV * ATT_BLOCK, 3 * ATT_BLOCK),
                        lambda b, g, qb: (jnp.where(qb == 0, 0, jnp.where(qb == nqb - 1, 2, 1)), 0, 0))
    return pl.pallas_call(
        _attn_lat_kernel,
        grid=(DEC_BATCH, ATT_KV_HEADS, nqb),
        in_specs=[
            pl.BlockSpec(memory_space=pltpu.SMEM),
            pl.BlockSpec((ATT_BLOCK, Q_PER_KV * HEAD_DIM), lambda b, g, qb: (qrow0 + b * nqb + qb, g)),
            pl.BlockSpec((DEC_SEQ, HEAD_DIM), lambda b, g, qb: (row0 + b, COL_K + g)),
            pl.BlockSpec((DEC_SEQ, HEAD_DIM), lambda b, g, qb: (row0 + b, COL_V + g)),
            ctx, ctx, tq, tq, tq, tk, tk, tk, bias, ANY_SPEC,
        ],
        out_specs=pl.BlockSpec((ATT_BLOCK, Q_PER_KV * HEAD_DIM),
                               lambda b, g, qb: (qrow0 + b * nqb + qb, g)),
        out_shape=jax.ShapeDtypeStruct((N_TOK, D_MODEL), BF16),
        scratch_shapes=[pltpu.VMEM((DEC_SEQ, HEAD_DIM), BF16), pltpu.VMEM((DEC_SEQ, HEAD_DIM), BF16)],
        input_output_aliases={13: 0},
        compiler_params=_params(("arbitrary", "arbitrary", "arbitrary")),
        name="attn_lat",
    )(sink, p, p, p, kc, vc, cos, sin_a, sin_b, cos, sin_a, sin_b, _band_bias(), mix)


def _band_bias():
    row = np.arange(Q_PER_KV * ATT_BLOCK)[:, None] % ATT_BLOCK
    lane = np.arange(3 * ATT_BLOCK)[None, :]
    cases = [lane - k * ATT_BLOCK - row for k in range(3)]
    return jnp.asarray(np.stack([np.where(np.abs(d) <= WINDOW, 0.0, -np.inf) for d in cases]).astype(np.float32))


def _rope_tables():
    rows = DEC_SEQ // GRID_W
    row = np.repeat(np.arange(rows), GRID_W).astype(np.float32)
    col = np.tile(np.arange(GRID_W), rows).astype(np.float32)
    axis_dim = HEAD_DIM // 2
    inv = (ROPE_BASE ** (-np.arange(0, axis_dim, 2, dtype=np.float32) / axis_dim)).astype(np.float32)
    lane = np.arange(HEAD_DIM)
    pos = np.where((lane // axis_dim)[None, :] == 0, row[:, None], col[:, None])
    ang = pos * inv[lane % (axis_dim // 2)][None, :]
    first = ((lane % axis_dim) < axis_dim // 2)[None, :]
    cos = np.cos(ang).astype(np.float32)
    sin = np.sin(ang).astype(np.float32)
    sin_a = np.where(first, -sin, 0.0).astype(np.float32)
    sin_b = np.where(first, 0.0, sin).astype(np.float32)
    return jnp.asarray(cos), jnp.asarray(sin_a), jnp.asarray(sin_b)


def _level_tables():
    t = np.arange(CHUNK)
    x = t[:, None] ^ t[None, :]
    lvl = np.where(x > 0, np.floor(np.log2(np.maximum(x, 1))), -1.0).astype(np.float32)
    tri = (t[:, None] >= t[None, :]).astype(np.float32)
    return jnp.asarray(lvl), jnp.asarray(np.stack([tri, tri.T]), dtype=BF16)


def _forget_gate(z, log_lb, log1m_lb):
    t = jnp.log(1.0 + jnp.exp(-jnp.abs(z)))
    c = log1m_lb + jnp.minimum(z, 0.0) - t
    log_f = jnp.maximum(log_lb, c) + jnp.log(1.0 + jnp.exp(-jnp.abs(log_lb - c)))
    return log_f, jnp.exp(log1m_lb - jnp.maximum(z, 0.0) - t)


def _running_log2_decay(g, tri):
    g1 = g.astype(BF16)
    g2 = (g - g1.astype(F32)).astype(BF16)
    dot = functools.partial(jnp.dot, preferred_element_type=F32)
    return (dot(tri, g1) + dot(tri, g2)) * LOG2E


def _intra_chunk_weights(qs, kfs, kbs, bfs, bbs, lvl):
    n = len(qs)
    acc = []
    for c in range(n):
        gram = _dot_nt(qs[c].astype(BF16), (kfs[c] + kbs[c]).astype(BF16))
        acc.append(jnp.where(lvl == -1.0, gram, 0.0))
    shape8 = (CHUNK // 8, 8, HG_DIM)
    sub = lax.broadcasted_iota(jnp.int32, shape8, 1)
    for j in range(N_LEVELS):
        half = 1 << j
        for c in range(n):
            q, kf, kb, bf, bb = qs[c], kfs[c], kbs[c], bfs[c], bbs[c]
            if j < 3:
                q, kf, kb, bf, bb = (a.reshape(shape8) for a in (q, kf, kb, bf, bb))
                upper = (sub & half) != 0
                if j == 0:
                    lhs = q * jnp.where(upper, 1.0 - kf, 1.0 - kb)
                    rhs = jnp.where(upper, kb, kf)
                else:
                    if j == 1:
                        ref_f = jnp.where(sub < 4, bf[:, 1:2, :], bf[:, 5:6, :])
                        ref_b = jnp.where(sub < 4, bb[:, 2:3, :], bb[:, 6:7, :])
                    else:
                        ref_f, ref_b = bf[:, 3:4, :], bb[:, 4:5, :]
                    lhs = q * jnp.exp2(jnp.where(upper, bf - ref_f, bb - ref_b))
                    rhs = jnp.where(upper, kb, kf) * jnp.exp2(jnp.where(upper, ref_b - bb, ref_f - bf))
            else:
                shape = (CHUNK // (2 * half), 2 * half, HG_DIM)
                q, kf, kb, bf, bb = (a.reshape(shape) for a in (q, kf, kb, bf, bb))
                ref_f, ref_b = bf[:, half - 1:half, :], bb[:, half:half + 1, :]
                lo, up = slice(0, half), slice(half, 2 * half)
                lhs = q * jnp.exp2(jnp.concatenate([bb[:, lo] - ref_b, bf[:, up] - ref_f], axis=1))
                rhs = (jnp.concatenate([kf[:, lo], kb[:, up]], axis=1)
                       * jnp.exp2(jnp.concatenate([ref_f - bf[:, lo], ref_b - bb[:, up]], axis=1)))
            gram = _dot_nt(lhs.reshape(CHUNK, HG_DIM).astype(BF16), rhs.reshape(CHUNK, HG_DIM).astype(BF16))
            acc[c] = jnp.where(lvl == float(j), gram, acc[c])
    return acc


HGRN_INTERLEAVE = 8


def _hgrn_kernel(*refs, seq_len, n_sub, has_init):
    refs = list(refs)
    hq_ref, hf_ref, hb_ref, hi_ref, hg_ref, lb_ref, gain_ref, tri_ref, lvl_ref = refs[:9]
    refs = refs[9:]
    init_ref = refs.pop(0) if has_init else None
    refs.pop(0)
    if not has_init:
        refs.pop(0)
    out_ref = refs.pop(0)
    fin_ref = None if has_init else refs.pop(0)
    q_s, vb_s, op_s, qdb_s, ub_s, dcb_s, sf_s = refs
    n_chunks = seq_len // CHUNK
    per = min(HGRN_INTERLEAVE // n_sub, n_chunks)
    slots = [(s, c) for s in range(n_sub) for c in range(per)]
    dot = functools.partial(jnp.dot, preferred_element_type=F32)

    q_s[...] = _silu(hq_ref[...])
    vb_s[...] = hi_ref[...].astype(BF16)

    def chunk_slice(s, ci):
        return pl.ds(pl.multiple_of(s * seq_len + ci * CHUNK, CHUNK), CHUNK)

    def fwd_step(i, carry):
        sls = [chunk_slice(s, per * i + c) for s, c in slots]
        lvl = lvl_ref[...]
        gf = [_forget_gate(hf_ref[sl, :], lb_ref[0:1, :], lb_ref[1:2, :]) for sl in sls]
        gb = [_forget_gate(hb_ref[sl, :], lb_ref[2:3, :], lb_ref[3:4, :]) for sl in sls]
        bfs = [_running_log2_decay(g, tri_ref[0]) for g, _ in gf]
        bbs = [_running_log2_decay(g, tri_ref[1]) for g, _ in gb]
        qs = [q_s[sl, :] for sl in sls]
        kfs = [k for _, k in gf]
        kbs = [k for _, k in gb]
        acc = _intra_chunk_weights(qs, kfs, kbs, bfs, bbs, lvl)
        for n, (s, c) in enumerate(slots):
            sl, q, bf, bb = sls[n], qs[n], bfs[n], bbs[n]
            ci = s * n_chunks + per * i + c
            vb = vb_s[sl, :]
            s_f = sf_s[s]
            qd = (q * jnp.exp2(bf)).astype(BF16)
            op_s[sl, :] = dot(jnp.concatenate([qd, acc[n].astype(BF16)], axis=1),
                              jnp.concatenate([s_f.astype(BF16), vb], axis=0))
            tot_f = bf[CHUNK - 1:CHUNK]
            kd = (kfs[n] * jnp.exp2(tot_f - bf)).astype(BF16)
            dcol = jnp.transpose(jnp.broadcast_to(jnp.exp2(tot_f), (HG_DIM, HG_DIM)))
            sf_s[s] = s_f * dcol + _dot_tn(kd, vb)
            tot_b = bb[0:1]
            qdb_s[sl, :] = (q * jnp.exp2(bb)).astype(BF16)
            kd = (kbs[n] * jnp.exp2(tot_b - bb)).astype(BF16)
            ub_s[ci] = _dot_tn(kd, vb)
            dcb_s[ci] = jnp.transpose(jnp.broadcast_to(jnp.exp2(tot_b), (HG_DIM, HG_DIM)))
        return carry

    zero = jnp.zeros((HG_DIM, HG_DIM), F32)
    for s in range(n_sub):
        sf_s[s] = init_ref[s, 0] if has_init else zero
    lax.fori_loop(0, n_chunks // per, fwd_step, 0)

    def bwd_step(i, states):
        states = list(states)
        outs = []
        for s, c in slots:
            ci = n_chunks - 1 - (per * i + c)
            sl = chunk_slice(s, ci)
            outs.append((sl, op_s[sl, :] + dot(qdb_s[sl, :], states[s].astype(BF16))))
            states[s] = states[s] * dcb_s[s * n_chunks + ci] + ub_s[s * n_chunks + ci]
        for sl, o in outs:
            o = o * lax.rsqrt(jnp.mean(o * o, axis=-1, keepdims=True) + NORM_EPS) * gain_ref[...]
            out_ref[sl, :] = (o * _silu(hg_ref[sl, :])).astype(out_ref.dtype)
        return tuple(states)

    s_b = lax.fori_loop(0, n_chunks // per, bwd_step,
                        tuple(init_ref[s, 1] if has_init else zero for s in range(n_sub)))
    if not has_init:
        for s in range(n_sub):
            fin_ref[s, 0] = sf_s[s]
            fin_ref[s, 1] = s_b[s]


def _hgrn(p, lb_tab, gain, tables, seq_len, n_seq, n_sub, row0, layer, mix, init=None, sbuf=None):
    lvl, tri = tables
    has_init = init is not None
    rows = n_sub * seq_len
    n_chunks = seq_len // CHUNK

    def col(c):
        return pl.BlockSpec((rows, HG_DIM), lambda b, h: (row0 + b, c + h))

    state_spec = pl.BlockSpec((n_sub, None, 2, None, HG_DIM, HG_DIM), lambda b, h: (b, layer, 0, h, 0, 0))
    in_specs = [col(COL_HQ), col(COL_HF), col(COL_HB), col(COL_HI), col(COL_HG),
                pl.BlockSpec((None, 4, HG_DIM), lambda b, h: (layer, 0, h)),
                pl.BlockSpec((1, HG_DIM), lambda b, h: (0, 0)),
                pl.BlockSpec((2, CHUNK, CHUNK), lambda b, h: (0, 0, 0)),
                pl.BlockSpec((CHUNK, CHUNK), lambda b, h: (0, 0))]
    args = [p, p, p, p, p, lb_tab, gain.reshape(1, HG_DIM), tri, lvl]
    if has_init:
        in_specs.append(state_spec)
        args.append(init)
    in_specs.append(ANY_SPEC)
    args.append(mix)
    aliases = {len(args) - 1: 0}
    out_specs = [pl.BlockSpec((rows, HG_DIM), lambda b, h: (row0 + b, MIX_HG + h))]
    out_shape = [jax.ShapeDtypeStruct((N_TOK, D_MODEL), BF16)]
    if not has_init:
        in_specs.append(ANY_SPEC)
        args.append(sbuf)
        aliases[len(args) - 1] = 1
        out_specs.append(state_spec)
        out_shape.append(jax.ShapeDtypeStruct((n_seq, DEPTH, 2, HG_HEADS, HG_DIM, HG_DIM), F32))

    def row_buf(dtype):
        return pltpu.VMEM((rows, HG_DIM), dtype)

    def state_buf(n):
        return pltpu.VMEM((n, HG_DIM, HG_DIM), F32)

    return pl.pallas_call(
        functools.partial(_hgrn_kernel, seq_len=seq_len, n_sub=n_sub, has_init=has_init),
        grid=(n_seq // n_sub, HG_HEADS),
        in_specs=in_specs,
        out_specs=out_specs,
        out_shape=out_shape,
        scratch_shapes=[row_buf(F32), row_buf(BF16), row_buf(F32), row_buf(BF16),
                        state_buf(n_sub * n_chunks), state_buf(n_sub * n_chunks), state_buf(n_sub)],
        input_output_aliases=aliases,
        compiler_params=_params(("arbitrary", "arbitrary")),
        name="hgrn_lat" if has_init else "hgrn_ctx",
    )(*args)


def _route(out, slab0):
    top_i = out[:, 0:2].astype(jnp.int32)
    gates = out[:, 2:4]
    e_flat = top_i.reshape(-1)
    onehot = (e_flat[:, None] == jnp.arange(N_EXPERTS)[None, :]).astype(jnp.int32)
    csum = jnp.cumsum(onehot, axis=0)
    counts = csum[-1]
    padded = ((counts + MOE_TM - 1) // MOE_TM) * MOE_TM
    ends = jnp.cumsum(padded)
    starts = ends - padded
    pos = jnp.sum(onehot * (csum - 1 + starts[None, :]), axis=1)
    row_token = jnp.zeros((MOE_ROWS,), jnp.int32).at[pos].set(jnp.arange(2 * N_TOK, dtype=jnp.int32) // 2)
    tile_start = jnp.arange(MOE_TILES, dtype=jnp.int32) * MOE_TM
    tile_expert = jnp.minimum(jnp.sum(tile_start[:, None] >= ends[None, :], axis=1), N_EXPERTS - 1)
    tile_valid = (tile_start < ends[-1]).astype(jnp.int32)
    return gates, pos.reshape(N_TOK, 2), row_token, (slab0 + tile_expert).astype(jnp.int32), tile_valid


def _rows(table, idx):
    return table.at[idx].get(mode="promise_in_bounds")


def kernel(x_prompt, x_sample, cache_k, cache_v, state_hgrn, c, c_ctx, w_mod, b_mod, norm_gains,
           w_in, w_out, attn_sink, hg_lb_logits, hg_norm_gain, ffn_w1, ffn_w3, ffn_w2,
           moe_router, moe_w1, moe_w3, moe_w2):
    x = jnp.concatenate([x_prompt.reshape(N_PROMPT, D_MODEL), x_sample.reshape(N_SAMPLE, D_MODEL)], axis=0)

    cvec = jnp.concatenate([c_ctx[None], c, jnp.zeros((3, D_MODEL), F32)], axis=0)
    mod = _modulation(cvec, w_mod, b_mod).reshape(DEPTH, 8, 6, D_MODEL)
    mod = jnp.pad(mod, ((0, 0), (0, 0), (0, 2), (0, 0)))
    group_src = np.array([0] * (N_PROMPT // MOD_GROUP) + list(range(1, 1 + DEC_BATCH)))
    mod = mod[:, group_src]

    lb_cum = jnp.cumsum(jax.nn.softmax(hg_lb_logits.astype(F32), axis=0), axis=0)
    lb = lb_cum - lb_cum[0:1]
    lb_tab = jnp.stack([jnp.log(lb[:, 0]), jnp.log1p(-lb[:, 0]),
                        jnp.log(lb[:, 1]), jnp.log1p(-lb[:, 1])], axis=1)

    rope = _rope_tables()
    tables = _level_tables()
    kc = cache_k.transpose(1, 0, 3, 2, 4).astype(BF16)
    vc = cache_v.transpose(1, 0, 3, 2, 4).astype(BF16)
    ffn_w = [ffn_w1, ffn_w3, ffn_w2]
    moe_w = [w.reshape((-1,) + w.shape[2:]) for w in (moe_w1, moe_w3, moe_w2)]

    (h,) = _resnorm(x, [], None, None, 0, norm_gains[0, 0], mod[0], 1, 0)
    mix = jnp.zeros((N_TOK, D_MODEL), BF16)
    kbuf = jnp.zeros((BATCH, DEPTH, SEQ, ATT_KV_HEADS, HEAD_DIM), F32)
    vbuf = jnp.zeros((BATCH, DEPTH, SEQ, ATT_KV_HEADS, HEAD_DIM), F32)
    sbuf = jnp.zeros((BATCH, DEPTH, 2, HG_HEADS, HG_DIM, HG_DIM), F32)
    for l in range(DEPTH):
        i = l // 2
        moe = l % 2 == 1
        p = _dense(h, [w_in], l, F32, "proj_in")
        mix, kbuf, vbuf = _attn_ctx(p, attn_sink[l], l, mix, kbuf, vbuf)
        mix = _attn_lat(p, attn_sink[l], kc, vc, l, rope, mix)
        mix, sbuf = _hgrn(p, lb_tab, hg_norm_gain[l], tables, SEQ, BATCH, 4, 0, l, mix, sbuf=sbuf)
        (mix,) = _hgrn(p, lb_tab, hg_norm_gain[l], tables, DEC_SEQ, DEC_BATCH, 1, N_PROMPT // DEC_SEQ,
                       l, mix, init=state_hgrn)
        m = _dense(mix, [w_out], l, F32, "proj_out")
        if not moe:
            x, h = _resnorm(x, [m], norm_gains[l, 1], mod[l], 2, norm_gains[l, 2], mod[l], 4, 3)
            act = _dense(h, ffn_w[:2], i, BF16, "ffn_up")
            f = [_dense(act, ffn_w[2:], i, F32, "ffn_down")]
            gates = None
        else:
            router = jnp.pad(moe_router[i], ((0, 0), (0, HEAD_DIM - N_EXPERTS))).astype(BF16)
            x, h_wide, table = _resnorm(x, [m], norm_gains[l, 1], mod[l], 2, norm_gains[l, 2], mod[l], 4, 3,
                                        router=router)
            gates, pos, row_token, tile_slab, tile_valid = _route(table, i * N_EXPERTS)
            xs = _rows(h_wide, row_token)
            act = _gmm(xs, moe_w[:2], tile_slab, tile_valid, BF16, "moe_up")
            ys = _gmm(act, moe_w[2:], tile_slab, tile_valid, F32, "moe_down")
            f = [_rows(ys, pos.reshape(-1)).reshape(N_TOK, 2 * D_MODEL)]
        if l + 1 < DEPTH:
            x, h = _resnorm(x, f, norm_gains[l, 3], mod[l], 5, norm_gains[l + 1, 0], mod[l + 1], 1, 0,
                            gates=gates)
        else:
            (x,) = _resnorm(x, f, norm_gains[l, 3], mod[l], 5, None, None, 0, 0, gates=gates)
    return (x[:N_PROMPT].reshape(BATCH, SEQ, D_MODEL), x[N_PROMPT:].reshape(DEC_BATCH, DEC_SEQ, D_MODEL),
            kbuf, vbuf, sbuf)
```

```python
import functools

import numpy as np
import jax
import jax.numpy as jnp
from jax import lax
from jax.experimental import pallas as pl
from jax.experimental.pallas import tpu as pltpu

F32 = jnp.float32
BF16 = jnp.bfloat16

D_MODEL = 2048
BATCH = 16
SEQ = 256
DEPTH = 4
DEC_BATCH = 4
DEC_SEQ = 1024
PAST_LEN = 256
GRID_W = 64
HEAD_DIM = 128
ATT_HEADS = 8
ATT_KV_HEADS = 2
Q_PER_KV = 4
ATT_WIDTH = 1024
WINDOW = 128
ATT_BLOCK = 128
ROPE_BASE = 10000.0
HG_WIDTH = 1024
HG_HEADS = 8
HG_DIM = 128
N_EXPERTS = 8
D_FF = 5632
D_FF_EXPERT = 2816
NORM_EPS = 1e-6
IN_COLS = 6656

N_PROMPT = BATCH * SEQ
N_SAMPLE = DEC_BATCH * DEC_SEQ
N_TOK = N_PROMPT + N_SAMPLE
MOD_GROUP = 1024
N_GROUPS = N_TOK // MOD_GROUP

COL_K = 8
COL_V = 10
COL_HQ = 12
COL_HF = 20
COL_HB = 28
COL_HI = 36
COL_HG = 44
MIX_HG = ATT_WIDTH // HG_DIM

CHUNK = 128
N_LEVELS = 7
LOG2E = 1.4426950408889634

VMEM_LIMIT = 60 * 1024 * 1024

TILES = {
    "proj_in": (1024, 1664),
    "proj_out": (512, 2048),
    "ffn_up": (512, 1408),
    "ffn_down": (512, 1024),
    "moe_up": (256, 1408),
    "moe_down": (256, 2048),
}

MOE_TM = TILES["moe_up"][0]
MOE_ROWS = 2 * N_TOK + N_EXPERTS * MOE_TM
MOE_TILES = MOE_ROWS // MOE_TM

ANY_SPEC = pl.BlockSpec(memory_space=pl.ANY)


def _params(sem):
    return pltpu.CompilerParams(dimension_semantics=sem, vmem_limit_bytes=VMEM_LIMIT)


def _silu(x):
    return x / (1.0 + jnp.exp(-x))


def _dot_nt(a, b):
    return lax.dot_general(a, b, (((1,), (1,)), ((), ())), preferred_element_type=F32)


def _dot_tn(a, b):
    return lax.dot_general(a, b, (((0,), (0,)), ((), ())), preferred_element_type=F32)


def _mod_kernel(c_ref, w_ref, b_ref, o_ref):
    s = _silu(c_ref[...]).astype(BF16)
    o_ref[...] = jnp.dot(s, w_ref[...].astype(BF16), preferred_element_type=F32) + b_ref[...]


def _modulation(cvec, w_mod, b_mod):
    tn = 1024
    n = 6 * D_MODEL
    return pl.pallas_call(
        _mod_kernel,
        grid=(DEPTH, n // tn),
        in_specs=[
            pl.BlockSpec((8, D_MODEL), lambda l, j: (0, 0)),
            pl.BlockSpec((None, D_MODEL, tn), lambda l, j: (l, 0, j)),
            pl.BlockSpec((None, 1, tn), lambda l, j: (l, 0, j)),
        ],
        out_specs=pl.BlockSpec((None, 8, tn), lambda l, j: (l, 0, j)),
        out_shape=jax.ShapeDtypeStruct((DEPTH, 8, n), F32),
        compiler_params=_params(("arbitrary", "arbitrary")),
        name="modulation",
    )(cvec, w_mod, b_mod.reshape(DEPTH, 1, n))


def _rms(x, gain):
    return x * lax.rsqrt(jnp.mean(x * x, axis=-1, keepdims=True) + NORM_EPS) * gain


def _top2_gates(lg):
    lane = lax.broadcasted_iota(jnp.int32, lg.shape, 1)
    lg = jnp.where(lane < N_EXPERTS, lg, -jnp.inf)
    m1 = jnp.max(lg, axis=-1, keepdims=True)
    i1 = jnp.min(jnp.where(lg == m1, lane, HEAD_DIM), axis=-1, keepdims=True)
    lg2 = jnp.where(lane == i1, -jnp.inf, lg)
    m2 = jnp.max(lg2, axis=-1, keepdims=True)
    i2 = jnp.min(jnp.where(lg2 == m2, lane, HEAD_DIM), axis=-1, keepdims=True)
    e2 = jnp.exp(m2 - m1)
    den = 1.0 + e2
    return jnp.where(lane == 0, i1.astype(F32),
                     jnp.where(lane == 1, i2.astype(F32),
                               jnp.where(lane == 2, 1.0 / den,
                                         jnp.where(lane == 3, e2 / den, 0.0))))


def _resnorm_kernel(*refs, n_m, gate_row, has_h, sc_row, sh_row, routed):
    refs = list(refs)
    x_ref = refs.pop(0)
    x = x_ref[...]
    if n_m == 1:
        m = refs.pop(0)[...]
    elif n_m == 2:
        pair = refs.pop(0)
        gt = refs.pop(0)[...]
        m = gt[:, 0:1] * pair[0] + gt[:, 1:2] * pair[1]
    if n_m:
        ga = refs.pop(0)[...]
        modg = refs.pop(0)
        x = x + modg[gate_row:gate_row + 1, :] * _rms(m, ga)
    if has_h:
        gb = refs.pop(0)[...]
        modh = refs.pop(0)
        h = _rms(x, gb) * (1.0 + modh[sc_row:sc_row + 1, :]) + modh[sh_row:sh_row + 1, :]
    if routed:
        router = refs.pop(0)[...]
    if n_m:
        refs.pop(0)[...] = x
    if routed:
        refs.pop(0)[...] = h
        refs.pop(0)[...] = _top2_gates(jnp.dot(h.astype(BF16), router, preferred_element_type=F32))
    elif has_h:
        refs.pop(0)[...] = h.astype(BF16)


def _resnorm(x, ms, gain_m, mod_gate, gate_row, gain_h, mod_h, sc_row, sh_row, gates=None, router=None):
    tm = 256
    per = MOD_GROUP // tm
    n_m = len(ms) if gates is None else 2
    has_h = gain_h is not None
    routed = router is not None
    row = pl.BlockSpec((tm, D_MODEL), lambda i: (i, 0))
    vec = pl.BlockSpec((1, D_MODEL), lambda i: (0, 0))
    mod = pl.BlockSpec((None, 8, D_MODEL), lambda i: (i // per, 0, 0))
    args, specs = [x], [row]
    if n_m == 1:
        args.append(ms[0])
        specs.append(row)
    elif n_m == 2:
        args += [ms[0], gates]
        specs += [pl.BlockSpec((2, tm, D_MODEL), lambda i: (0, i, 0)), pl.BlockSpec((tm, 2), lambda i: (i, 0))]
    if n_m:
        args += [gain_m.reshape(1, D_MODEL), mod_gate]
        specs += [vec, mod]
    if has_h:
        args += [gain_h.reshape(1, D_MODEL), mod_h]
        specs += [vec, mod]
    if routed:
        args.append(router)
        specs.append(pl.BlockSpec((D_MODEL, HEAD_DIM), lambda i: (0, 0)))
    out_shape, out_specs = [], []
    if n_m:
        out_shape.append(jax.ShapeDtypeStruct((N_TOK, D_MODEL), F32))
        out_specs.append(row)
    if routed:
        out_shape += [jax.ShapeDtypeStruct((N_TOK, D_MODEL), F32), jax.ShapeDtypeStruct((N_TOK, HEAD_DIM), F32)]
        out_specs += [row, pl.BlockSpec((tm, HEAD_DIM), lambda i: (i, 0))]
    elif has_h:
        out_shape.append(jax.ShapeDtypeStruct((N_TOK, D_MODEL), BF16))
        out_specs.append(row)
    outs = pl.pallas_call(
        functools.partial(_resnorm_kernel, n_m=n_m, gate_row=gate_row, has_h=has_h,
                          sc_row=sc_row, sh_row=sh_row, routed=routed),
        grid=(N_TOK // tm,),
        in_specs=specs,
        out_specs=out_specs,
        out_shape=out_shape,
        compiler_params=_params(("arbitrary",)),
        name="resnorm",
    )(*args)
    return outs


CAST_ROWS = 256


def _gmm_kernel(load_ref, slab_ref, col_ref, more_ref, nslab_ref, ncol_ref, valid_ref, x_ref, *rest,
                nw, tn, k_rows):
    w_hbm = rest[:nw]
    o_ref = rest[nw]
    stage = rest[nw + 1:2 * nw + 1]
    wb = rest[2 * nw + 1:3 * nw + 1]
    sem = rest[3 * nw + 1]
    t = pl.program_id(0) * pl.num_programs(1) + pl.program_id(1)

    def copies(slab, col):
        cols = pl.ds(pl.multiple_of(col * tn, 128), tn)
        return [pltpu.make_async_copy(w_hbm[k].at[slab, :, cols], stage[k], sem.at[k]) for k in range(nw)]

    @pl.when(t == 0)
    def _():
        for cp in copies(slab_ref[0], col_ref[0]):
            cp.start()

    @pl.when(load_ref[t] != 0)
    def _():
        for cp in copies(slab_ref[t], col_ref[t]):
            cp.wait()

        def cast(r, carry):
            rows = pl.ds(pl.multiple_of(r * CAST_ROWS, CAST_ROWS), CAST_ROWS)
            for k in range(nw):
                wb[k][rows, :] = stage[k][rows, :].astype(BF16)
            return carry

        lax.fori_loop(0, k_rows // CAST_ROWS, cast, 0)

        @pl.when(more_ref[t] != 0)
        def _():
            for cp in copies(nslab_ref[t], ncol_ref[t]):
                cp.start()

    valid = valid_ref[pl.program_id(1)] != 0

    @pl.when(valid)
    def _():
        x = x_ref[...].astype(BF16)
        a = jnp.dot(x, wb[0][...], preferred_element_type=F32)
        if nw == 2:
            a = _silu(a) * jnp.dot(x, wb[1][...], preferred_element_type=F32)
        o_ref[...] = a.astype(o_ref.dtype)

    @pl.when(jnp.logical_not(valid))
    def _():
        o_ref[...] = jnp.zeros_like(o_ref)


def _weight_schedule(tile_slab, tile_valid, n_col):
    n_row = tile_slab.shape[0]
    idx = jnp.arange(n_row, dtype=jnp.int32)
    keep = lax.cummax(jnp.where(tile_valid != 0, idx, 0), axis=0)
    slab = jnp.tile(tile_slab[keep], n_col)
    col = jnp.repeat(jnp.arange(n_col, dtype=jnp.int32), n_row)
    steps = n_row * n_col
    key = slab * n_col + col
    load = jnp.concatenate([jnp.ones((1,), bool), key[1:] != key[:-1]])
    load_at = jnp.where(load, jnp.arange(steps, dtype=jnp.int32), steps)
    first_from = lax.cummin(load_at, axis=0, reverse=True)
    nxt = jnp.concatenate([first_from[1:], jnp.full((1,), steps, jnp.int32)])
    more = nxt < steps
    nxt = jnp.minimum(nxt, steps - 1)
    return (load.astype(jnp.int32), slab.astype(jnp.int32), col, more.astype(jnp.int32),
            slab[nxt].astype(jnp.int32), col[nxt])


def _gmm(x, ws, tile_slab, tile_valid, out_dtype, name):
    tm, tn = TILES[name]
    m, k = x.shape
    n = ws[0].shape[-1]
    nw = len(ws)
    sched = _weight_schedule(tile_slab, tile_valid, n // tn)
    return pl.pallas_call(
        functools.partial(_gmm_kernel, nw=nw, tn=tn, k_rows=k),
        grid_spec=pltpu.PrefetchScalarGridSpec(
            num_scalar_prefetch=7,
            grid=(n // tn, m // tm),
            in_specs=[pl.BlockSpec((tm, k), lambda j, i, *_: (i, 0))] + [ANY_SPEC] * nw,
            out_specs=pl.BlockSpec((tm, tn), lambda j, i, *_: (i, j)),
            scratch_shapes=([pltpu.VMEM((k, tn), F32)] * nw + [pltpu.VMEM((k, tn), BF16)] * nw
                            + [pltpu.SemaphoreType.DMA((nw,))]),
        ),
        out_shape=jax.ShapeDtypeStruct((m, n), out_dtype),
        compiler_params=_params(("arbitrary", "arbitrary")),
        name=name,
    )(*sched, tile_valid, x, *ws)


def _dense(x, ws, slab, out_dtype, name):
    tiles = x.shape[0] // TILES[name][0]
    return _gmm(x, ws, jnp.full((tiles,), slab, jnp.int32), jnp.ones((tiles,), jnp.int32),
                out_dtype, name)


def _sink_rows(sink_ref, g, rows, per):
    r = lax.broadcasted_iota(jnp.int32, (rows, 1), 0)
    sk = jnp.full((rows, 1), sink_ref[g * Q_PER_KV + Q_PER_KV - 1], F32)
    for h in range(Q_PER_KV - 2, -1, -1):
        sk = jnp.where(r < (h + 1) * per, sink_ref[g * Q_PER_KV + h], sk)
    return sk


def _attn_ctx_kernel(*refs):
    sink_ref, q_ref, k_ref, v_ref = refs[:4]
    o_ref, ko_ref, vo_ref = refs[-3:]
    g = pl.program_id(1)
    scale = HEAD_DIM ** -0.5
    k = k_ref[...]
    v = v_ref[...]
    for head in range(ATT_KV_HEADS):
        @pl.when(g == head)
        def _(head=head):
            ko_ref[:, head, :] = k
            vo_ref[:, head, :] = v
    q = q_ref[...]
    qs = jnp.concatenate([q[:, r * HEAD_DIM:(r + 1) * HEAD_DIM] for r in range(Q_PER_KV)], axis=0) * scale
    s = _dot_nt(qs.astype(BF16), k.astype(BF16))
    sk = _sink_rows(sink_ref, g, Q_PER_KV * SEQ, SEQ)
    m = jnp.maximum(jnp.max(s, axis=-1, keepdims=True), sk)
    e = jnp.exp(s - m)
    den = jnp.sum(e, axis=-1, keepdims=True) + jnp.exp(sk - m)
    o = jnp.dot(e.astype(BF16), v.astype(BF16), preferred_element_type=F32) / den
    o_ref[...] = jnp.concatenate([o[r * SEQ:(r + 1) * SEQ] for r in range(Q_PER_KV)],
                                 axis=1).astype(o_ref.dtype)


def _attn_ctx(p, sink, layer, mix, kbuf, vbuf):
    kv_spec = pl.BlockSpec((None, None, SEQ, ATT_KV_HEADS, HEAD_DIM), lambda b, g: (b, layer, 0, 0, 0))
    kv_shape = jax.ShapeDtypeStruct((BATCH, DEPTH, SEQ, ATT_KV_HEADS, HEAD_DIM), F32)
    in_specs = [
        pl.BlockSpec(memory_space=pltpu.SMEM),
        pl.BlockSpec((SEQ, Q_PER_KV * HEAD_DIM), lambda b, g: (b, g)),
        pl.BlockSpec((SEQ, HEAD_DIM), lambda b, g: (b, COL_K + g)),
        pl.BlockSpec((SEQ, HEAD_DIM), lambda b, g: (b, COL_V + g)),
    ]
    in_specs += [ANY_SPEC] * 3
    args = [sink, p, p, p, mix, kbuf, vbuf]
    aliases = {4: 0, 5: 1, 6: 2}
    return pl.pallas_call(
        _attn_ctx_kernel,
        grid=(BATCH, ATT_KV_HEADS),
        in_specs=in_specs,
        out_specs=[pl.BlockSpec((SEQ, Q_PER_KV * HEAD_DIM), lambda b, g: (b, g)), kv_spec, kv_spec],
        out_shape=[jax.ShapeDtypeStruct((N_TOK, D_MODEL), BF16), kv_shape, kv_shape],
        input_output_aliases=aliases,
        compiler_params=_params(("arbitrary", "arbitrary")),
        name="attn_ctx",
    )(*args)


def _rope(x, c, sa, sb):
    return x * c + pltpu.roll(x, 96, 1) * sa + pltpu.roll(x, 32, 1) * sb


def _attn_lat_kernel(sink_ref, q_ref, k_ref, v_ref, kc_ref, vc_ref, cq_ref, sqa_ref, sqb_ref,
                     ck_ref, ska_ref, skb_ref, bias_ref, mix_ref, o_ref, kr_s, vb_s):
    del mix_ref
    g = pl.program_id(1)
    qb = pl.program_id(2)
    scale = HEAD_DIM ** -0.5
    band = 3 * ATT_BLOCK

    @pl.when(qb == 0)
    def _():
        kr_s[...] = _rope(k_ref[...], ck_ref[...], ska_ref[...], skb_ref[...]).astype(BF16)
        vb_s[...] = v_ref[...].astype(BF16)

    q = q_ref[...]
    cq, sqa, sqb = cq_ref[...], sqa_ref[...], sqb_ref[...]
    qs = jnp.concatenate(
        [_rope(q[:, r * HEAD_DIM:(r + 1) * HEAD_DIM], cq, sqa, sqb) * scale for r in range(Q_PER_KV)],
        axis=0).astype(BF16)
    rows = Q_PER_KV * ATT_BLOCK
    start = pl.multiple_of(jnp.clip(qb - 1, 0, DEC_SEQ // ATT_BLOCK - 3) * ATT_BLOCK, ATT_BLOCK)
    kb = kr_s[pl.ds(start, band), :]
    vb = vb_s[pl.ds(start, band), :]
    s_loc = _dot_nt(qs, kb) + bias_ref[...]
    s_ctx = _dot_nt(qs, kc_ref[...])
    sk = _sink_rows(sink_ref, g, rows, ATT_BLOCK)
    m = jnp.maximum(jnp.maximum(jnp.max(s_loc, axis=-1, keepdims=True),
                                jnp.max(s_ctx, axis=-1, keepdims=True)), sk)
    e_loc = jnp.exp(s_loc - m)
    e_ctx = jnp.exp(s_ctx - m)
    den = (jnp.sum(e_loc, axis=-1, keepdims=True) + jnp.sum(e_ctx, axis=-1, keepdims=True)
           + jnp.exp(sk - m))
    o = (jnp.dot(e_loc.astype(BF16), vb, preferred_element_type=F32)
         + jnp.dot(e_ctx.astype(BF16), vc_ref[...], preferred_element_type=F32)) / den
    o_ref[...] = jnp.concatenate([o[r * ATT_BLOCK:(r + 1) * ATT_BLOCK] for r in range(Q_PER_KV)],
                                 axis=1).astype(o_ref.dtype)


def _attn_lat(p, sink, kc, vc, layer, rope, mix):
    nqb = DEC_SEQ // ATT_BLOCK
    row0 = N_PROMPT // DEC_SEQ
    qrow0 = N_PROMPT // ATT_BLOCK
    tq = pl.BlockSpec((ATT_BLOCK, HEAD_DIM), lambda b, g, qb: (qb, 0))
    tk = pl.BlockSpec((DEC_SEQ, HEAD_DIM), lambda b, g, qb: (0, 0))
    ctx = pl.BlockSpec((None, None, None, PAST_LEN, HEAD_DIM), lambda b, g, qb: (layer, b, g, 0, 0))
    cos, sin_a, sin_b = rope
    bias = pl.BlockSpec((None, Q_PER_KV * ATT_BLOCK, 3 * ATT_BLOCK),
                        lambda b, g, qb: (jnp.where(qb == 0, 0, jnp.where(qb == nqb - 1, 2, 1)), 0, 0))
    return pl.pallas_call(
        _attn_lat_kernel,
        grid=(DEC_BATCH, ATT_KV_HEADS, nqb),
        in_specs=[
            pl.BlockSpec(memory_space=pltpu.SMEM),
            pl.BlockSpec((ATT_BLOCK, Q_PER_KV * HEAD_DIM), lambda b, g, qb: (qrow0 + b * nqb + qb, g)),
            pl.BlockSpec((DEC_SEQ, HEAD_DIM), lambda b, g, qb: (row0 + b, COL_K + g)),
            pl.BlockSpec((DEC_SEQ, HEAD_DIM), lambda b, g, qb: (row0 + b, COL_V + g)),
            ctx, ctx, tq, tq, tq, tk, tk, tk, bias, ANY_SPEC,
        ],
        out_specs=pl.BlockSpec((ATT_BLOCK, Q_PER_KV * HEAD_DIM),
                               lambda b, g, qb: (qrow0 + b * nqb + qb, g)),
        out_shape=jax.ShapeDtypeStruct((N_TOK, D_MODEL), BF16),
        scratch_shapes=[pltpu.VMEM((DEC_SEQ, HEAD_DIM), BF16), pltpu.VMEM((DEC_SEQ, HEAD_DIM), BF16)],
        input_output_aliases={13: 0},
        compiler_params=_params(("arbitrary", "arbitrary", "arbitrary")),
        name="attn_lat",
    )(sink, p, p, p, kc, vc, cos, sin_a, sin_b, cos, sin_a, sin_b, _band_bias(), mix)


def _band_bias():
    row = np.arange(Q_PER_KV * ATT_BLOCK)[:, None] % ATT_BLOCK
    lane = np.arange(3 * ATT_BLOCK)[None, :]
    cases = [lane - k * ATT_BLOCK - row for k in range(3)]
    return jnp.asarray(np.stack([np.where(np.abs(d) <= WINDOW, 0.0, -np.inf) for d in cases]).astype(np.float32))


def _rope_tables():
    rows = DEC_SEQ // GRID_W
    row = np.repeat(np.arange(rows), GRID_W).astype(np.float32)
    col = np.tile(np.arange(GRID_W), rows).astype(np.float32)
    axis_dim = HEAD_DIM // 2
    inv = (ROPE_BASE ** (-np.arange(0, axis_dim, 2, dtype=np.float32) / axis_dim)).astype(np.float32)
    lane = np.arange(HEAD_DIM)
    pos = np.where((lane // axis_dim)[None, :] == 0, row[:, None], col[:, None])
    ang = pos * inv[lane % (axis_dim // 2)][None, :]
    first = ((lane % axis_dim) < axis_dim // 2)[None, :]
    cos = np.cos(ang).astype(np.float32)
    sin = np.sin(ang).astype(np.float32)
    sin_a = np.where(first, -sin, 0.0).astype(np.float32)
    sin_b = np.where(first, 0.0, sin).astype(np.float32)
    return jnp.asarray(cos), jnp.asarray(sin_a), jnp.asarray(sin_b)


def _level_tables():
    t = np.arange(CHUNK)
    x = t[:, None] ^ t[None, :]
    lvl = np.where(x > 0, np.floor(np.log2(np.maximum(x, 1))), -1.0).astype(np.float32)
    tri = (t[:, None] >= t[None, :]).astype(np.float32)
    return jnp.asarray(lvl), jnp.asarray(np.stack([tri, tri.T]), dtype=BF16)


def _forget_gate(z, log_lb, log1m_lb):
    t = jnp.log(1.0 + jnp.exp(-jnp.abs(z)))
    c = log1m_lb + jnp.minimum(z, 0.0) - t
    log_f = jnp.maximum(log_lb, c) + jnp.log(1.0 + jnp.exp(-jnp.abs(log_lb - c)))
    return log_f, jnp.exp(log1m_lb - jnp.maximum(z, 0.0) - t)


def _running_log2_decay(g, tri):
    g1 = g.astype(BF16)
    g2 = (g - g1.astype(F32)).astype(BF16)
    dot = functools.partial(jnp.dot, preferred_element_type=F32)
    return (dot(tri, g1) + dot(tri, g2)) * LOG2E


def _intra_chunk_weights(qs, kfs, kbs, bfs, bbs, lvl):
    n = len(qs)
    acc = []
    for c in range(n):
        gram = _dot_nt(qs[c].astype(BF16), (kfs[c] + kbs[c]).astype(BF16))
        acc.append(jnp.where(lvl == -1.0, gram, 0.0))
    shape8 = (CHUNK // 8, 8, HG_DIM)
    sub = lax.broadcasted_iota(jnp.int32, shape8, 1)
    for j in range(N_LEVELS):
        half = 1 << j
        for c in range(n):
            q, kf, kb, bf, bb = qs[c], kfs[c], kbs[c], bfs[c], bbs[c]
            if j < 3:
                q, kf, kb, bf, bb = (a.reshape(shape8) for a in (q, kf, kb, bf, bb))
                upper = (sub & half) != 0
                if j == 0:
                    lhs = q * jnp.where(upper, 1.0 - kf, 1.0 - kb)
                    rhs = jnp.where(upper, kb, kf)
                else:
                    if j == 1:
                        ref_f = jnp.where(sub < 4, bf[:, 1:2, :], bf[:, 5:6, :])
                        ref_b = jnp.where(sub < 4, bb[:, 2:3, :], bb[:, 6:7, :])
                    else:
                        ref_f, ref_b = bf[:, 3:4, :], bb[:, 4:5, :]
                    lhs = q * jnp.exp2(jnp.where(upper, bf - ref_f, bb - ref_b))
                    rhs = jnp.where(upper, kb, kf) * jnp.exp2(jnp.where(upper, ref_b - bb, ref_f - bf))
            else:
                shape = (CHUNK // (2 * half), 2 * half, HG_DIM)
                q, kf, kb, bf, bb = (a.reshape(shape) for a in (q, kf, kb, bf, bb))
                ref_f, ref_b = bf[:, half - 1:half, :], bb[:, half:half + 1, :]
                lo, up = slice(0, half), slice(half, 2 * half)
                lhs = q * jnp.exp2(jnp.concatenate([bb[:, lo] - ref_b, bf[:, up] - ref_f], axis=1))
                rhs = (jnp.concatenate([kf[:, lo], kb[:, up]], axis=1)
                       * jnp.exp2(jnp.concatenate([ref_f - bf[:, lo], ref_b - bb[:, up]], axis=1)))
            gram = _dot_nt(lhs.reshape(CHUNK, HG_DIM).astype(BF16), rhs.reshape(CHUNK, HG_DIM).astype(BF16))
            acc[c] = jnp.where(lvl == float(j), gram, acc[c])
    return acc


HGRN_INTERLEAVE = 8


def _hgrn_kernel(*refs, seq_len, n_sub, has_init):
    refs = list(refs)
    hq_ref, hf_ref, hb_ref, hi_ref, hg_ref, lb_ref, gain_ref, tri_ref, lvl_ref = refs[:9]
    refs = refs[9:]
    init_ref = refs.pop(0) if has_init else None
    refs.pop(0)
    if not has_init:
        refs.pop(0)
    out_ref = refs.pop(0)
    fin_ref = None if has_init else refs.pop(0)
    q_s, vb_s, op_s, qdb_s, ub_s, dcb_s, sf_s = refs
    n_chunks = seq_len // CHUNK
    per = min(HGRN_INTERLEAVE // n_sub, n_chunks)
    slots = [(s, c) for s in range(n_sub) for c in range(per)]
    dot = functools.partial(jnp.dot, preferred_element_type=F32)

    q_s[...] = _silu(hq_ref[...])
    vb_s[...] = hi_ref[...].astype(BF16)

    def chunk_slice(s, ci):
        return pl.ds(pl.multiple_of(s * seq_len + ci * CHUNK, CHUNK), CHUNK)

    def fwd_step(i, carry):
        sls = [chunk_slice(s, per * i + c) for s, c in slots]
        lvl = lvl_ref[...]
        gf = [_forget_gate(hf_ref[sl, :], lb_ref[0:1, :], lb_ref[1:2, :]) for sl in sls]
        gb = [_forget_gate(hb_ref[sl, :], lb_ref[2:3, :], lb_ref[3:4, :]) for sl in sls]
        bfs = [_running_log2_decay(g, tri_ref[0]) for g, _ in gf]
        bbs = [_running_log2_decay(g, tri_ref[1]) for g, _ in gb]
        qs = [q_s[sl, :] for sl in sls]
        kfs = [k for _, k in gf]
        kbs = [k for _, k in gb]
        acc = _intra_chunk_weights(qs, kfs, kbs, bfs, bbs, lvl)
        for n, (s, c) in enumerate(slots):
            sl, q, bf, bb = sls[n], qs[n], bfs[n], bbs[n]
            ci = s * n_chunks + per * i + c
            vb = vb_s[sl, :]
            s_f = sf_s[s]
            qd = (q * jnp.exp2(bf)).astype(BF16)
            op_s[sl, :] = dot(jnp.concatenate([qd, acc[n].astype(BF16)], axis=1),
                              jnp.concatenate([s_f.astype(BF16), vb], axis=0))
            tot_f = bf[CHUNK - 1:CHUNK]
            kd = (kfs[n] * jnp.exp2(tot_f - bf)).astype(BF16)
            dcol = jnp.transpose(jnp.broadcast_to(jnp.exp2(tot_f), (HG_DIM, HG_DIM)))
            sf_s[s] = s_f * dcol + _dot_tn(kd, vb)
            tot_b = bb[0:1]
            qdb_s[sl, :] = (q * jnp.exp2(bb)).astype(BF16)
            kd = (kbs[n] * jnp.exp2(tot_b - bb)).astype(BF16)
            ub_s[ci] = _dot_tn(kd, vb)
            dcb_s[ci] = jnp.transpose(jnp.broadcast_to(jnp.exp2(tot_b), (HG_DIM, HG_DIM)))
        return carry

    zero = jnp.zeros((HG_DIM, HG_DIM), F32)
    for s in range(n_sub):
        sf_s[s] = init_ref[s, 0] if has_init else zero
    lax.fori_loop(0, n_chunks // per, fwd_step, 0)

    def bwd_step(i, states):
        states = list(states)
        outs = []
        for s, c in slots:
            ci = n_chunks - 1 - (per * i + c)
            sl = chunk_slice(s, ci)
            outs.append((sl, op_s[sl, :] + dot(qdb_s[sl, :], states[s].astype(BF16))))
            states[s] = states[s] * dcb_s[s * n_chunks + ci] + ub_s[s * n_chunks + ci]
        for sl, o in outs:
            o = o * lax.rsqrt(jnp.mean(o * o, axis=-1, keepdims=True) + NORM_EPS) * gain_ref[...]
            out_ref[sl, :] = (o * _silu(hg_ref[sl, :])).astype(out_ref.dtype)
        return tuple(states)

    s_b = lax.fori_loop(0, n_chunks // per, bwd_step,
                        tuple(init_ref[s, 1] if has_init else zero for s in range(n_sub)))
    if not has_init:
        for s in range(n_sub):
            fin_ref[s, 0] = sf_s[s]
            fin_ref[s, 1] = s_b[s]


def _hgrn(p, lb_tab, gain, tables, seq_len, n_seq, n_sub, row0, layer, mix, init=None, sbuf=None):
    lvl, tri = tables
    has_init = init is not None
    rows = n_sub * seq_len
    n_chunks = seq_len // CHUNK

    def col(c):
        return pl.BlockSpec((rows, HG_DIM), lambda b, h: (row0 + b, c + h))

    state_spec = pl.BlockSpec((n_sub, None, 2, None, HG_DIM, HG_DIM), lambda b, h: (b, layer, 0, h, 0, 0))
    in_specs = [col(COL_HQ), col(COL_HF), col(COL_HB), col(COL_HI), col(COL_HG),
                pl.BlockSpec((None, 4, HG_DIM), lambda b, h: (layer, 0, h)),
                pl.BlockSpec((1, HG_DIM), lambda b, h: (0, 0)),
                pl.BlockSpec((2, CHUNK, CHUNK), lambda b, h: (0, 0, 0)),
                pl.BlockSpec((CHUNK, CHUNK), lambda b, h: (0, 0))]
    args = [p, p, p, p, p, lb_tab, gain.reshape(1, HG_DIM), tri, lvl]
    if has_init:
        in_specs.append(state_spec)
        args.append(init)
    in_specs.append(ANY_SPEC)
    args.append(mix)
    aliases = {len(args) - 1: 0}
    out_specs = [pl.BlockSpec((rows, HG_DIM), lambda b, h: (row0 + b, MIX_HG + h))]
    out_shape = [jax.ShapeDtypeStruct((N_TOK, D_MODEL), BF16)]
    if not has_init:
        in_specs.append(ANY_SPEC)
        args.append(sbuf)
        aliases[len(args) - 1] = 1
        out_specs.append(state_spec)
        out_shape.append(jax.ShapeDtypeStruct((n_seq, DEPTH, 2, HG_HEADS, HG_DIM, HG_DIM), F32))

    def row_buf(dtype):
        return pltpu.VMEM((rows, HG_DIM), dtype)

    def state_buf(n):
        return pltpu.VMEM((n, HG_DIM, HG_DIM), F32)

    return pl.pallas_call(
        functools.partial(_hgrn_kernel, seq_len=seq_len, n_sub=n_sub, has_init=has_init),
        grid=(n_seq // n_sub, HG_HEADS),
        in_specs=in_specs,
        out_specs=out_specs,
        out_shape=out_shape,
        scratch_shapes=[row_buf(F32), row_buf(BF16), row_buf(F32), row_buf(BF16),
                        state_buf(n_sub * n_chunks), state_buf(n_sub * n_chunks), state_buf(n_sub)],
        input_output_aliases=aliases,
        compiler_params=_params(("arbitrary", "arbitrary")),
        name="hgrn_lat" if has_init else "hgrn_ctx",
    )(*args)


def _route(out, slab0):
    top_i = out[:, 0:2].astype(jnp.int32)
    gates = out[:, 2:4]
    e_flat = top_i.reshape(-1)
    onehot = (e_flat[:, None] == jnp.arange(N_EXPERTS)[None, :]).astype(jnp.int32)
    csum = jnp.cumsum(onehot, axis=0)
    counts = csum[-1]
    padded = ((counts + MOE_TM - 1) // MOE_TM) * MOE_TM
    ends = jnp.cumsum(padded)
    starts = ends - padded
    pos = jnp.sum(onehot * (csum - 1 + starts[None, :]), axis=1)
    row_token = jnp.zeros((MOE_ROWS,), jnp.int32).at[pos].set(jnp.arange(2 * N_TOK, dtype=jnp.int32) // 2)
    tile_start = jnp.arange(MOE_TILES, dtype=jnp.int32) * MOE_TM
    tile_expert = jnp.minimum(jnp.sum(tile_start[:, None] >= ends[None, :], axis=1), N_EXPERTS - 1)
    tile_valid = (tile_start < ends[-1]).astype(jnp.int32)
    return gates, pos.reshape(N_TOK, 2), row_token, (slab0 + tile_expert).astype(jnp.int32), tile_valid


def _rows(table, idx):
    return table.at[idx].get(mode="promise_in_bounds")


def kernel(x_prompt, x_sample, cache_k, cache_v, state_hgrn, c, c_ctx, w_mod, b_mod, norm_gains,
           w_in, w_out, attn_sink, hg_lb_logits, hg_norm_gain, ffn_w1, ffn_w3, ffn_w2,
           moe_router, moe_w1, moe_w3, moe_w2):
    x = jnp.concatenate([x_prompt.reshape(N_PROMPT, D_MODEL), x_sample.reshape(N_SAMPLE, D_MODEL)], axis=0)

    cvec = jnp.concatenate([c_ctx[None], c, jnp.zeros((3, D_MODEL), F32)], axis=0)
    mod = _modulation(cvec, w_mod, b_mod).reshape(DEPTH, 8, 6, D_MODEL)
    mod = jnp.pad(mod, ((0, 0), (0, 0), (0, 2), (0, 0)))
    group_src = np.array([0] * (N_PROMPT // MOD_GROUP) + list(range(1, 1 + DEC_BATCH)))
    mod = mod[:, group_src]

    lb_cum = jnp.cumsum(jax.nn.softmax(hg_lb_logits.astype(F32), axis=0), axis=0)
    lb = lb_cum - lb_cum[0:1]
    lb_tab = jnp.stack([jnp.log(lb[:, 0]), jnp.log1p(-lb[:, 0]),
                        jnp.log(lb[:, 1]), jnp.log1p(-lb[:, 1])], axis=1)

    rope = _rope_tables()
    tables = _level_tables()
    kc = cache_k.transpose(1, 0, 3, 2, 4).astype(BF16)
    vc = cache_v.transpose(1, 0, 3, 2, 4).astype(BF16)
    ffn_w = [ffn_w1, ffn_w3, ffn_w2]
    moe_w = [w.reshape((-1,) + w.shape[2:]) for w in (moe_w1, moe_w3, moe_w2)]

    (h,) = _resnorm(x, [], None, None, 0, norm_gains[0, 0], mod[0], 1, 0)
    mix = jnp.zeros((N_TOK, D_MODEL), BF16)
    kbuf = jnp.zeros((BATCH, DEPTH, SEQ, ATT_KV_HEADS, HEAD_DIM), F32)
    vbuf = jnp.zeros((BATCH, DEPTH, SEQ, ATT_KV_HEADS, HEAD_DIM), F32)
    sbuf = jnp.zeros((BATCH, DEPTH, 2, HG_HEADS, HG_DIM, HG_DIM), F32)
    for l in range(DEPTH):
        i = l // 2
        moe = l % 2 == 1
        p = _dense(h, [w_in], l, F32, "proj_in")
        mix, kbuf, vbuf = _attn_ctx(p, attn_sink[l], l, mix, kbuf, vbuf)
        mix = _attn_lat(p, attn_sink[l], kc, vc, l, rope, mix)
        mix, sbuf = _hgrn(p, lb_tab, hg_norm_gain[l], tables, SEQ, BATCH, 4, 0, l, mix, sbuf=sbuf)
        (mix,) = _hgrn(p, lb_tab, hg_norm_gain[l], tables, DEC_SEQ, DEC_BATCH, 1, N_PROMPT // DEC_SEQ,
                       l, mix, init=state_hgrn)
        m = _dense(mix, [w_out], l, F32, "proj_out")
        if not moe:
            x, h = _resnorm(x, [m], norm_gains[l, 1], mod[l], 2, norm_gains[l, 2], mod[l], 4, 3)
            act = _dense(h, ffn_w[:2], i, BF16, "ffn_up")
            f = [_dense(act, ffn_w[2:], i, F32, "ffn_down")]
            gates = None
        else:
            router = jnp.pad(moe_router[i], ((0, 0), (0, HEAD_DIM - N_EXPERTS))).astype(BF16)
            x, h_wide, table = _resnorm(x, [m], norm_gains[l, 1], mod[l], 2, norm_gains[l, 2], mod[l], 4, 3,
                                        router=router)
            gates, pos, row_token, tile_slab, tile_valid = _route(table, i * N_EXPERTS)
            xs = _rows(h_wide, row_token)
            act = _gmm(xs, moe_w[:2], tile_slab, tile_valid, BF16, "moe_up")
            ys = _gmm(act, moe_w[2:], tile_slab, tile_valid, F32, "moe_down")
            f = [_rows(ys, pos.T.reshape(-1)).reshape(2, N_TOK, D_MODEL)]
        if l + 1 < DEPTH:
            x, h = _resnorm(x, f, norm_gains[l, 3], mod[l], 5, norm_gains[l + 1, 0], mod[l + 1], 1, 0,
                            gates=gates)
        else:
            (x,) = _resnorm(x, f, norm_gains[l, 3], mod[l], 5, None, None, 0, 0, gates=gates)
    return (x[:N_PROMPT].reshape(BATCH, SEQ, D_MODEL), x[N_PROMPT:].reshape(DEC_BATCH, DEC_SEQ, D_MODEL),
            kbuf, vbuf, sbuf)
```

```python
import functools

import numpy as np
import jax
import jax.numpy as jnp
from jax import lax
from jax.experimental import pallas as pl
from jax.experimental.pallas import tpu as pltpu

F32 = jnp.float32
BF16 = jnp.bfloat16

D_MODEL = 2048
BATCH = 16
SEQ = 256
DEPTH = 4
DEC_BATCH = 4
DEC_SEQ = 1024
PAST_LEN = 256
GRID_W = 64
HEAD_DIM = 128
ATT_HEADS = 8
ATT_KV_HEADS = 2
Q_PER_KV = 4
ATT_WIDTH = 1024
WINDOW = 128
ATT_BLOCK = 128
ROPE_BASE = 10000.0
HG_WIDTH = 1024
HG_HEADS = 8
HG_DIM = 128
N_EXPERTS = 8
D_FF = 5632
D_FF_EXPERT = 2816
NORM_EPS = 1e-6
IN_COLS = 6656

N_PROMPT = BATCH * SEQ
N_SAMPLE = DEC_BATCH * DEC_SEQ
N_TOK = N_PROMPT + N_SAMPLE
MOD_GROUP = 1024
N_GROUPS = N_TOK // MOD_GROUP

COL_K = 8
COL_V = 10
COL_HQ = 12
COL_HF = 20
COL_HB = 28
COL_HI = 36
COL_HG = 44
MIX_HG = ATT_WIDTH // HG_DIM

CHUNK = 128
N_LEVELS = 7
LOG2E = 1.4426950408889634

VMEM_LIMIT = 60 * 1024 * 1024

TILES = {
    "proj_in": (1024, 1664),
    "proj_out": (512, 2048),
    "ffn_up": (512, 1408),
    "ffn_down": (512, 1024),
    "moe_up": (256, 1408),
    "moe_down": (256, 2048),
}

MOE_TM = TILES["moe_up"][0]
MOE_ROWS = 2 * N_TOK + N_EXPERTS * MOE_TM
MOE_TILES = MOE_ROWS // MOE_TM

ANY_SPEC = pl.BlockSpec(memory_space=pl.ANY)


def _params(sem):
    return pltpu.CompilerParams(dimension_semantics=sem, vmem_limit_bytes=VMEM_LIMIT)


def _silu(x):
    return x / (1.0 + jnp.exp(-x))


def _dot_nt(a, b):
    return lax.dot_general(a, b, (((1,), (1,)), ((), ())), preferred_element_type=F32)


def _dot_tn(a, b):
    return lax.dot_general(a, b, (((0,), (0,)), ((), ())), preferred_element_type=F32)


def _mod_kernel(c_ref, w_ref, b_ref, o_ref):
    s = _silu(c_ref[...]).astype(BF16)
    o_ref[...] = jnp.dot(s, w_ref[...].astype(BF16), preferred_element_type=F32) + b_ref[...]


def _modulation(cvec, w_mod, b_mod):
    tn = 1024
    n = 6 * D_MODEL
    return pl.pallas_call(
        _mod_kernel,
        grid=(DEPTH, n // tn),
        in_specs=[
            pl.BlockSpec((8, D_MODEL), lambda l, j: (0, 0)),
            pl.BlockSpec((None, D_MODEL, tn), lambda l, j: (l, 0, j)),
            pl.BlockSpec((None, 1, tn), lambda l, j: (l, 0, j)),
        ],
        out_specs=pl.BlockSpec((None, 8, tn), lambda l, j: (l, 0, j)),
        out_shape=jax.ShapeDtypeStruct((DEPTH, 8, n), F32),
        compiler_params=_params(("arbitrary", "arbitrary")),
        name="modulation",
    )(cvec, w_mod, b_mod.reshape(DEPTH, 1, n))


def _rms(x, gain):
    return x * lax.rsqrt(jnp.mean(x * x, axis=-1, keepdims=True) + NORM_EPS) * gain


def _top2_gates(lg):
    lane = lax.broadcasted_iota(jnp.int32, lg.shape, 1)
    lg = jnp.where(lane < N_EXPERTS, lg, -jnp.inf)
    m1 = jnp.max(lg, axis=-1, keepdims=True)
    i1 = jnp.min(jnp.where(lg == m1, lane, HEAD_DIM), axis=-1, keepdims=True)
    lg2 = jnp.where(lane == i1, -jnp.inf, lg)
    m2 = jnp.max(lg2, axis=-1, keepdims=True)
    i2 = jnp.min(jnp.where(lg2 == m2, lane, HEAD_DIM), axis=-1, keepdims=True)
    e2 = jnp.exp(m2 - m1)
    den = 1.0 + e2
    return jnp.where(lane == 0, i1.astype(F32),
                     jnp.where(lane == 1, i2.astype(F32),
                               jnp.where(lane == 2, 1.0 / den,
                                         jnp.where(lane == 3, e2 / den, 0.0))))


def _resnorm_kernel(*refs, n_m, gate_row, has_h, sc_row, sh_row, routed):
    refs = list(refs)
    x_ref = refs.pop(0)
    x = x_ref[...]
    if n_m == 1:
        m = refs.pop(0)[...]
    elif n_m == 2:
        pair = refs.pop(0)
        gt = refs.pop(0)[...]
        m = gt[:, 0:1] * pair[0] + gt[:, 1:2] * pair[1]
    if n_m:
        ga = refs.pop(0)[...]
        modg = refs.pop(0)
        x = x + modg[gate_row:gate_row + 1, :] * _rms(m, ga)
    if has_h:
        gb = refs.pop(0)[...]
        modh = refs.pop(0)
        h = _rms(x, gb) * (1.0 + modh[sc_row:sc_row + 1, :]) + modh[sh_row:sh_row + 1, :]
    if routed:
        router = refs.pop(0)[...]
    if n_m:
        refs.pop(0)[...] = x
    if routed:
        refs.pop(0)[...] = h
        refs.pop(0)[...] = _top2_gates(jnp.dot(h.astype(BF16), router, preferred_element_type=F32))
    elif has_h:
        refs.pop(0)[...] = h.astype(BF16)


def _resnorm(x, ms, gain_m, mod_gate, gate_row, gain_h, mod_h, sc_row, sh_row, gates=None, router=None):
    tm = 256
    per = MOD_GROUP // tm
    n_m = len(ms) if gates is None else 2
    has_h = gain_h is not None
    routed = router is not None
    row = pl.BlockSpec((tm, D_MODEL), lambda i: (i, 0))
    vec = pl.BlockSpec((1, D_MODEL), lambda i: (0, 0))
    mod = pl.BlockSpec((None, 8, D_MODEL), lambda i: (i // per, 0, 0))
    args, specs = [x], [row]
    if n_m == 1:
        args.append(ms[0])
        specs.append(row)
    elif n_m == 2:
        args += [ms[0], gates]
        specs += [pl.BlockSpec((2, tm, D_MODEL), lambda i: (0, i, 0)), pl.BlockSpec((tm, 2), lambda i: (i, 0))]
    if n_m:
        args += [gain_m.reshape(1, D_MODEL), mod_gate]
        specs += [vec, mod]
    if has_h:
        args += [gain_h.reshape(1, D_MODEL), mod_h]
        specs += [vec, mod]
    if routed:
        args.append(router)
        specs.append(pl.BlockSpec((D_MODEL, HEAD_DIM), lambda i: (0, 0)))
    out_shape, out_specs = [], []
    if n_m:
        out_shape.append(jax.ShapeDtypeStruct((N_TOK, D_MODEL), F32))
        out_specs.append(row)
    if routed:
        out_shape += [jax.ShapeDtypeStruct((N_TOK, D_MODEL), F32), jax.ShapeDtypeStruct((N_TOK, HEAD_DIM), F32)]
        out_specs += [row, pl.BlockSpec((tm, HEAD_DIM), lambda i: (i, 0))]
    elif has_h:
        out_shape.append(jax.ShapeDtypeStruct((N_TOK, D_MODEL), BF16))
        out_specs.append(row)
    outs = pl.pallas_call(
        functools.partial(_resnorm_kernel, n_m=n_m, gate_row=gate_row, has_h=has_h,
                          sc_row=sc_row, sh_row=sh_row, routed=routed),
        grid=(N_TOK // tm,),
        in_specs=specs,
        out_specs=out_specs,
        out_shape=out_shape,
        compiler_params=_params(("arbitrary",)),
        name="resnorm",
    )(*args)
    return outs


CAST_ROWS = 256


def _gmm_kernel(load_ref, slab_ref, col_ref, more_ref, nslab_ref, ncol_ref, valid_ref, x_ref, *rest,
                nw, tn, k_rows, aliased):
    w_hbm = rest[:nw]
    rest = rest[nw + 1:] if aliased else rest[nw:]
    o_ref = rest[0]
    stage = rest[1:nw + 1]
    wb = rest[nw + 1:2 * nw + 1]
    sem = rest[2 * nw + 1]
    t = pl.program_id(0) * pl.num_programs(1) + pl.program_id(1)

    def copies(slab, col):
        cols = pl.ds(pl.multiple_of(col * tn, 128), tn)
        return [pltpu.make_async_copy(w_hbm[k].at[slab, :, cols], stage[k], sem.at[k]) for k in range(nw)]

    @pl.when(t == 0)
    def _():
        for cp in copies(slab_ref[0], col_ref[0]):
            cp.start()

    @pl.when(load_ref[t] != 0)
    def _():
        for cp in copies(slab_ref[t], col_ref[t]):
            cp.wait()

        def cast(r, carry):
            rows = pl.ds(pl.multiple_of(r * CAST_ROWS, CAST_ROWS), CAST_ROWS)
            for k in range(nw):
                wb[k][rows, :] = stage[k][rows, :].astype(BF16)
            return carry

        lax.fori_loop(0, k_rows // CAST_ROWS, cast, 0)

        @pl.when(more_ref[t] != 0)
        def _():
            for cp in copies(nslab_ref[t], ncol_ref[t]):
                cp.start()

    valid = valid_ref[pl.program_id(1)] != 0

    @pl.when(valid)
    def _():
        x = x_ref[...].astype(BF16)
        a = jnp.dot(x, wb[0][...], preferred_element_type=F32)
        if nw == 2:
            a = _silu(a) * jnp.dot(x, wb[1][...], preferred_element_type=F32)
        o_ref[...] = a.astype(o_ref.dtype)

    @pl.when(jnp.logical_not(valid))
    def _():
        o_ref[...] = jnp.zeros_like(o_ref)


def _weight_schedule(tile_slab, tile_valid, n_col):
    n_row = tile_slab.shape[0]
    idx = jnp.arange(n_row, dtype=jnp.int32)
    keep = lax.cummax(jnp.where(tile_valid != 0, idx, 0), axis=0)
    slab = jnp.tile(tile_slab[keep], n_col)
    col = jnp.repeat(jnp.arange(n_col, dtype=jnp.int32), n_row)
    steps = n_row * n_col
    key = slab * n_col + col
    load = jnp.concatenate([jnp.ones((1,), bool), key[1:] != key[:-1]])
    load_at = jnp.where(load, jnp.arange(steps, dtype=jnp.int32), steps)
    first_from = lax.cummin(load_at, axis=0, reverse=True)
    nxt = jnp.concatenate([first_from[1:], jnp.full((1,), steps, jnp.int32)])
    more = nxt < steps
    nxt = jnp.minimum(nxt, steps - 1)
    return (load.astype(jnp.int32), slab.astype(jnp.int32), col, more.astype(jnp.int32),
            slab[nxt].astype(jnp.int32), col[nxt])


def _gmm(x, ws, tile_slab, tile_valid, out_dtype, name, into=None, first_tile=0):
    tm, tn = TILES[name]
    m, k = x.shape
    n = ws[0].shape[-1]
    nw = len(ws)
    n_tiles = tile_slab.shape[0]
    last = m // tm - 1
    sched = _weight_schedule(tile_slab, tile_valid, n // tn)
    in_specs = [pl.BlockSpec((tm, k), lambda j, i, *_: (jnp.minimum(i, last), 0))] + [ANY_SPEC] * nw
    args = [*sched, tile_valid, x, *ws]
    aliases = {}
    if into is not None:
        in_specs.append(ANY_SPEC)
        args.append(into)
        aliases = {len(args) - 1: 0}
    return pl.pallas_call(
        functools.partial(_gmm_kernel, nw=nw, tn=tn, k_rows=k, aliased=into is not None),
        grid_spec=pltpu.PrefetchScalarGridSpec(
            num_scalar_prefetch=7,
            grid=(n // tn, n_tiles),
            in_specs=in_specs,
            out_specs=pl.BlockSpec((tm, tn), lambda j, i, *_: (i + first_tile, j)),
            scratch_shapes=([pltpu.VMEM((k, tn), F32)] * nw + [pltpu.VMEM((k, tn), BF16)] * nw
                            + [pltpu.SemaphoreType.DMA((nw,))]),
        ),
        out_shape=jax.ShapeDtypeStruct(((n_tiles * tm, n) if into is None else into.shape), out_dtype),
        input_output_aliases=aliases,
        compiler_params=_params(("arbitrary", "arbitrary")),
        name=name,
    )(*args)


def _dense(x, ws, slab, out_dtype, name):
    tiles = x.shape[0] // TILES[name][0]
    return _gmm(x, ws, jnp.full((tiles,), slab, jnp.int32), jnp.ones((tiles,), jnp.int32),
                out_dtype, name)


def _sink_rows(sink_ref, g, rows, per):
    r = lax.broadcasted_iota(jnp.int32, (rows, 1), 0)
    sk = jnp.full((rows, 1), sink_ref[g * Q_PER_KV + Q_PER_KV - 1], F32)
    for h in range(Q_PER_KV - 2, -1, -1):
        sk = jnp.where(r < (h + 1) * per, sink_ref[g * Q_PER_KV + h], sk)
    return sk


def _attn_ctx_kernel(*refs):
    sink_ref, q_ref, k_ref, v_ref = refs[:4]
    o_ref, ko_ref, vo_ref = refs[-3:]
    g = pl.program_id(1)
    scale = HEAD_DIM ** -0.5
    k = k_ref[...]
    v = v_ref[...]
    for head in range(ATT_KV_HEADS):
        @pl.when(g == head)
        def _(head=head):
            ko_ref[:, head, :] = k
            vo_ref[:, head, :] = v
    q = q_ref[...]
    qs = jnp.concatenate([q[:, r * HEAD_DIM:(r + 1) * HEAD_DIM] for r in range(Q_PER_KV)], axis=0) * scale
    s = _dot_nt(qs.astype(BF16), k.astype(BF16))
    sk = _sink_rows(sink_ref, g, Q_PER_KV * SEQ, SEQ)
    m = jnp.maximum(jnp.max(s, axis=-1, keepdims=True), sk)
    e = jnp.exp(s - m)
    den = jnp.sum(e, axis=-1, keepdims=True) + jnp.exp(sk - m)
    o = jnp.dot(e.astype(BF16), v.astype(BF16), preferred_element_type=F32) / den
    o_ref[...] = jnp.concatenate([o[r * SEQ:(r + 1) * SEQ] for r in range(Q_PER_KV)],
                                 axis=1).astype(o_ref.dtype)


def _attn_ctx(p, sink, layer, mix, kbuf, vbuf):
    kv_spec = pl.BlockSpec((None, None, SEQ, ATT_KV_HEADS, HEAD_DIM), lambda b, g: (b, layer, 0, 0, 0))
    kv_shape = jax.ShapeDtypeStruct((BATCH, DEPTH, SEQ, ATT_KV_HEADS, HEAD_DIM), F32)
    in_specs = [
        pl.BlockSpec(memory_space=pltpu.SMEM),
        pl.BlockSpec((SEQ, Q_PER_KV * HEAD_DIM), lambda b, g: (b, g)),
        pl.BlockSpec((SEQ, HEAD_DIM), lambda b, g: (b, COL_K + g)),
        pl.BlockSpec((SEQ, HEAD_DIM), lambda b, g: (b, COL_V + g)),
    ]
    in_specs += [ANY_SPEC] * 3
    args = [sink, p, p, p, mix, kbuf, vbuf]
    aliases = {4: 0, 5: 1, 6: 2}
    return pl.pallas_call(
        _attn_ctx_kernel,
        grid=(BATCH, ATT_KV_HEADS),
        in_specs=in_specs,
        out_specs=[pl.BlockSpec((SEQ, Q_PER_KV * HEAD_DIM), lambda b, g: (b, g)), kv_spec, kv_spec],
        out_shape=[jax.ShapeDtypeStruct((N_TOK, D_MODEL), BF16), kv_shape, kv_shape],
        input_output_aliases=aliases,
        compiler_params=_params(("arbitrary", "arbitrary")),
        name="attn_ctx",
    )(*args)


def _rope(x, c, sa, sb):
    return x * c + pltpu.roll(x, 96, 1) * sa + pltpu.roll(x, 32, 1) * sb


def _attn_lat_kernel(sink_ref, q_ref, k_ref, v_ref, kc_ref, vc_ref, cq_ref, sqa_ref, sqb_ref,
                     ck_ref, ska_ref, skb_ref, bias_ref, mix_ref, o_ref, kr_s, vb_s):
    del mix_ref
    g = pl.program_id(1)
    qb = pl.program_id(2)
    scale = HEAD_DIM ** -0.5
    band = 3 * ATT_BLOCK

    @pl.when(qb == 0)
    def _():
        kr_s[...] = _rope(k_ref[...], ck_ref[...], ska_ref[...], skb_ref[...]).astype(BF16)
        vb_s[...] = v_ref[...].astype(BF16)

    q = q_ref[...]
    cq, sqa, sqb = cq_ref[...], sqa_ref[...], sqb_ref[...]
    qs = jnp.concatenate(
        [_rope(q[:, r * HEAD_DIM:(r + 1) * HEAD_DIM], cq, sqa, sqb) * scale for r in range(Q_PER_KV)],
        axis=0).astype(BF16)
    rows = Q_PER_KV * ATT_BLOCK
    start = pl.multiple_of(jnp.clip(qb - 1, 0, DEC_SEQ // ATT_BLOCK - 3) * ATT_BLOCK, ATT_BLOCK)
    kb = kr_s[pl.ds(start, band), :]
    vb = vb_s[pl.ds(start, band), :]
    s_loc = _dot_nt(qs, kb) + bias_ref[...]
    s_ctx = _dot_nt(qs, kc_ref[...])
    sk = _sink_rows(sink_ref, g, rows, ATT_BLOCK)
    m = jnp.maximum(jnp.maximum(jnp.max(s_loc, axis=-1, keepdims=True),
                                jnp.max(s_ctx, axis=-1, keepdims=True)), sk)
    e_loc = jnp.exp(s_loc - m)
    e_ctx = jnp.exp(s_ctx - m)
    den = (jnp.sum(e_loc, axis=-1, keepdims=True) + jnp.sum(e_ctx, axis=-1, keepdims=True)
           + jnp.exp(sk - m))
    o = (jnp.dot(e_loc.astype(BF16), vb, preferred_element_type=F32)
         + jnp.dot(e_ctx.astype(BF16), vc_ref[...], preferred_element_type=F32)) / den
    o_ref[...] = jnp.concatenate([o[r * ATT_BLOCK:(r + 1) * ATT_BLOCK] for r in range(Q_PER_KV)],
                                 axis=1).astype(o_ref.dtype)


def _attn_lat(p, sink, kc, vc, layer, rope, mix):
    nqb = DEC_SEQ // ATT_BLOCK
    row0 = N_PROMPT // DEC_SEQ
    qrow0 = N_PROMPT // ATT_BLOCK
    tq = pl.BlockSpec((ATT_BLOCK, HEAD_DIM), lambda b, g, qb: (qb, 0))
    tk = pl.BlockSpec((DEC_SEQ, HEAD_DIM), lambda b, g, qb: (0, 0))
    ctx = pl.BlockSpec((None, None, None, PAST_LEN, HEAD_DIM), lambda b, g, qb: (layer, b, g, 0, 0))
    cos, sin_a, sin_b = rope
    bias = pl.BlockSpec((None, Q_PER_KV * ATT_BLOCK, 3 * ATT_BLOCK),
                        lambda b, g, qb: (jnp.where(qb == 0, 0, jnp.where(qb == nqb - 1, 2, 1)), 0, 0))
    return pl.pallas_call(
        _attn_lat_kernel,
        grid=(DEC_BATCH, ATT_KV_HEADS, nqb),
        in_specs=[
            pl.BlockSpec(memory_space=pltpu.SMEM),
            pl.BlockSpec((ATT_BLOCK, Q_PER_KV * HEAD_DIM), lambda b, g, qb: (qrow0 + b * nqb + qb, g)),
            pl.BlockSpec((DEC_SEQ, HEAD_DIM), lambda b, g, qb: (row0 + b, COL_K + g)),
            pl.BlockSpec((DEC_SEQ, HEAD_DIM), lambda b, g, qb: (row0 + b, COL_V + g)),
            ctx, ctx, tq, tq, tq, tk, tk, tk, bias, ANY_SPEC,
        ],
        out_specs=pl.BlockSpec((ATT_BLOCK, Q_PER_KV * HEAD_DIM),
                               lambda b, g, qb: (qrow0 + b * nqb + qb, g)),
        out_shape=jax.ShapeDtypeStruct((N_TOK, D_MODEL), BF16),
        scratch_shapes=[pltpu.VMEM((DEC_SEQ, HEAD_DIM), BF16), pltpu.VMEM((DEC_SEQ, HEAD_DIM), BF16)],
        input_output_aliases={13: 0},
        compiler_params=_params(("arbitrary", "arbitrary", "arbitrary")),
        name="attn_lat",
    )(sink, p, p, p, kc, vc, cos, sin_a, sin_b, cos, sin_a, sin_b, _band_bias(), mix)


def _band_bias():
    row = np.arange(Q_PER_KV * ATT_BLOCK)[:, None] % ATT_BLOCK
    lane = np.arange(3 * ATT_BLOCK)[None, :]
    cases = [lane - k * ATT_BLOCK - row for k in range(3)]
    return jnp.asarray(np.stack([np.where(np.abs(d) <= WINDOW, 0.0, -np.inf) for d in cases]).astype(np.float32))


def _rope_tables():
    rows = DEC_SEQ // GRID_W
    row = np.repeat(np.arange(rows), GRID_W).astype(np.float32)
    col = np.tile(np.arange(GRID_W), rows).astype(np.float32)
    axis_dim = HEAD_DIM // 2
    inv = (ROPE_BASE ** (-np.arange(0, axis_dim, 2, dtype=np.float32) / axis_dim)).astype(np.float32)
    lane = np.arange(HEAD_DIM)
    pos = np.where((lane // axis_dim)[None, :] == 0, row[:, None], col[:, None])
    ang = pos * inv[lane % (axis_dim // 2)][None, :]
    first = ((lane % axis_dim) < axis_dim // 2)[None, :]
    cos = np.cos(ang).astype(np.float32)
    sin = np.sin(ang).astype(np.float32)
    sin_a = np.where(first, -sin, 0.0).astype(np.float32)
    sin_b = np.where(first, 0.0, sin).astype(np.float32)
    return jnp.asarray(cos), jnp.asarray(sin_a), jnp.asarray(sin_b)


def _level_tables():
    t = np.arange(CHUNK)
    x = t[:, None] ^ t[None, :]
    lvl = np.where(x > 0, np.floor(np.log2(np.maximum(x, 1))), -1.0).astype(np.float32)
    tri = (t[:, None] >= t[None, :]).astype(np.float32)
    return jnp.asarray(lvl), jnp.asarray(np.stack([tri, tri.T]), dtype=BF16)


def _forget_gate(z, log_lb, log1m_lb):
    t = jnp.log(1.0 + jnp.exp(-jnp.abs(z)))
    c = log1m_lb + jnp.minimum(z, 0.0) - t
    log_f = jnp.maximum(log_lb, c) + jnp.log(1.0 + jnp.exp(-jnp.abs(log_lb - c)))
    return log_f, jnp.exp(log1m_lb - jnp.maximum(z, 0.0) - t)


def _running_log2_decay(g, tri):
    g1 = g.astype(BF16)
    g2 = (g - g1.astype(F32)).astype(BF16)
    dot = functools.partial(jnp.dot, preferred_element_type=F32)
    return (dot(tri, g1) + dot(tri, g2)) * LOG2E


def _intra_chunk_weights(qs, kfs, kbs, bfs, bbs, lvl):
    n = len(qs)
    acc = []
    for c in range(n):
        gram = _dot_nt(qs[c].astype(BF16), (kfs[c] + kbs[c]).astype(BF16))
        acc.append(jnp.where(lvl == -1.0, gram, 0.0))
    shape8 = (CHUNK // 8, 8, HG_DIM)
    sub = lax.broadcasted_iota(jnp.int32, shape8, 1)
    for j in range(N_LEVELS):
        half = 1 << j
        for c in range(n):
            q, kf, kb, bf, bb = qs[c], kfs[c], kbs[c], bfs[c], bbs[c]
            if j < 3:
                q, kf, kb, bf, bb = (a.reshape(shape8) for a in (q, kf, kb, bf, bb))
                upper = (sub & half) != 0
                if j == 0:
                    lhs = q * jnp.where(upper, 1.0 - kf, 1.0 - kb)
                    rhs = jnp.where(upper, kb, kf)
                else:
                    if j == 1:
                        ref_f = jnp.where(sub < 4, bf[:, 1:2, :], bf[:, 5:6, :])
                        ref_b = jnp.where(sub < 4, bb[:, 2:3, :], bb[:, 6:7, :])
                    else:
                        ref_f, ref_b = bf[:, 3:4, :], bb[:, 4:5, :]
                    lhs = q * jnp.exp2(jnp.where(upper, bf - ref_f, bb - ref_b))
                    rhs = jnp.where(upper, kb, kf) * jnp.exp2(jnp.where(upper, ref_b - bb, ref_f - bf))
            else:
                shape = (CHUNK // (2 * half), 2 * half, HG_DIM)
                q, kf, kb, bf, bb = (a.reshape(shape) for a in (q, kf, kb, bf, bb))
                ref_f, ref_b = bf[:, half - 1:half, :], bb[:, half:half + 1, :]
                lo, up = slice(0, half), slice(half, 2 * half)
                lhs = q * jnp.exp2(jnp.concatenate([bb[:, lo] - ref_b, bf[:, up] - ref_f], axis=1))
                rhs = (jnp.concatenate([kf[:, lo], kb[:, up]], axis=1)
                       * jnp.exp2(jnp.concatenate([ref_f - bf[:, lo], ref_b - bb[:, up]], axis=1)))
            gram = _dot_nt(lhs.reshape(CHUNK, HG_DIM).astype(BF16), rhs.reshape(CHUNK, HG_DIM).astype(BF16))
            acc[c] = jnp.where(lvl == float(j), gram, acc[c])
    return acc


HGRN_INTERLEAVE = 8


def _hgrn_kernel(*refs, seq_len, n_sub, has_init):
    refs = list(refs)
    hq_ref, hf_ref, hb_ref, hi_ref, hg_ref, lb_ref, gain_ref, tri_ref, lvl_ref = refs[:9]
    refs = refs[9:]
    init_ref = refs.pop(0) if has_init else None
    refs.pop(0)
    if not has_init:
        refs.pop(0)
    out_ref = refs.pop(0)
    fin_ref = None if has_init else refs.pop(0)
    q_s, vb_s, op_s, qdb_s, ub_s, dcb_s, sf_s = refs
    n_chunks = seq_len // CHUNK
    per = min(HGRN_INTERLEAVE // n_sub, n_chunks)
    slots = [(s, c) for s in range(n_sub) for c in range(per)]
    dot = functools.partial(jnp.dot, preferred_element_type=F32)

    q_s[...] = _silu(hq_ref[...])
    vb_s[...] = hi_ref[...].astype(BF16)

    def chunk_slice(s, ci):
        return pl.ds(pl.multiple_of(s * seq_len + ci * CHUNK, CHUNK), CHUNK)

    def fwd_step(i, carry):
        sls = [chunk_slice(s, per * i + c) for s, c in slots]
        lvl = lvl_ref[...]
        gf = [_forget_gate(hf_ref[sl, :], lb_ref[0:1, :], lb_ref[1:2, :]) for sl in sls]
        gb = [_forget_gate(hb_ref[sl, :], lb_ref[2:3, :], lb_ref[3:4, :]) for sl in sls]
        bfs = [_running_log2_decay(g, tri_ref[0]) for g, _ in gf]
        bbs = [_running_log2_decay(g, tri_ref[1]) for g, _ in gb]
        qs = [q_s[sl, :] for sl in sls]
        kfs = [k for _, k in gf]
        kbs = [k for _, k in gb]
        acc = _intra_chunk_weights(qs, kfs, kbs, bfs, bbs, lvl)
        for n, (s, c) in enumerate(slots):
            sl, q, bf, bb = sls[n], qs[n], bfs[n], bbs[n]
            ci = s * n_chunks + per * i + c
            vb = vb_s[sl, :]
            s_f = sf_s[s]
            qd = (q * jnp.exp2(bf)).astype(BF16)
            op_s[sl, :] = dot(jnp.concatenate([qd, acc[n].astype(BF16)], axis=1),
                              jnp.concatenate([s_f.astype(BF16), vb], axis=0))
            tot_f = bf[CHUNK - 1:CHUNK]
            kd = (kfs[n] * jnp.exp2(tot_f - bf)).astype(BF16)
            dcol = jnp.transpose(jnp.broadcast_to(jnp.exp2(tot_f), (HG_DIM, HG_DIM)))
            sf_s[s] = s_f * dcol + _dot_tn(kd, vb)
            tot_b = bb[0:1]
            qdb_s[sl, :] = (q * jnp.exp2(bb)).astype(BF16)
            kd = (kbs[n] * jnp.exp2(tot_b - bb)).astype(BF16)
            ub_s[ci] = _dot_tn(kd, vb)
            dcb_s[ci] = jnp.transpose(jnp.broadcast_to(jnp.exp2(tot_b), (HG_DIM, HG_DIM)))
        return carry

    zero = jnp.zeros((HG_DIM, HG_DIM), F32)
    for s in range(n_sub):
        sf_s[s] = init_ref[s, 0] if has_init else zero
    lax.fori_loop(0, n_chunks // per, fwd_step, 0)

    def bwd_step(i, states):
        states = list(states)
        outs = []
        for s, c in slots:
            ci = n_chunks - 1 - (per * i + c)
            sl = chunk_slice(s, ci)
            outs.append((sl, op_s[sl, :] + dot(qdb_s[sl, :], states[s].astype(BF16))))
            states[s] = states[s] * dcb_s[s * n_chunks + ci] + ub_s[s * n_chunks + ci]
        for sl, o in outs:
            o = o * lax.rsqrt(jnp.mean(o * o, axis=-1, keepdims=True) + NORM_EPS) * gain_ref[...]
            out_ref[sl, :] = (o * _silu(hg_ref[sl, :])).astype(out_ref.dtype)
        return tuple(states)

    s_b = lax.fori_loop(0, n_chunks // per, bwd_step,
                        tuple(init_ref[s, 1] if has_init else zero for s in range(n_sub)))
    if not has_init:
        for s in range(n_sub):
            fin_ref[s, 0] = sf_s[s]
            fin_ref[s, 1] = s_b[s]


def _hgrn(p, lb_tab, gain, tables, seq_len, n_seq, n_sub, row0, layer, mix, init=None, sbuf=None):
    lvl, tri = tables
    has_init = init is not None
    rows = n_sub * seq_len
    n_chunks = seq_len // CHUNK

    def col(c):
        return pl.BlockSpec((rows, HG_DIM), lambda b, h: (row0 + b, c + h))

    state_spec = pl.BlockSpec((n_sub, None, 2, None, HG_DIM, HG_DIM), lambda b, h: (b, layer, 0, h, 0, 0))
    in_specs = [col(COL_HQ), col(COL_HF), col(COL_HB), col(COL_HI), col(COL_HG),
                pl.BlockSpec((None, 4, HG_DIM), lambda b, h: (layer, 0, h)),
                pl.BlockSpec((1, HG_DIM), lambda b, h: (0, 0)),
                pl.BlockSpec((2, CHUNK, CHUNK), lambda b, h: (0, 0, 0)),
                pl.BlockSpec((CHUNK, CHUNK), lambda b, h: (0, 0))]
    args = [p, p, p, p, p, lb_tab, gain.reshape(1, HG_DIM), tri, lvl]
    if has_init:
        in_specs.append(state_spec)
        args.append(init)
    in_specs.append(ANY_SPEC)
    args.append(mix)
    aliases = {len(args) - 1: 0}
    out_specs = [pl.BlockSpec((rows, HG_DIM), lambda b, h: (row0 + b, MIX_HG + h))]
    out_shape = [jax.ShapeDtypeStruct((N_TOK, D_MODEL), BF16)]
    if not has_init:
        in_specs.append(ANY_SPEC)
        args.append(sbuf)
        aliases[len(args) - 1] = 1
        out_specs.append(state_spec)
        out_shape.append(jax.ShapeDtypeStruct((n_seq, DEPTH, 2, HG_HEADS, HG_DIM, HG_DIM), F32))

    def row_buf(dtype):
        return pltpu.VMEM((rows, HG_DIM), dtype)

    def state_buf(n):
        return pltpu.VMEM((n, HG_DIM, HG_DIM), F32)

    return pl.pallas_call(
        functools.partial(_hgrn_kernel, seq_len=seq_len, n_sub=n_sub, has_init=has_init),
        grid=(n_seq // n_sub, HG_HEADS),
        in_specs=in_specs,
        out_specs=out_specs,
        out_shape=out_shape,
        scratch_shapes=[row_buf(F32), row_buf(BF16), row_buf(F32), row_buf(BF16),
                        state_buf(n_sub * n_chunks), state_buf(n_sub * n_chunks), state_buf(n_sub)],
        input_output_aliases=aliases,
        compiler_params=_params(("arbitrary", "arbitrary")),
        name="hgrn_lat" if has_init else "hgrn_ctx",
    )(*args)


def _route(out, slab0):
    top_i = out[:, 0:2].astype(jnp.int32)
    gates = out[:, 2:4]
    e_flat = top_i.reshape(-1)
    onehot = (e_flat[:, None] == jnp.arange(N_EXPERTS)[None, :]).astype(jnp.int32)
    csum = jnp.cumsum(onehot, axis=0)
    counts = csum[-1]
    padded = ((counts + MOE_TM - 1) // MOE_TM) * MOE_TM
    ends = jnp.cumsum(padded)
    starts = ends - padded
    pos = jnp.sum(onehot * (csum - 1 + starts[None, :]), axis=1)
    row_token = jnp.zeros((MOE_ROWS,), jnp.int32).at[pos].set(jnp.arange(2 * N_TOK, dtype=jnp.int32) // 2)
    tile_start = jnp.arange(MOE_TILES, dtype=jnp.int32) * MOE_TM
    tile_expert = jnp.minimum(jnp.sum(tile_start[:, None] >= ends[None, :], axis=1), N_EXPERTS - 1)
    tile_valid = (tile_start < ends[-1]).astype(jnp.int32)
    return gates, pos.reshape(N_TOK, 2), row_token, (slab0 + tile_expert).astype(jnp.int32), tile_valid


def _rows(table, idx):
    return table.at[idx].get(mode="promise_in_bounds")


def kernel(x_prompt, x_sample, cache_k, cache_v, state_hgrn, c, c_ctx, w_mod, b_mod, norm_gains,
           w_in, w_out, attn_sink, hg_lb_logits, hg_norm_gain, ffn_w1, ffn_w3, ffn_w2,
           moe_router, moe_w1, moe_w3, moe_w2):
    x = jnp.concatenate([x_prompt.reshape(N_PROMPT, D_MODEL), x_sample.reshape(N_SAMPLE, D_MODEL)], axis=0)

    cvec = jnp.concatenate([c_ctx[None], c, jnp.zeros((3, D_MODEL), F32)], axis=0)
    mod = _modulation(cvec, w_mod, b_mod).reshape(DEPTH, 8, 6, D_MODEL)
    mod = jnp.pad(mod, ((0, 0), (0, 0), (0, 2), (0, 0)))
    group_src = np.array([0] * (N_PROMPT // MOD_GROUP) + list(range(1, 1 + DEC_BATCH)))
    mod = mod[:, group_src]

    lb_cum = jnp.cumsum(jax.nn.softmax(hg_lb_logits.astype(F32), axis=0), axis=0)
    lb = lb_cum - lb_cum[0:1]
    lb_tab = jnp.stack([jnp.log(lb[:, 0]), jnp.log1p(-lb[:, 0]),
                        jnp.log(lb[:, 1]), jnp.log1p(-lb[:, 1])], axis=1)

    rope = _rope_tables()
    tables = _level_tables()
    kc = cache_k.transpose(1, 0, 3, 2, 4).astype(BF16)
    vc = cache_v.transpose(1, 0, 3, 2, 4).astype(BF16)
    ffn_w = [ffn_w1, ffn_w3, ffn_w2]
    moe_w = [w.reshape((-1,) + w.shape[2:]) for w in (moe_w1, moe_w3, moe_w2)]

    (h,) = _resnorm(x, [], None, None, 0, norm_gains[0, 0], mod[0], 1, 0)
    mix = jnp.zeros((N_TOK, D_MODEL), BF16)
    kbuf = jnp.zeros((BATCH, DEPTH, SEQ, ATT_KV_HEADS, HEAD_DIM), F32)
    vbuf = jnp.zeros((BATCH, DEPTH, SEQ, ATT_KV_HEADS, HEAD_DIM), F32)
    sbuf = jnp.zeros((BATCH, DEPTH, 2, HG_HEADS, HG_DIM, HG_DIM), F32)
    for l in range(DEPTH):
        i = l // 2
        moe = l % 2 == 1
        p = _dense(h, [w_in], l, F32, "proj_in")
        mix, kbuf, vbuf = _attn_ctx(p, attn_sink[l], l, mix, kbuf, vbuf)
        mix = _attn_lat(p, attn_sink[l], kc, vc, l, rope, mix)
        mix, sbuf = _hgrn(p, lb_tab, hg_norm_gain[l], tables, SEQ, BATCH, 4, 0, l, mix, sbuf=sbuf)
        (mix,) = _hgrn(p, lb_tab, hg_norm_gain[l], tables, DEC_SEQ, DEC_BATCH, 1, N_PROMPT // DEC_SEQ,
                       l, mix, init=state_hgrn)
        m = _dense(mix, [w_out], l, F32, "proj_out")
        if not moe:
            x, h = _resnorm(x, [m], norm_gains[l, 1], mod[l], 2, norm_gains[l, 2], mod[l], 4, 3)
            act = _dense(h, ffn_w[:2], i, BF16, "ffn_up")
            f = [_dense(act, ffn_w[2:], i, F32, "ffn_down")]
            gates = None
        else:
            router = jnp.pad(moe_router[i], ((0, 0), (0, HEAD_DIM - N_EXPERTS))).astype(BF16)
            x, h_wide, table = _resnorm(x, [m], norm_gains[l, 1], mod[l], 2, norm_gains[l, 2], mod[l], 4, 3,
                                        router=router)
            gates, pos, row_token, tile_slab, tile_valid = _route(table, i * N_EXPERTS)
            half = MOE_TILES // 2
            first = jnp.arange(MOE_TILES) < half
            act = _gmm(_rows(h_wide, row_token[:half * MOE_TM]), moe_w[:2], tile_slab,
                       jnp.where(first, tile_valid, 0), BF16, "moe_up")
            act = _gmm(_rows(h_wide, row_token[half * MOE_TM:]), moe_w[:2], tile_slab[half:],
                       tile_valid[half:], BF16, "moe_up", into=act, first_tile=half)
            ys = _gmm(act, moe_w[2:], tile_slab, tile_valid, F32, "moe_down")
            f = [_rows(ys, pos.T.reshape(-1)).reshape(2, N_TOK, D_MODEL)]
        if l + 1 < DEPTH:
            x, h = _resnorm(x, f, norm_gains[l, 3], mod[l], 5, norm_gains[l + 1, 0], mod[l + 1], 1, 0,
                            gates=gates)
        else:
            (x,) = _resnorm(x, f, norm_gains[l, 3], mod[l], 5, None, None, 0, 0, gates=gates)
    return (x[:N_PROMPT].reshape(BATCH, SEQ, D_MODEL), x[N_PROMPT:].reshape(DEC_BATCH, DEC_SEQ, D_MODEL),
            kbuf, vbuf, sbuf)
```

```python
import functools

import numpy as np
import jax
import jax.numpy as jnp
from jax import lax
from jax.experimental import pallas as pl
from jax.experimental.pallas import tpu as pltpu

F32 = jnp.float32
BF16 = jnp.bfloat16

D_MODEL = 2048
BATCH = 16
SEQ = 256
DEPTH = 4
DEC_BATCH = 4
DEC_SEQ = 1024
PAST_LEN = 256
GRID_W = 64
HEAD_DIM = 128
ATT_HEADS = 8
ATT_KV_HEADS = 2
Q_PER_KV = 4
ATT_WIDTH = 1024
WINDOW = 128
ATT_BLOCK = 128
ROPE_BASE = 10000.0
HG_WIDTH = 1024
HG_HEADS = 8
HG_DIM = 128
N_EXPERTS = 8
D_FF = 5632
D_FF_EXPERT = 2816
NORM_EPS = 1e-6
IN_COLS = 6656

N_PROMPT = BATCH * SEQ
N_SAMPLE = DEC_BATCH * DEC_SEQ
N_TOK = N_PROMPT + N_SAMPLE
MOD_GROUP = 1024
N_GROUPS = N_TOK // MOD_GROUP

COL_K = 8
COL_V = 10
COL_HQ = 12
COL_HF = 20
COL_HB = 28
COL_HI = 36
COL_HG = 44
MIX_HG = ATT_WIDTH // HG_DIM

CHUNK = 128
N_LEVELS = 7
LOG2E = 1.4426950408889634

VMEM_LIMIT = 60 * 1024 * 1024

TILES = {
    "proj_in": (1024, 1664),
    "proj_out": (512, 2048),
    "ffn_up": (512, 1408),
    "ffn_down": (512, 1024),
    "moe_up": (256, 1408),
    "moe_down": (256, 2048),
}

MOE_TM = TILES["moe_up"][0]
MOE_ROWS = 2 * N_TOK + N_EXPERTS * MOE_TM
MOE_TILES = MOE_ROWS // MOE_TM

ANY_SPEC = pl.BlockSpec(memory_space=pl.ANY)


def _params(sem):
    return pltpu.CompilerParams(dimension_semantics=sem, vmem_limit_bytes=VMEM_LIMIT)


def _silu(x):
    return x / (1.0 + jnp.exp(-x))


def _dot_nt(a, b):
    return lax.dot_general(a, b, (((1,), (1,)), ((), ())), preferred_element_type=F32)


def _dot_tn(a, b):
    return lax.dot_general(a, b, (((0,), (0,)), ((), ())), preferred_element_type=F32)


def _mod_kernel(c_ref, w_ref, b_ref, o_ref):
    s = _silu(c_ref[...]).astype(BF16)
    o_ref[...] = jnp.dot(s, w_ref[...].astype(BF16), preferred_element_type=F32) + b_ref[...]


def _modulation(cvec, w_mod, b_mod):
    tn = 1024
    n = 6 * D_MODEL
    return pl.pallas_call(
        _mod_kernel,
        grid=(DEPTH, n // tn),
        in_specs=[
            pl.BlockSpec((8, D_MODEL), lambda l, j: (0, 0)),
            pl.BlockSpec((None, D_MODEL, tn), lambda l, j: (l, 0, j)),
            pl.BlockSpec((None, 1, tn), lambda l, j: (l, 0, j)),
        ],
        out_specs=pl.BlockSpec((None, 8, tn), lambda l, j: (l, 0, j)),
        out_shape=jax.ShapeDtypeStruct((DEPTH, 8, n), F32),
        compiler_params=_params(("arbitrary", "arbitrary")),
        name="modulation",
    )(cvec, w_mod, b_mod.reshape(DEPTH, 1, n))


def _rms(x, gain):
    return x * lax.rsqrt(jnp.mean(x * x, axis=-1, keepdims=True) + NORM_EPS) * gain


def _top2_gates(lg):
    lane = lax.broadcasted_iota(jnp.int32, lg.shape, 1)
    lg = jnp.where(lane < N_EXPERTS, lg, -jnp.inf)
    m1 = jnp.max(lg, axis=-1, keepdims=True)
    i1 = jnp.min(jnp.where(lg == m1, lane, HEAD_DIM), axis=-1, keepdims=True)
    lg2 = jnp.where(lane == i1, -jnp.inf, lg)
    m2 = jnp.max(lg2, axis=-1, keepdims=True)
    i2 = jnp.min(jnp.where(lg2 == m2, lane, HEAD_DIM), axis=-1, keepdims=True)
    e2 = jnp.exp(m2 - m1)
    den = 1.0 + e2
    return jnp.where(lane == 0, i1.astype(F32),
                     jnp.where(lane == 1, i2.astype(F32),
                               jnp.where(lane == 2, 1.0 / den,
                                         jnp.where(lane == 3, e2 / den, 0.0))))


def _resnorm_kernel(*refs, n_m, gate_row, has_h, sc_row, sh_row, routed, split_tiles):
    refs = list(refs)
    x_ref = refs.pop(0)
    x = x_ref[...]
    if n_m == 1:
        m = refs.pop(0)[...]
    elif n_m == 2:
        pair = refs.pop(0)
        gt = refs.pop(0)[...]
        m = gt[:, 0:1] * pair[0] + gt[:, 1:2] * pair[1]
    if n_m:
        ga = refs.pop(0)[...]
        modg = refs.pop(0)
        x = x + modg[gate_row:gate_row + 1, :] * _rms(m, ga)
    if has_h:
        gb = refs.pop(0)[...]
        modh = refs.pop(0)
        h = _rms(x, gb) * (1.0 + modh[sc_row:sc_row + 1, :]) + modh[sh_row:sh_row + 1, :]
    if routed:
        router = refs.pop(0)[...]
    if split_tiles:
        prompt_ref, latent_ref = refs.pop(0), refs.pop(0)

        @pl.when(pl.program_id(0) < split_tiles)
        def _():
            prompt_ref[...] = x

        @pl.when(pl.program_id(0) >= split_tiles)
        def _():
            latent_ref[...] = x
    elif n_m:
        refs.pop(0)[...] = x
    if routed:
        refs.pop(0)[...] = h
        refs.pop(0)[...] = _top2_gates(jnp.dot(h.astype(BF16), router, preferred_element_type=F32))
    elif has_h:
        refs.pop(0)[...] = h.astype(BF16)


def _resnorm(x, ms, gain_m, mod_gate, gate_row, gain_h, mod_h, sc_row, sh_row, gates=None, router=None,
             split=False):
    tm = 256
    per = MOD_GROUP // tm
    n_m = len(ms) if gates is None else 2
    has_h = gain_h is not None
    routed = router is not None
    row = pl.BlockSpec((tm, D_MODEL), lambda i: (i, 0))
    vec = pl.BlockSpec((1, D_MODEL), lambda i: (0, 0))
    mod = pl.BlockSpec((None, 8, D_MODEL), lambda i: (i // per, 0, 0))
    args, specs = [x], [row]
    if n_m == 1:
        args.append(ms[0])
        specs.append(row)
    elif n_m == 2:
        args += [ms[0], gates]
        specs += [pl.BlockSpec((2, tm, D_MODEL), lambda i: (0, i, 0)), pl.BlockSpec((tm, 2), lambda i: (i, 0))]
    if n_m:
        args += [gain_m.reshape(1, D_MODEL), mod_gate]
        specs += [vec, mod]
    if has_h:
        args += [gain_h.reshape(1, D_MODEL), mod_h]
        specs += [vec, mod]
    if routed:
        args.append(router)
        specs.append(pl.BlockSpec((D_MODEL, HEAD_DIM), lambda i: (0, 0)))
    out_shape, out_specs = [], []
    split_tiles = N_PROMPT // tm if split else 0
    if split:
        out_shape += [jax.ShapeDtypeStruct((N_PROMPT, D_MODEL), F32), jax.ShapeDtypeStruct((N_SAMPLE, D_MODEL), F32)]
        out_specs += [pl.BlockSpec((tm, D_MODEL), lambda i: (jnp.minimum(i, split_tiles - 1), 0)),
                      pl.BlockSpec((tm, D_MODEL), lambda i: (jnp.maximum(i - split_tiles, 0), 0))]
    elif n_m:
        out_shape.append(jax.ShapeDtypeStruct((N_TOK, D_MODEL), F32))
        out_specs.append(row)
    if routed:
        out_shape += [jax.ShapeDtypeStruct((N_TOK, D_MODEL), F32), jax.ShapeDtypeStruct((N_TOK, HEAD_DIM), F32)]
        out_specs += [row, pl.BlockSpec((tm, HEAD_DIM), lambda i: (i, 0))]
    elif has_h:
        out_shape.append(jax.ShapeDtypeStruct((N_TOK, D_MODEL), BF16))
        out_specs.append(row)
    outs = pl.pallas_call(
        functools.partial(_resnorm_kernel, n_m=n_m, gate_row=gate_row, has_h=has_h,
                          sc_row=sc_row, sh_row=sh_row, routed=routed, split_tiles=split_tiles),
        grid=(N_TOK // tm,),
        in_specs=specs,
        out_specs=out_specs,
        out_shape=out_shape,
        compiler_params=_params(("arbitrary",)),
        name="resnorm",
    )(*args)
    return outs


CAST_ROWS = 256


def _gmm_kernel(load_ref, slab_ref, col_ref, more_ref, nslab_ref, ncol_ref, valid_ref, x_ref, *rest,
                nw, tn, k_rows):
    w_hbm = rest[:nw]
    o_ref = rest[nw]
    stage = rest[nw + 1:2 * nw + 1]
    wb = rest[2 * nw + 1:3 * nw + 1]
    sem = rest[3 * nw + 1]
    t = pl.program_id(0) * pl.num_programs(1) + pl.program_id(1)

    def copies(slab, col):
        cols = pl.ds(pl.multiple_of(col * tn, 128), tn)
        return [pltpu.make_async_copy(w_hbm[k].at[slab, :, cols], stage[k], sem.at[k]) for k in range(nw)]

    @pl.when(t == 0)
    def _():
        for cp in copies(slab_ref[0], col_ref[0]):
            cp.start()

    @pl.when(load_ref[t] != 0)
    def _():
        for cp in copies(slab_ref[t], col_ref[t]):
            cp.wait()

        def cast(r, carry):
            rows = pl.ds(pl.multiple_of(r * CAST_ROWS, CAST_ROWS), CAST_ROWS)
            for k in range(nw):
                wb[k][rows, :] = stage[k][rows, :].astype(BF16)
            return carry

        lax.fori_loop(0, k_rows // CAST_ROWS, cast, 0)

        @pl.when(more_ref[t] != 0)
        def _():
            for cp in copies(nslab_ref[t], ncol_ref[t]):
                cp.start()

    valid = valid_ref[pl.program_id(1)] != 0

    @pl.when(valid)
    def _():
        x = x_ref[...].astype(BF16)
        a = jnp.dot(x, wb[0][...], preferred_element_type=F32)
        if nw == 2:
            a = _silu(a) * jnp.dot(x, wb[1][...], preferred_element_type=F32)
        o_ref[...] = a.astype(o_ref.dtype)

    @pl.when(jnp.logical_not(valid))
    def _():
        o_ref[...] = jnp.zeros_like(o_ref)


def _weight_schedule(tile_slab, tile_valid, n_col):
    n_row = tile_slab.shape[0]
    idx = jnp.arange(n_row, dtype=jnp.int32)
    keep = lax.cummax(jnp.where(tile_valid != 0, idx, 0), axis=0)
    slab = jnp.tile(tile_slab[keep], n_col)
    col = jnp.repeat(jnp.arange(n_col, dtype=jnp.int32), n_row)
    steps = n_row * n_col
    key = slab * n_col + col
    load = jnp.concatenate([jnp.ones((1,), bool), key[1:] != key[:-1]])
    load_at = jnp.where(load, jnp.arange(steps, dtype=jnp.int32), steps)
    first_from = lax.cummin(load_at, axis=0, reverse=True)
    nxt = jnp.concatenate([first_from[1:], jnp.full((1,), steps, jnp.int32)])
    more = nxt < steps
    nxt = jnp.minimum(nxt, steps - 1)
    return (load.astype(jnp.int32), slab.astype(jnp.int32), col, more.astype(jnp.int32),
            slab[nxt].astype(jnp.int32), col[nxt])


def _gmm(x, ws, tile_slab, tile_valid, out_dtype, name):
    tm, tn = TILES[name]
    m, k = x.shape
    n = ws[0].shape[-1]
    nw = len(ws)
    sched = _weight_schedule(tile_slab, tile_valid, n // tn)
    return pl.pallas_call(
        functools.partial(_gmm_kernel, nw=nw, tn=tn, k_rows=k),
        grid_spec=pltpu.PrefetchScalarGridSpec(
            num_scalar_prefetch=7,
            grid=(n // tn, m // tm),
            in_specs=[pl.BlockSpec((tm, k), lambda j, i, *_: (i, 0))] + [ANY_SPEC] * nw,
            out_specs=pl.BlockSpec((tm, tn), lambda j, i, *_: (i, j)),
            scratch_shapes=([pltpu.VMEM((k, tn), F32)] * nw + [pltpu.VMEM((k, tn), BF16)] * nw
                            + [pltpu.SemaphoreType.DMA((nw,))]),
        ),
        out_shape=jax.ShapeDtypeStruct((m, n), out_dtype),
        compiler_params=_params(("arbitrary", "arbitrary")),
        name=name,
    )(*sched, tile_valid, x, *ws)


def _dense(x, ws, slab, out_dtype, name):
    tiles = x.shape[0] // TILES[name][0]
    return _gmm(x, ws, jnp.full((tiles,), slab, jnp.int32), jnp.ones((tiles,), jnp.int32),
                out_dtype, name)


def _sink_rows(sink_ref, g, rows, per):
    r = lax.broadcasted_iota(jnp.int32, (rows, 1), 0)
    sk = jnp.full((rows, 1), sink_ref[g * Q_PER_KV + Q_PER_KV - 1], F32)
    for h in range(Q_PER_KV - 2, -1, -1):
        sk = jnp.where(r < (h + 1) * per, sink_ref[g * Q_PER_KV + h], sk)
    return sk


def _attn_ctx_kernel(*refs):
    sink_ref, q_ref, k_ref, v_ref = refs[:4]
    o_ref, ko_ref, vo_ref = refs[-3:]
    g = pl.program_id(1)
    scale = HEAD_DIM ** -0.5
    k = k_ref[...]
    v = v_ref[...]
    for head in range(ATT_KV_HEADS):
        @pl.when(g == head)
        def _(head=head):
            ko_ref[:, head, :] = k
            vo_ref[:, head, :] = v
    q = q_ref[...]
    qs = jnp.concatenate([q[:, r * HEAD_DIM:(r + 1) * HEAD_DIM] for r in range(Q_PER_KV)], axis=0) * scale
    s = _dot_nt(qs.astype(BF16), k.astype(BF16))
    sk = _sink_rows(sink_ref, g, Q_PER_KV * SEQ, SEQ)
    m = jnp.maximum(jnp.max(s, axis=-1, keepdims=True), sk)
    e = jnp.exp(s - m)
    den = jnp.sum(e, axis=-1, keepdims=True) + jnp.exp(sk - m)
    o = jnp.dot(e.astype(BF16), v.astype(BF16), preferred_element_type=F32) / den
    o_ref[...] = jnp.concatenate([o[r * SEQ:(r + 1) * SEQ] for r in range(Q_PER_KV)],
                                 axis=1).astype(o_ref.dtype)


def _attn_ctx(p, sink, layer, mix, kbuf, vbuf):
    kv_spec = pl.BlockSpec((None, None, SEQ, ATT_KV_HEADS, HEAD_DIM), lambda b, g: (b, layer, 0, 0, 0))
    kv_shape = jax.ShapeDtypeStruct((BATCH, DEPTH, SEQ, ATT_KV_HEADS, HEAD_DIM), F32)
    in_specs = [
        pl.BlockSpec(memory_space=pltpu.SMEM),
        pl.BlockSpec((SEQ, Q_PER_KV * HEAD_DIM), lambda b, g: (b, g)),
        pl.BlockSpec((SEQ, HEAD_DIM), lambda b, g: (b, COL_K + g)),
        pl.BlockSpec((SEQ, HEAD_DIM), lambda b, g: (b, COL_V + g)),
    ]
    in_specs += [ANY_SPEC] * 3
    args = [sink, p, p, p, mix, kbuf, vbuf]
    aliases = {4: 0, 5: 1, 6: 2}
    return pl.pallas_call(
        _attn_ctx_kernel,
        grid=(BATCH, ATT_KV_HEADS),
        in_specs=in_specs,
        out_specs=[pl.BlockSpec((SEQ, Q_PER_KV * HEAD_DIM), lambda b, g: (b, g)), kv_spec, kv_spec],
        out_shape=[jax.ShapeDtypeStruct((N_TOK, D_MODEL), BF16), kv_shape, kv_shape],
        input_output_aliases=aliases,
        compiler_params=_params(("arbitrary", "arbitrary")),
        name="attn_ctx",
    )(*args)


def _rope(x, c, sa, sb):
    return x * c + pltpu.roll(x, 96, 1) * sa + pltpu.roll(x, 32, 1) * sb


def _attn_lat_kernel(sink_ref, q_ref, k_ref, v_ref, kc_ref, vc_ref, cq_ref, sqa_ref, sqb_ref,
                     ck_ref, ska_ref, skb_ref, bias_ref, mix_ref, o_ref, kr_s, vb_s):
    del mix_ref
    g = pl.program_id(1)
    qb = pl.program_id(2)
    scale = HEAD_DIM ** -0.5
    band = 3 * ATT_BLOCK

    @pl.when(qb == 0)
    def _():
        kr_s[...] = _rope(k_ref[...], ck_ref[...], ska_ref[...], skb_ref[...]).astype(BF16)
        vb_s[...] = v_ref[...].astype(BF16)

    q = q_ref[...]
    cq, sqa, sqb = cq_ref[...], sqa_ref[...], sqb_ref[...]
    qs = jnp.concatenate(
        [_rope(q[:, r * HEAD_DIM:(r + 1) * HEAD_DIM], cq, sqa, sqb) * scale for r in range(Q_PER_KV)],
        axis=0).astype(BF16)
    rows = Q_PER_KV * ATT_BLOCK
    start = pl.multiple_of(jnp.clip(qb - 1, 0, DEC_SEQ // ATT_BLOCK - 3) * ATT_BLOCK, ATT_BLOCK)
    kb = kr_s[pl.ds(start, band), :]
    vb = vb_s[pl.ds(start, band), :]
    s_loc = _dot_nt(qs, kb) + bias_ref[...]
    s_ctx = _dot_nt(qs, kc_ref[...])
    sk = _sink_rows(sink_ref, g, rows, ATT_BLOCK)
    m = jnp.maximum(jnp.maximum(jnp.max(s_loc, axis=-1, keepdims=True),
                                jnp.max(s_ctx, axis=-1, keepdims=True)), sk)
    e_loc = jnp.exp(s_loc - m)
    e_ctx = jnp.exp(s_ctx - m)
    den = (jnp.sum(e_loc, axis=-1, keepdims=True) + jnp.sum(e_ctx, axis=-1, keepdims=True)
           + jnp.exp(sk - m))
    o = (jnp.dot(e_loc.astype(BF16), vb, preferred_element_type=F32)
         + jnp.dot(e_ctx.astype(BF16), vc_ref[...], preferred_element_type=F32)) / den
    o_ref[...] = jnp.concatenate([o[r * ATT_BLOCK:(r + 1) * ATT_BLOCK] for r in range(Q_PER_KV)],
                                 axis=1).astype(o_ref.dtype)


def _attn_lat(p, sink, kc, vc, layer, rope, mix):
    nqb = DEC_SEQ // ATT_BLOCK
    row0 = N_PROMPT // DEC_SEQ
    qrow0 = N_PROMPT // ATT_BLOCK
    tq = pl.BlockSpec((ATT_BLOCK, HEAD_DIM), lambda b, g, qb: (qb, 0))
    tk = pl.BlockSpec((DEC_SEQ, HEAD_DIM), lambda b, g, qb: (0, 0))
    ctx = pl.BlockSpec((None, None, None, PAST_LEN, HEAD_DIM), lambda b, g, qb: (layer, b, g, 0, 0))
    cos, sin_a, sin_b = rope
    bias = pl.BlockSpec((None, Q_PER_KV * ATT_BLOCK, 3 * ATT_BLOCK),
                        lambda b, g, qb: (jnp.where(qb == 0, 0, jnp.where(qb == nqb - 1, 2, 1)), 0, 0))
    return pl.pallas_call(
        _attn_lat_kernel,
        grid=(DEC_BATCH, ATT_KV_HEADS, nqb),
        in_specs=[
            pl.BlockSpec(memory_space=pltpu.SMEM),
            pl.BlockSpec((ATT_BLOCK, Q_PER_KV * HEAD_DIM), lambda b, g, qb: (qrow0 + b * nqb + qb, g)),
            pl.BlockSpec((DEC_SEQ, HEAD_DIM), lambda b, g, qb: (row0 + b, COL_K + g)),
            pl.BlockSpec((DEC_SEQ, HEAD_DIM), lambda b, g, qb: (row0 + b, COL_V + g)),
            ctx, ctx, tq, tq, tq, tk, tk, tk, bias, ANY_SPEC,
        ],
        out_specs=pl.BlockSpec((ATT_BLOCK, Q_PER_KV * HEAD_DIM),
                               lambda b, g, qb: (qrow0 + b * nqb + qb, g)),
        out_shape=jax.ShapeDtypeStruct((N_TOK, D_MODEL), BF16),
        scratch_shapes=[pltpu.VMEM((DEC_SEQ, HEAD_DIM), BF16), pltpu.VMEM((DEC_SEQ, HEAD_DIM), BF16)],
        input_output_aliases={13: 0},
        compiler_params=_params(("arbitrary", "arbitrary", "arbitrary")),
        name="attn_lat",
    )(sink, p, p, p, kc, vc, cos, sin_a, sin_b, cos, sin_a, sin_b, _band_bias(), mix)


def _band_bias():
    row = np.arange(Q_PER_KV * ATT_BLOCK)[:, None] % ATT_BLOCK
    lane = np.arange(3 * ATT_BLOCK)[None, :]
    cases = [lane - k * ATT_BLOCK - row for k in range(3)]
    return jnp.asarray(np.stack([np.where(np.abs(d) <= WINDOW, 0.0, -np.inf) for d in cases]).astype(np.float32))


def _rope_tables():
    rows = DEC_SEQ // GRID_W
    row = np.repeat(np.arange(rows), GRID_W).astype(np.float32)
    col = np.tile(np.arange(GRID_W), rows).astype(np.float32)
    axis_dim = HEAD_DIM // 2
    inv = (ROPE_BASE ** (-np.arange(0, axis_dim, 2, dtype=np.float32) / axis_dim)).astype(np.float32)
    lane = np.arange(HEAD_DIM)
    pos = np.where((lane // axis_dim)[None, :] == 0, row[:, None], col[:, None])
    ang = pos * inv[lane % (axis_dim // 2)][None, :]
    first = ((lane % axis_dim) < axis_dim // 2)[None, :]
    cos = np.cos(ang).astype(np.float32)
    sin = np.sin(ang).astype(np.float32)
    sin_a = np.where(first, -sin, 0.0).astype(np.float32)
    sin_b = np.where(first, 0.0, sin).astype(np.float32)
    return jnp.asarray(cos), jnp.asarray(sin_a), jnp.asarray(sin_b)


def _level_tables():
    t = np.arange(CHUNK)
    x = t[:, None] ^ t[None, :]
    lvl = np.where(x > 0, np.floor(np.log2(np.maximum(x, 1))), -1.0).astype(np.float32)
    tri = (t[:, None] >= t[None, :]).astype(np.float32)
    return jnp.asarray(lvl), jnp.asarray(np.stack([tri, tri.T]), dtype=BF16)


def _forget_gate(z, log_lb, log1m_lb):
    t = jnp.log(1.0 + jnp.exp(-jnp.abs(z)))
    c = log1m_lb + jnp.minimum(z, 0.0) - t
    log_f = jnp.maximum(log_lb, c) + jnp.log(1.0 + jnp.exp(-jnp.abs(log_lb - c)))
    return log_f, jnp.exp(log1m_lb - jnp.maximum(z, 0.0) - t)


def _running_log2_decay(g, tri):
    g1 = g.astype(BF16)
    g2 = (g - g1.astype(F32)).astype(BF16)
    dot = functools.partial(jnp.dot, preferred_element_type=F32)
    return (dot(tri, g1) + dot(tri, g2)) * LOG2E


def _intra_chunk_weights(qs, kfs, kbs, bfs, bbs, lvl):
    n = len(qs)
    acc = []
    for c in range(n):
        gram = _dot_nt(qs[c].astype(BF16), (kfs[c] + kbs[c]).astype(BF16))
        acc.append(jnp.where(lvl == -1.0, gram, 0.0))
    shape8 = (CHUNK // 8, 8, HG_DIM)
    sub = lax.broadcasted_iota(jnp.int32, shape8, 1)
    for j in range(N_LEVELS):
        half = 1 << j
        for c in range(n):
            q, kf, kb, bf, bb = qs[c], kfs[c], kbs[c], bfs[c], bbs[c]
            if j < 3:
                q, kf, kb, bf, bb = (a.reshape(shape8) for a in (q, kf, kb, bf, bb))
                upper = (sub & half) != 0
                if j == 0:
                    lhs = q * jnp.where(upper, 1.0 - kf, 1.0 - kb)
                    rhs = jnp.where(upper, kb, kf)
                else:
                    if j == 1:
                        ref_f = jnp.where(sub < 4, bf[:, 1:2, :], bf[:, 5:6, :])
                        ref_b = jnp.where(sub < 4, bb[:, 2:3, :], bb[:, 6:7, :])
                    else:
                        ref_f, ref_b = bf[:, 3:4, :], bb[:, 4:5, :]
                    lhs = q * jnp.exp2(jnp.where(upper, bf - ref_f, bb - ref_b))
                    rhs = jnp.where(upper, kb, kf) * jnp.exp2(jnp.where(upper, ref_b - bb, ref_f - bf))
            else:
                shape = (CHUNK // (2 * half), 2 * half, HG_DIM)
                q, kf, kb, bf, bb = (a.reshape(shape) for a in (q, kf, kb, bf, bb))
                ref_f, ref_b = bf[:, half - 1:half, :], bb[:, half:half + 1, :]
                lo, up = slice(0, half), slice(half, 2 * half)
                lhs = q * jnp.exp2(jnp.concatenate([bb[:, lo] - ref_b, bf[:, up] - ref_f], axis=1))
                rhs = (jnp.concatenate([kf[:, lo], kb[:, up]], axis=1)
                       * jnp.exp2(jnp.concatenate([ref_f - bf[:, lo], ref_b - bb[:, up]], axis=1)))
            gram = _dot_nt(lhs.reshape(CHUNK, HG_DIM).astype(BF16), rhs.reshape(CHUNK, HG_DIM).astype(BF16))
            acc[c] = jnp.where(lvl == float(j), gram, acc[c])
    return acc


HGRN_INTERLEAVE = 8


def _hgrn_kernel(*refs, seq_len, n_sub, has_init):
    refs = list(refs)
    hq_ref, hf_ref, hb_ref, hi_ref, hg_ref, lb_ref, gain_ref, tri_ref, lvl_ref = refs[:9]
    refs = refs[9:]
    init_ref = refs.pop(0) if has_init else None
    refs.pop(0)
    if not has_init:
        refs.pop(0)
    out_ref = refs.pop(0)
    fin_ref = None if has_init else refs.pop(0)
    q_s, vb_s, op_s, qdb_s, ub_s, dcb_s, sf_s = refs
    n_chunks = seq_len // CHUNK
    per = min(HGRN_INTERLEAVE // n_sub, n_chunks)
    slots = [(s, c) for s in range(n_sub) for c in range(per)]
    dot = functools.partial(jnp.dot, preferred_element_type=F32)

    q_s[...] = _silu(hq_ref[...])
    vb_s[...] = hi_ref[...].astype(BF16)

    def chunk_slice(s, ci):
        return pl.ds(pl.multiple_of(s * seq_len + ci * CHUNK, CHUNK), CHUNK)

    def fwd_step(i, carry):
        sls = [chunk_slice(s, per * i + c) for s, c in slots]
        lvl = lvl_ref[...]
        gf = [_forget_gate(hf_ref[sl, :], lb_ref[0:1, :], lb_ref[1:2, :]) for sl in sls]
        gb = [_forget_gate(hb_ref[sl, :], lb_ref[2:3, :], lb_ref[3:4, :]) for sl in sls]
        bfs = [_running_log2_decay(g, tri_ref[0]) for g, _ in gf]
        bbs = [_running_log2_decay(g, tri_ref[1]) for g, _ in gb]
        qs = [q_s[sl, :] for sl in sls]
        kfs = [k for _, k in gf]
        kbs = [k for _, k in gb]
        acc = _intra_chunk_weights(qs, kfs, kbs, bfs, bbs, lvl)
        for n, (s, c) in enumerate(slots):
            sl, q, bf, bb = sls[n], qs[n], bfs[n], bbs[n]
            ci = s * n_chunks + per * i + c
            vb = vb_s[sl, :]
            s_f = sf_s[s]
            qd = (q * jnp.exp2(bf)).astype(BF16)
            op_s[sl, :] = dot(jnp.concatenate([qd, acc[n].astype(BF16)], axis=1),
                              jnp.concatenate([s_f.astype(BF16), vb], axis=0))
            tot_f = bf[CHUNK - 1:CHUNK]
            kd = (kfs[n] * jnp.exp2(tot_f - bf)).astype(BF16)
            dcol = jnp.transpose(jnp.broadcast_to(jnp.exp2(tot_f), (HG_DIM, HG_DIM)))
            sf_s[s] = s_f * dcol + _dot_tn(kd, vb)
            tot_b = bb[0:1]
            qdb_s[sl, :] = (q * jnp.exp2(bb)).astype(BF16)
            kd = (kbs[n] * jnp.exp2(tot_b - bb)).astype(BF16)
            ub_s[ci] = _dot_tn(kd, vb)
            dcb_s[ci] = jnp.transpose(jnp.broadcast_to(jnp.exp2(tot_b), (HG_DIM, HG_DIM)))
        return carry

    zero = jnp.zeros((HG_DIM, HG_DIM), F32)
    for s in range(n_sub):
        sf_s[s] = init_ref[s, 0] if has_init else zero
    lax.fori_loop(0, n_chunks // per, fwd_step, 0)

    def bwd_step(i, states):
        states = list(states)
        outs = []
        for s, c in slots:
            ci = n_chunks - 1 - (per * i + c)
            sl = chunk_slice(s, ci)
            outs.append((sl, op_s[sl, :] + dot(qdb_s[sl, :], states[s].astype(BF16))))
            states[s] = states[s] * dcb_s[s * n_chunks + ci] + ub_s[s * n_chunks + ci]
        for sl, o in outs:
            o = o * lax.rsqrt(jnp.mean(o * o, axis=-1, keepdims=True) + NORM_EPS) * gain_ref[...]
            out_ref[sl, :] = (o * _silu(hg_ref[sl, :])).astype(out_ref.dtype)
        return tuple(states)

    s_b = lax.fori_loop(0, n_chunks // per, bwd_step,
                        tuple(init_ref[s, 1] if has_init else zero for s in range(n_sub)))
    if not has_init:
        for s in range(n_sub):
            fin_ref[s, 0] = sf_s[s]
            fin_ref[s, 1] = s_b[s]


def _hgrn(p, lb_tab, gain, tables, seq_len, n_seq, n_sub, row0, layer, mix, init=None, sbuf=None):
    lvl, tri = tables
    has_init = init is not None
    rows = n_sub * seq_len
    n_chunks = seq_len // CHUNK

    def col(c):
        return pl.BlockSpec((rows, HG_DIM), lambda b, h: (row0 + b, c + h))

    state_spec = pl.BlockSpec((n_sub, None, 2, None, HG_DIM, HG_DIM), lambda b, h: (b, layer, 0, h, 0, 0))
    in_specs = [col(COL_HQ), col(COL_HF), col(COL_HB), col(COL_HI), col(COL_HG),
                pl.BlockSpec((None, 4, HG_DIM), lambda b, h: (layer, 0, h)),
                pl.BlockSpec((1, HG_DIM), lambda b, h: (0, 0)),
                pl.BlockSpec((2, CHUNK, CHUNK), lambda b, h: (0, 0, 0)),
                pl.BlockSpec((CHUNK, CHUNK), lambda b, h: (0, 0))]
    args = [p, p, p, p, p, lb_tab, gain.reshape(1, HG_DIM), tri, lvl]
    if has_init:
        in_specs.append(state_spec)
        args.append(init)
    in_specs.append(ANY_SPEC)
    args.append(mix)
    aliases = {len(args) - 1: 0}
    out_specs = [pl.BlockSpec((rows, HG_DIM), lambda b, h: (row0 + b, MIX_HG + h))]
    out_shape = [jax.ShapeDtypeStruct((N_TOK, D_MODEL), BF16)]
    if not has_init:
        in_specs.append(ANY_SPEC)
        args.append(sbuf)
        aliases[len(args) - 1] = 1
        out_specs.append(state_spec)
        out_shape.append(jax.ShapeDtypeStruct((n_seq, DEPTH, 2, HG_HEADS, HG_DIM, HG_DIM), F32))

    def row_buf(dtype):
        return pltpu.VMEM((rows, HG_DIM), dtype)

    def state_buf(n):
        return pltpu.VMEM((n, HG_DIM, HG_DIM), F32)

    return pl.pallas_call(
        functools.partial(_hgrn_kernel, seq_len=seq_len, n_sub=n_sub, has_init=has_init),
        grid=(n_seq // n_sub, HG_HEADS),
        in_specs=in_specs,
        out_specs=out_specs,
        out_shape=out_shape,
        scratch_shapes=[row_buf(F32), row_buf(BF16), row_buf(F32), row_buf(BF16),
                        state_buf(n_sub * n_chunks), state_buf(n_sub * n_chunks), state_buf(n_sub)],
        input_output_aliases=aliases,
        compiler_params=_params(("arbitrary", "arbitrary")),
        name="hgrn_lat" if has_init else "hgrn_ctx",
    )(*args)


def _route(out, slab0):
    top_i = out[:, 0:2].astype(jnp.int32)
    gates = out[:, 2:4]
    e_flat = top_i.reshape(-1)
    onehot = (e_flat[:, None] == jnp.arange(N_EXPERTS)[None, :]).astype(jnp.int32)
    csum = jnp.cumsum(onehot, axis=0)
    counts = csum[-1]
    padded = ((counts + MOE_TM - 1) // MOE_TM) * MOE_TM
    ends = jnp.cumsum(padded)
    starts = ends - padded
    pos = jnp.sum(onehot * (csum - 1 + starts[None, :]), axis=1)
    row_token = jnp.zeros((MOE_ROWS,), jnp.int32).at[pos].set(jnp.arange(2 * N_TOK, dtype=jnp.int32) // 2)
    tile_start = jnp.arange(MOE_TILES, dtype=jnp.int32) * MOE_TM
    tile_expert = jnp.minimum(jnp.sum(tile_start[:, None] >= ends[None, :], axis=1), N_EXPERTS - 1)
    tile_valid = (tile_start < ends[-1]).astype(jnp.int32)
    return gates, pos.reshape(N_TOK, 2), row_token, (slab0 + tile_expert).astype(jnp.int32), tile_valid


def _rows(table, idx):
    return table.at[idx].get(mode="promise_in_bounds")


def kernel(x_prompt, x_sample, cache_k, cache_v, state_hgrn, c, c_ctx, w_mod, b_mod, norm_gains,
           w_in, w_out, attn_sink, hg_lb_logits, hg_norm_gain, ffn_w1, ffn_w3, ffn_w2,
           moe_router, moe_w1, moe_w3, moe_w2):
    x = jnp.concatenate([x_prompt.reshape(N_PROMPT, D_MODEL), x_sample.reshape(N_SAMPLE, D_MODEL)], axis=0)

    cvec = jnp.concatenate([c_ctx[None], c, jnp.zeros((3, D_MODEL), F32)], axis=0)
    mod = _modulation(cvec, w_mod, b_mod).reshape(DEPTH, 8, 6, D_MODEL)
    mod = jnp.pad(mod, ((0, 0), (0, 0), (0, 2), (0, 0)))
    group_src = np.array([0] * (N_PROMPT // MOD_GROUP) + list(range(1, 1 + DEC_BATCH)))
    mod = mod[:, group_src]

    lb_cum = jnp.cumsum(jax.nn.softmax(hg_lb_logits.astype(F32), axis=0), axis=0)
    lb = lb_cum - lb_cum[0:1]
    lb_tab = jnp.stack([jnp.log(lb[:, 0]), jnp.log1p(-lb[:, 0]),
                        jnp.log(lb[:, 1]), jnp.log1p(-lb[:, 1])], axis=1)

    rope = _rope_tables()
    tables = _level_tables()
    kc = cache_k.transpose(1, 0, 3, 2, 4).astype(BF16)
    vc = cache_v.transpose(1, 0, 3, 2, 4).astype(BF16)
    ffn_w = [ffn_w1, ffn_w3, ffn_w2]
    moe_w = [w.reshape((-1,) + w.shape[2:]) for w in (moe_w1, moe_w3, moe_w2)]

    (h,) = _resnorm(x, [], None, None, 0, norm_gains[0, 0], mod[0], 1, 0)
    mix = jnp.zeros((N_TOK, D_MODEL), BF16)
    kbuf = jnp.zeros((BATCH, DEPTH, SEQ, ATT_KV_HEADS, HEAD_DIM), F32)
    vbuf = jnp.zeros((BATCH, DEPTH, SEQ, ATT_KV_HEADS, HEAD_DIM), F32)
    sbuf = jnp.zeros((BATCH, DEPTH, 2, HG_HEADS, HG_DIM, HG_DIM), F32)
    for l in range(DEPTH):
        i = l // 2
        moe = l % 2 == 1
        p = _dense(h, [w_in], l, F32, "proj_in")
        mix, kbuf, vbuf = _attn_ctx(p, attn_sink[l], l, mix, kbuf, vbuf)
        mix = _attn_lat(p, attn_sink[l], kc, vc, l, rope, mix)
        mix, sbuf = _hgrn(p, lb_tab, hg_norm_gain[l], tables, SEQ, BATCH, 4, 0, l, mix, sbuf=sbuf)
        (mix,) = _hgrn(p, lb_tab, hg_norm_gain[l], tables, DEC_SEQ, DEC_BATCH, 1, N_PROMPT // DEC_SEQ,
                       l, mix, init=state_hgrn)
        m = _dense(mix, [w_out], l, F32, "proj_out")
        if not moe:
            x, h = _resnorm(x, [m], norm_gains[l, 1], mod[l], 2, norm_gains[l, 2], mod[l], 4, 3)
            act = _dense(h, ffn_w[:2], i, BF16, "ffn_up")
            f = [_dense(act, ffn_w[2:], i, F32, "ffn_down")]
            gates = None
        else:
            router = jnp.pad(moe_router[i], ((0, 0), (0, HEAD_DIM - N_EXPERTS))).astype(BF16)
            x, h_wide, table = _resnorm(x, [m], norm_gains[l, 1], mod[l], 2, norm_gains[l, 2], mod[l], 4, 3,
                                        router=router)
            gates, pos, row_token, tile_slab, tile_valid = _route(table, i * N_EXPERTS)
            xs = _rows(h_wide, row_token)
            act = _gmm(xs, moe_w[:2], tile_slab, tile_valid, BF16, "moe_up")
            ys = _gmm(act, moe_w[2:], tile_slab, tile_valid, F32, "moe_down")
            f = [_rows(ys, pos.T.reshape(-1)).reshape(2, N_TOK, D_MODEL)]
        if l + 1 < DEPTH:
            x, h = _resnorm(x, f, norm_gains[l, 3], mod[l], 5, norm_gains[l + 1, 0], mod[l + 1], 1, 0,
                            gates=gates)
        else:
            y_prompt, y_latent = _resnorm(x, f, norm_gains[l, 3], mod[l], 5, None, None, 0, 0, gates=gates,
                                          split=True)
    return (y_prompt.reshape(BATCH, SEQ, D_MODEL), y_latent.reshape(DEC_BATCH, DEC_SEQ, D_MODEL),
            kbuf, vbuf, sbuf)
```

```python
import functools

import numpy as np
import jax
import jax.numpy as jnp
from jax import lax
from jax.experimental import pallas as pl
from jax.experimental.pallas import tpu as pltpu

F32 = jnp.float32
BF16 = jnp.bfloat16

D_MODEL = 2048
BATCH = 16
SEQ = 256
DEPTH = 4
DEC_BATCH = 4
DEC_SEQ = 1024
PAST_LEN = 256
GRID_W = 64
HEAD_DIM = 128
ATT_HEADS = 8
ATT_KV_HEADS = 2
Q_PER_KV = 4
ATT_WIDTH = 1024
WINDOW = 128
ATT_BLOCK = 128
ROPE_BASE = 10000.0
HG_WIDTH = 1024
HG_HEADS = 8
HG_DIM = 128
N_EXPERTS = 8
D_FF = 5632
D_FF_EXPERT = 2816
NORM_EPS = 1e-6
IN_COLS = 6656

N_PROMPT = BATCH * SEQ
N_SAMPLE = DEC_BATCH * DEC_SEQ
N_TOK = N_PROMPT + N_SAMPLE
MOD_GROUP = 1024
N_GROUPS = N_TOK // MOD_GROUP

COL_K = 8
COL_V = 10
COL_HQ = 12
COL_HF = 20
COL_HB = 28
COL_HI = 36
COL_HG = 44
MIX_HG = ATT_WIDTH // HG_DIM

CHUNK = 128
N_LEVELS = 7
LOG2E = 1.4426950408889634

VMEM_LIMIT = 60 * 1024 * 1024

TILES = {
    "proj_in": (1024, 1664),
    "proj_out": (512, 2048),
    "ffn_up": (512, 1408),
    "ffn_down": (512, 1024),
    "moe_up": (256, 1408),
    "moe_down": (256, 2048),
}

RESNORM_TM = 512
MOE_TM = TILES["moe_up"][0]
MOE_ROWS = 2 * N_TOK + N_EXPERTS * MOE_TM
MOE_TILES = MOE_ROWS // MOE_TM

ANY_SPEC = pl.BlockSpec(memory_space=pl.ANY)


def _params(sem):
    return pltpu.CompilerParams(dimension_semantics=sem, vmem_limit_bytes=VMEM_LIMIT)


def _silu(x):
    return x / (1.0 + jnp.exp(-x))


def _dot_nt(a, b):
    return lax.dot_general(a, b, (((1,), (1,)), ((), ())), preferred_element_type=F32)


def _dot_tn(a, b):
    return lax.dot_general(a, b, (((0,), (0,)), ((), ())), preferred_element_type=F32)


def _mod_kernel(c_ref, w_ref, b_ref, o_ref):
    s = _silu(c_ref[...]).astype(BF16)
    o_ref[...] = jnp.dot(s, w_ref[...].astype(BF16), preferred_element_type=F32) + b_ref[...]


def _modulation(cvec, w_mod, b_mod):
    tn = 1024
    n = 6 * D_MODEL
    return pl.pallas_call(
        _mod_kernel,
        grid=(DEPTH, n // tn),
        in_specs=[
            pl.BlockSpec((8, D_MODEL), lambda l, j: (0, 0)),
            pl.BlockSpec((None, D_MODEL, tn), lambda l, j: (l, 0, j)),
            pl.BlockSpec((None, 1, tn), lambda l, j: (l, 0, j)),
        ],
        out_specs=pl.BlockSpec((None, 8, tn), lambda l, j: (l, 0, j)),
        out_shape=jax.ShapeDtypeStruct((DEPTH, 8, n), F32),
        compiler_params=_params(("arbitrary", "arbitrary")),
        name="modulation",
    )(cvec, w_mod, b_mod.reshape(DEPTH, 1, n))


def _rms(x, gain):
    return x * lax.rsqrt(jnp.mean(x * x, axis=-1, keepdims=True) + NORM_EPS) * gain


def _top2_gates(lg):
    lane = lax.broadcasted_iota(jnp.int32, lg.shape, 1)
    lg = jnp.where(lane < N_EXPERTS, lg, -jnp.inf)
    m1 = jnp.max(lg, axis=-1, keepdims=True)
    i1 = jnp.min(jnp.where(lg == m1, lane, HEAD_DIM), axis=-1, keepdims=True)
    lg2 = jnp.where(lane == i1, -jnp.inf, lg)
    m2 = jnp.max(lg2, axis=-1, keepdims=True)
    i2 = jnp.min(jnp.where(lg2 == m2, lane, HEAD_DIM), axis=-1, keepdims=True)
    e2 = jnp.exp(m2 - m1)
    den = 1.0 + e2
    return jnp.where(lane == 0, i1.astype(F32),
                     jnp.where(lane == 1, i2.astype(F32),
                               jnp.where(lane == 2, 1.0 / den,
                                         jnp.where(lane == 3, e2 / den, 0.0))))


def _resnorm_kernel(*refs, n_m, gate_row, has_h, sc_row, sh_row, routed, split_tiles, joined):
    refs = list(refs)
    if joined:
        prompt_in, latent_in = refs.pop(0), refs.pop(0)
        x = jnp.where(pl.program_id(0) < joined, prompt_in[...], latent_in[...])
    else:
        x = refs.pop(0)[...]
    if n_m == 1:
        m = refs.pop(0)[...]
    elif n_m == 2:
        pair = refs.pop(0)
        gt = refs.pop(0)[...]
        m = gt[:, 0:1] * pair[0] + gt[:, 1:2] * pair[1]
    if n_m:
        ga = refs.pop(0)[...]
        modg = refs.pop(0)
        x = x + modg[gate_row:gate_row + 1, :] * _rms(m, ga)
    if has_h:
        gb = refs.pop(0)[...]
        modh = refs.pop(0)
        h = _rms(x, gb) * (1.0 + modh[sc_row:sc_row + 1, :]) + modh[sh_row:sh_row + 1, :]
    if routed:
        router = refs.pop(0)[...]
    if split_tiles:
        prompt_ref, latent_ref = refs.pop(0), refs.pop(0)

        @pl.when(pl.program_id(0) < split_tiles)
        def _():
            prompt_ref[...] = x

        @pl.when(pl.program_id(0) >= split_tiles)
        def _():
            latent_ref[...] = x
    elif n_m or joined:
        refs.pop(0)[...] = x
    if routed:
        refs.pop(0)[...] = h
        refs.pop(0)[...] = _top2_gates(jnp.dot(h.astype(BF16), router, preferred_element_type=F32))
    elif has_h:
        refs.pop(0)[...] = h.astype(BF16)


def _resnorm(x, ms, gain_m, mod_gate, gate_row, gain_h, mod_h, sc_row, sh_row, gates=None, router=None,
             split=False):
    tm = RESNORM_TM
    per = MOD_GROUP // tm
    n_m = len(ms) if gates is None else 2
    has_h = gain_h is not None
    routed = router is not None
    half_tiles = N_PROMPT // tm
    joined = half_tiles if isinstance(x, tuple) else 0
    row = pl.BlockSpec((tm, D_MODEL), lambda i: (i, 0))
    prompt_rows = pl.BlockSpec((tm, D_MODEL), lambda i: (jnp.minimum(i, half_tiles - 1), 0))
    latent_rows = pl.BlockSpec((tm, D_MODEL), lambda i: (jnp.maximum(i - half_tiles, 0), 0))
    vec = pl.BlockSpec((1, D_MODEL), lambda i: (0, 0))
    mod = pl.BlockSpec((None, 8, D_MODEL), lambda i: (i // per, 0, 0))
    args, specs = ([*x], [prompt_rows, latent_rows]) if joined else ([x], [row])
    if n_m == 1:
        args.append(ms[0])
        specs.append(row)
    elif n_m == 2:
        args += [ms[0], gates]
        specs += [pl.BlockSpec((2, tm, D_MODEL), lambda i: (0, i, 0)), pl.BlockSpec((tm, 2), lambda i: (i, 0))]
    if n_m:
        args += [gain_m.reshape(1, D_MODEL), mod_gate]
        specs += [vec, mod]
    if has_h:
        args += [gain_h.reshape(1, D_MODEL), mod_h]
        specs += [vec, mod]
    if routed:
        args.append(router)
        specs.append(pl.BlockSpec((D_MODEL, HEAD_DIM), lambda i: (0, 0)))
    out_shape, out_specs = [], []
    split_tiles = half_tiles if split else 0
    if split:
        out_shape += [jax.ShapeDtypeStruct((N_PROMPT, D_MODEL), F32), jax.ShapeDtypeStruct((N_SAMPLE, D_MODEL), F32)]
        out_specs += [prompt_rows, latent_rows]
    elif n_m or joined:
        out_shape.append(jax.ShapeDtypeStruct((N_TOK, D_MODEL), F32))
        out_specs.append(row)
    if routed:
        out_shape += [jax.ShapeDtypeStruct((N_TOK, D_MODEL), F32), jax.ShapeDtypeStruct((N_TOK, HEAD_DIM), F32)]
        out_specs += [row, pl.BlockSpec((tm, HEAD_DIM), lambda i: (i, 0))]
    elif has_h:
        out_shape.append(jax.ShapeDtypeStruct((N_TOK, D_MODEL), BF16))
        out_specs.append(row)
    outs = pl.pallas_call(
        functools.partial(_resnorm_kernel, n_m=n_m, gate_row=gate_row, has_h=has_h,
                          sc_row=sc_row, sh_row=sh_row, routed=routed, split_tiles=split_tiles,
                          joined=joined),
        grid=(N_TOK // tm,),
        in_specs=specs,
        out_specs=out_specs,
        out_shape=out_shape,
        compiler_params=_params(("arbitrary",)),
        name="resnorm",
    )(*args)
    return outs


CAST_ROWS = 256


def _gmm_kernel(load_ref, slab_ref, col_ref, more_ref, nslab_ref, ncol_ref, valid_ref, x_ref, *rest,
                nw, tn, k_rows):
    w_hbm = rest[:nw]
    o_ref = rest[nw]
    stage = rest[nw + 1:2 * nw + 1]
    wb = rest[2 * nw + 1:3 * nw + 1]
    sem = rest[3 * nw + 1]
    t = pl.program_id(0) * pl.num_programs(1) + pl.program_id(1)

    def copies(slab, col):
        cols = pl.ds(pl.multiple_of(col * tn, 128), tn)
        return [pltpu.make_async_copy(w_hbm[k].at[slab, :, cols], stage[k], sem.at[k]) for k in range(nw)]

    @pl.when(t == 0)
    def _():
        for cp in copies(slab_ref[0], col_ref[0]):
            cp.start()

    @pl.when(load_ref[t] != 0)
    def _():
        for cp in copies(slab_ref[t], col_ref[t]):
            cp.wait()

        def cast(r, carry):
            rows = pl.ds(pl.multiple_of(r * CAST_ROWS, CAST_ROWS), CAST_ROWS)
            for k in range(nw):
                wb[k][rows, :] = stage[k][rows, :].astype(BF16)
            return carry

        lax.fori_loop(0, k_rows // CAST_ROWS, cast, 0)

        @pl.when(more_ref[t] != 0)
        def _():
            for cp in copies(nslab_ref[t], ncol_ref[t]):
                cp.start()

    valid = valid_ref[pl.program_id(1)] != 0

    @pl.when(valid)
    def _():
        x = x_ref[...].astype(BF16)
        a = jnp.dot(x, wb[0][...], preferred_element_type=F32)
        if nw == 2:
            a = _silu(a) * jnp.dot(x, wb[1][...], preferred_element_type=F32)
        o_ref[...] = a.astype(o_ref.dtype)

    @pl.when(jnp.logical_not(valid))
    def _():
        o_ref[...] = jnp.zeros_like(o_ref)


def _weight_schedule(tile_slab, tile_valid, n_col):
    n_row = tile_slab.shape[0]
    idx = jnp.arange(n_row, dtype=jnp.int32)
    keep = lax.cummax(jnp.where(tile_valid != 0, idx, 0), axis=0)
    slab = jnp.tile(tile_slab[keep], n_col)
    col = jnp.repeat(jnp.arange(n_col, dtype=jnp.int32), n_row)
    steps = n_row * n_col
    key = slab * n_col + col
    load = jnp.concatenate([jnp.ones((1,), bool), key[1:] != key[:-1]])
    load_at = jnp.where(load, jnp.arange(steps, dtype=jnp.int32), steps)
    first_from = lax.cummin(load_at, axis=0, reverse=True)
    nxt = jnp.concatenate([first_from[1:], jnp.full((1,), steps, jnp.int32)])
    more = nxt < steps
    nxt = jnp.minimum(nxt, steps - 1)
    return (load.astype(jnp.int32), slab.astype(jnp.int32), col, more.astype(jnp.int32),
            slab[nxt].astype(jnp.int32), col[nxt])


def _gmm(x, ws, tile_slab, tile_valid, out_dtype, name):
    tm, tn = TILES[name]
    m, k = x.shape
    n = ws[0].shape[-1]
    nw = len(ws)
    sched = _weight_schedule(tile_slab, tile_valid, n // tn)
    return pl.pallas_call(
        functools.partial(_gmm_kernel, nw=nw, tn=tn, k_rows=k),
        grid_spec=pltpu.PrefetchScalarGridSpec(
            num_scalar_prefetch=7,
            grid=(n // tn, m // tm),
            in_specs=[pl.BlockSpec((tm, k), lambda j, i, *_: (i, 0))] + [ANY_SPEC] * nw,
            out_specs=pl.BlockSpec((tm, tn), lambda j, i, *_: (i, j)),
            scratch_shapes=([pltpu.VMEM((k, tn), F32)] * nw + [pltpu.VMEM((k, tn), BF16)] * nw
                            + [pltpu.SemaphoreType.DMA((nw,))]),
        ),
        out_shape=jax.ShapeDtypeStruct((m, n), out_dtype),
        compiler_params=_params(("arbitrary", "arbitrary")),
        name=name,
    )(*sched, tile_valid, x, *ws)


def _dense(x, ws, slab, out_dtype, name):
    tiles = x.shape[0] // TILES[name][0]
    return _gmm(x, ws, jnp.full((tiles,), slab, jnp.int32), jnp.ones((tiles,), jnp.int32),
                out_dtype, name)


def _sink_rows(sink_ref, g, rows, per):
    r = lax.broadcasted_iota(jnp.int32, (rows, 1), 0)
    sk = jnp.full((rows, 1), sink_ref[g * Q_PER_KV + Q_PER_KV - 1], F32)
    for h in range(Q_PER_KV - 2, -1, -1):
        sk = jnp.where(r < (h + 1) * per, sink_ref[g * Q_PER_KV + h], sk)
    return sk


def _attn_ctx_kernel(*refs):
    sink_ref, q_ref, k_ref, v_ref = refs[:4]
    o_ref, ko_ref, vo_ref = refs[-3:]
    g = pl.program_id(1)
    scale = HEAD_DIM ** -0.5
    k = k_ref[...]
    v = v_ref[...]
    for head in range(ATT_KV_HEADS):
        @pl.when(g == head)
        def _(head=head):
            ko_ref[:, head, :] = k
            vo_ref[:, head, :] = v
    q = q_ref[...]
    qs = jnp.concatenate([q[:, r * HEAD_DIM:(r + 1) * HEAD_DIM] for r in range(Q_PER_KV)], axis=0) * scale
    s = _dot_nt(qs.astype(BF16), k.astype(BF16))
    sk = _sink_rows(sink_ref, g, Q_PER_KV * SEQ, SEQ)
    m = jnp.maximum(jnp.max(s, axis=-1, keepdims=True), sk)
    e = jnp.exp(s - m)
    den = jnp.sum(e, axis=-1, keepdims=True) + jnp.exp(sk - m)
    o = jnp.dot(e.astype(BF16), v.astype(BF16), preferred_element_type=F32) / den
    o_ref[...] = jnp.concatenate([o[r * SEQ:(r + 1) * SEQ] for r in range(Q_PER_KV)],
                                 axis=1).astype(o_ref.dtype)


def _attn_ctx(p, sink, layer, mix, kbuf, vbuf):
    kv_spec = pl.BlockSpec((None, None, SEQ, ATT_KV_HEADS, HEAD_DIM), lambda b, g: (b, layer, 0, 0, 0))
    kv_shape = jax.ShapeDtypeStruct((BATCH, DEPTH, SEQ, ATT_KV_HEADS, HEAD_DIM), F32)
    in_specs = [
        pl.BlockSpec(memory_space=pltpu.SMEM),
        pl.BlockSpec((SEQ, Q_PER_KV * HEAD_DIM), lambda b, g: (b, g)),
        pl.BlockSpec((SEQ, HEAD_DIM), lambda b, g: (b, COL_K + g)),
        pl.BlockSpec((SEQ, HEAD_DIM), lambda b, g: (b, COL_V + g)),
    ]
    in_specs += [ANY_SPEC] * 3
    args = [sink, p, p, p, mix, kbuf, vbuf]
    aliases = {4: 0, 5: 1, 6: 2}
    return pl.pallas_call(
        _attn_ctx_kernel,
        grid=(BATCH, ATT_KV_HEADS),
        in_specs=in_specs,
        out_specs=[pl.BlockSpec((SEQ, Q_PER_KV * HEAD_DIM), lambda b, g: (b, g)), kv_spec, kv_spec],
        out_shape=[jax.ShapeDtypeStruct((N_TOK, D_MODEL), BF16), kv_shape, kv_shape],
        input_output_aliases=aliases,
        compiler_params=_params(("arbitrary", "arbitrary")),
        name="attn_ctx",
    )(*args)


def _rope(x, c, sa, sb):
    return x * c + pltpu.roll(x, 96, 1) * sa + pltpu.roll(x, 32, 1) * sb


def _attn_lat_kernel(sink_ref, q_ref, k_ref, v_ref, kc_ref, vc_ref, cq_ref, sqa_ref, sqb_ref,
                     ck_ref, ska_ref, skb_ref, bias_ref, mix_ref, o_ref, kr_s, vb_s):
    del mix_ref
    g = pl.program_id(1)
    qb = pl.program_id(2)
    scale = HEAD_DIM ** -0.5
    band = 3 * ATT_BLOCK

    @pl.when(qb == 0)
    def _():
        kr_s[...] = _rope(k_ref[...], ck_ref[...], ska_ref[...], skb_ref[...]).astype(BF16)
        vb_s[...] = v_ref[...].astype(BF16)

    q = q_ref[...]
    cq, sqa, sqb = cq_ref[...], sqa_ref[...], sqb_ref[...]
    qs = jnp.concatenate(
        [_rope(q[:, r * HEAD_DIM:(r + 1) * HEAD_DIM], cq, sqa, sqb) * scale for r in range(Q_PER_KV)],
        axis=0).astype(BF16)
    rows = Q_PER_KV * ATT_BLOCK
    start = pl.multiple_of(jnp.clip(qb - 1, 0, DEC_SEQ // ATT_BLOCK - 3) * ATT_BLOCK, ATT_BLOCK)
    kb = kr_s[pl.ds(start, band), :]
    vb = vb_s[pl.ds(start, band), :]
    s_loc = _dot_nt(qs, kb) + bias_ref[...]
    s_ctx = _dot_nt(qs, kc_ref[...])
    sk = _sink_rows(sink_ref, g, rows, ATT_BLOCK)
    m = jnp.maximum(jnp.maximum(jnp.max(s_loc, axis=-1, keepdims=True),
                                jnp.max(s_ctx, axis=-1, keepdims=True)), sk)
    e_loc = jnp.exp(s_loc - m)
    e_ctx = jnp.exp(s_ctx - m)
    den = (jnp.sum(e_loc, axis=-1, keepdims=True) + jnp.sum(e_ctx, axis=-1, keepdims=True)
           + jnp.exp(sk - m))
    o = (jnp.dot(e_loc.astype(BF16), vb, preferred_element_type=F32)
         + jnp.dot(e_ctx.astype(BF16), vc_ref[...], preferred_element_type=F32)) / den
    o_ref[...] = jnp.concatenate([o[r * ATT_BLOCK:(r + 1) * ATT_BLOCK] for r in range(Q_PER_KV)],
                                 axis=1).astype(o_ref.dtype)


def _attn_lat(p, sink, kc, vc, layer, rope, mix):
    nqb = DEC_SEQ // ATT_BLOCK
    row0 = N_PROMPT // DEC_SEQ
    qrow0 = N_PROMPT // ATT_BLOCK
    tq = pl.BlockSpec((ATT_BLOCK, HEAD_DIM), lambda b, g, qb: (qb, 0))
    tk = pl.BlockSpec((DEC_SEQ, HEAD_DIM), lambda b, g, qb: (0, 0))
    ctx = pl.BlockSpec((None, None, None, PAST_LEN, HEAD_DIM), lambda b, g, qb: (layer, b, g, 0, 0))
    cos, sin_a, sin_b = rope
    bias = pl.BlockSpec((None, Q_PER_KV * ATT_BLOCK, 3 * ATT_BLOCK),
                        lambda b, g, qb: (jnp.where(qb == 0, 0, jnp.where(qb == nqb - 1, 2, 1)), 0, 0))
    return pl.pallas_call(
        _attn_lat_kernel,
        grid=(DEC_BATCH, ATT_KV_HEADS, nqb),
        in_specs=[
            pl.BlockSpec(memory_space=pltpu.SMEM),
            pl.BlockSpec((ATT_BLOCK, Q_PER_KV * HEAD_DIM), lambda b, g, qb: (qrow0 + b * nqb + qb, g)),
            pl.BlockSpec((DEC_SEQ, HEAD_DIM), lambda b, g, qb: (row0 + b, COL_K + g)),
            pl.BlockSpec((DEC_SEQ, HEAD_DIM), lambda b, g, qb: (row0 + b, COL_V + g)),
            ctx, ctx, tq, tq, tq, tk, tk, tk, bias, ANY_SPEC,
        ],
        out_specs=pl.BlockSpec((ATT_BLOCK, Q_PER_KV * HEAD_DIM),
                               lambda b, g, qb: (qrow0 + b * nqb + qb, g)),
        out_shape=jax.ShapeDtypeStruct((N_TOK, D_MODEL), BF16),
        scratch_shapes=[pltpu.VMEM((DEC_SEQ, HEAD_DIM), BF16), pltpu.VMEM((DEC_SEQ, HEAD_DIM), BF16)],
        input_output_aliases={13: 0},
        compiler_params=_params(("arbitrary", "arbitrary", "arbitrary")),
        name="attn_lat",
    )(sink, p, p, p, kc, vc, cos, sin_a, sin_b, cos, sin_a, sin_b, _band_bias(), mix)


def _band_bias():
    row = np.arange(Q_PER_KV * ATT_BLOCK)[:, None] % ATT_BLOCK
    lane = np.arange(3 * ATT_BLOCK)[None, :]
    cases = [lane - k * ATT_BLOCK - row for k in range(3)]
    return jnp.asarray(np.stack([np.where(np.abs(d) <= WINDOW, 0.0, -np.inf) for d in cases]).astype(np.float32))


def _rope_tables():
    rows = DEC_SEQ // GRID_W
    row = np.repeat(np.arange(rows), GRID_W).astype(np.float32)
    col = np.tile(np.arange(GRID_W), rows).astype(np.float32)
    axis_dim = HEAD_DIM // 2
    inv = (ROPE_BASE ** (-np.arange(0, axis_dim, 2, dtype=np.float32) / axis_dim)).astype(np.float32)
    lane = np.arange(HEAD_DIM)
    pos = np.where((lane // axis_dim)[None, :] == 0, row[:, None], col[:, None])
    ang = pos * inv[lane % (axis_dim // 2)][None, :]
    first = ((lane % axis_dim) < axis_dim // 2)[None, :]
    cos = np.cos(ang).astype(np.float32)
    sin = np.sin(ang).astype(np.float32)
    sin_a = np.where(first, -sin, 0.0).astype(np.float32)
    sin_b = np.where(first, 0.0, sin).astype(np.float32)
    return jnp.asarray(cos), jnp.asarray(sin_a), jnp.asarray(sin_b)


def _level_tables():
    t = np.arange(CHUNK)
    x = t[:, None] ^ t[None, :]
    lvl = np.where(x > 0, np.floor(np.log2(np.maximum(x, 1))), -1.0).astype(np.float32)
    tri = (t[:, None] >= t[None, :]).astype(np.float32)
    return jnp.asarray(lvl), jnp.asarray(np.stack([tri, tri.T]), dtype=BF16)


def _forget_gate(z, lb, one_m_lb, log1m_lb):
    e = jnp.exp(-jnp.abs(z))
    r = 1.0 / (1.0 + e)
    er = e * r
    up = z >= 0.0
    f = lb + one_m_lb * jnp.where(up, r, er)
    log_f = jnp.where(f > 0.0, jnp.log(f), log1m_lb + z)
    return log_f, one_m_lb * jnp.where(up, er, r)


def _running_log2_decay(g, tri):
    g1 = g.astype(BF16)
    g2 = (g - g1.astype(F32)).astype(BF16)
    dot = functools.partial(jnp.dot, preferred_element_type=F32)
    return (dot(tri, g1) + dot(tri, g2)) * LOG2E


def _intra_chunk_weights(qs, kfs, kbs, bfs, bbs, lvl):
    n = len(qs)
    acc = []
    for c in range(n):
        gram = _dot_nt(qs[c].astype(BF16), (kfs[c] + kbs[c]).astype(BF16))
        acc.append(jnp.where(lvl == -1.0, gram, 0.0))
    shape8 = (CHUNK // 8, 8, HG_DIM)
    sub = lax.broadcasted_iota(jnp.int32, shape8, 1)
    for j in range(N_LEVELS):
        half = 1 << j
        for c in range(n):
            q, kf, kb, bf, bb = qs[c], kfs[c], kbs[c], bfs[c], bbs[c]
            if j < 3:
                q, kf, kb, bf, bb = (a.reshape(shape8) for a in (q, kf, kb, bf, bb))
                upper = (sub & half) != 0
                if j == 0:
                    lhs = q * jnp.where(upper, 1.0 - kf, 1.0 - kb)
                    rhs = jnp.where(upper, kb, kf)
                else:
                    if j == 1:
                        ref_f = jnp.where(sub < 4, bf[:, 1:2, :], bf[:, 5:6, :])
                        ref_b = jnp.where(sub < 4, bb[:, 2:3, :], bb[:, 6:7, :])
                    else:
                        ref_f, ref_b = bf[:, 3:4, :], bb[:, 4:5, :]
                    lhs = q * jnp.exp2(jnp.where(upper, bf - ref_f, bb - ref_b))
                    rhs = jnp.where(upper, kb, kf) * jnp.exp2(jnp.where(upper, ref_b - bb, ref_f - bf))
            else:
                shape = (CHUNK // (2 * half), 2 * half, HG_DIM)
                q, kf, kb, bf, bb = (a.reshape(shape) for a in (q, kf, kb, bf, bb))
                ref_f, ref_b = bf[:, half - 1:half, :], bb[:, half:half + 1, :]
                lo, up = slice(0, half), slice(half, 2 * half)
                lhs = q * jnp.exp2(jnp.concatenate([bb[:, lo] - ref_b, bf[:, up] - ref_f], axis=1))
                rhs = (jnp.concatenate([kf[:, lo], kb[:, up]], axis=1)
                       * jnp.exp2(jnp.concatenate([ref_f - bf[:, lo], ref_b - bb[:, up]], axis=1)))
            gram = _dot_nt(lhs.reshape(CHUNK, HG_DIM).astype(BF16), rhs.reshape(CHUNK, HG_DIM).astype(BF16))
            acc[c] = jnp.where(lvl == float(j), gram, acc[c])
    return acc


HGRN_INTERLEAVE = 8


def _hgrn_kernel(*refs, seq_len, n_sub, has_init):
    refs = list(refs)
    hq_ref, hf_ref, hb_ref, hi_ref, hg_ref, lb_ref, gain_ref, tri_ref, lvl_ref = refs[:9]
    refs = refs[9:]
    init_ref = refs.pop(0) if has_init else None
    refs.pop(0)
    if not has_init:
        refs.pop(0)
    out_ref = refs.pop(0)
    fin_ref = None if has_init else refs.pop(0)
    op_s, qdb_s, ub_s, dcb_s, sf_s = refs
    n_chunks = seq_len // CHUNK
    per = min(HGRN_INTERLEAVE // n_sub, n_chunks)
    slots = [(s, c) for s in range(n_sub) for c in range(per)]
    dot = functools.partial(jnp.dot, preferred_element_type=F32)

    def chunk_slice(s, ci):
        return pl.ds(pl.multiple_of(s * seq_len + ci * CHUNK, CHUNK), CHUNK)

    def fwd_step(i, carry):
        sls = [chunk_slice(s, per * i + c) for s, c in slots]
        lvl = lvl_ref[...]
        gf = [_forget_gate(hf_ref[sl, :], lb_ref[0:1, :], lb_ref[1:2, :], lb_ref[2:3, :]) for sl in sls]
        gb = [_forget_gate(hb_ref[sl, :], lb_ref[3:4, :], lb_ref[4:5, :], lb_ref[5:6, :]) for sl in sls]
        bfs = [_running_log2_decay(g, tri_ref[0]) for g, _ in gf]
        bbs = [_running_log2_decay(g, tri_ref[1]) for g, _ in gb]
        qs = [_silu(hq_ref[sl, :]) for sl in sls]
        kfs = [k for _, k in gf]
        kbs = [k for _, k in gb]
        acc = _intra_chunk_weights(qs, kfs, kbs, bfs, bbs, lvl)
        for n, (s, c) in enumerate(slots):
            sl, q, bf, bb = sls[n], qs[n], bfs[n], bbs[n]
            ci = s * n_chunks + per * i + c
            vb = hi_ref[sl, :].astype(BF16)
            s_f = sf_s[s]
            qd = (q * jnp.exp2(bf)).astype(BF16)
            op_s[sl, :] = dot(jnp.concatenate([qd, acc[n].astype(BF16)], axis=1),
                              jnp.concatenate([s_f.astype(BF16), vb], axis=0))
            tot_f = bf[CHUNK - 1:CHUNK]
            kd = (kfs[n] * jnp.exp2(tot_f - bf)).astype(BF16)
            dcol = jnp.transpose(jnp.broadcast_to(jnp.exp2(tot_f), (HG_DIM, HG_DIM)))
            sf_s[s] = s_f * dcol + _dot_tn(kd, vb)
            tot_b = bb[0:1]
            qdb_s[sl, :] = (q * jnp.exp2(bb)).astype(BF16)
            kd = (kbs[n] * jnp.exp2(tot_b - bb)).astype(BF16)
            ub_s[ci] = _dot_tn(kd, vb)
            dcb_s[ci] = jnp.transpose(jnp.broadcast_to(jnp.exp2(tot_b), (HG_DIM, HG_DIM)))
        return carry

    zero = jnp.zeros((HG_DIM, HG_DIM), F32)
    for s in range(n_sub):
        sf_s[s] = init_ref[s, 0] if has_init else zero
    lax.fori_loop(0, n_chunks // per, fwd_step, 0)

    def bwd_step(i, states):
        states = list(states)
        outs = []
        for s, c in slots:
            ci = n_chunks - 1 - (per * i + c)
            sl = chunk_slice(s, ci)
            outs.append((sl, op_s[sl, :] + dot(qdb_s[sl, :], states[s].astype(BF16))))
            states[s] = states[s] * dcb_s[s * n_chunks + ci] + ub_s[s * n_chunks + ci]
        for sl, o in outs:
            o = o * lax.rsqrt(jnp.mean(o * o, axis=-1, keepdims=True) + NORM_EPS) * gain_ref[...]
            out_ref[sl, :] = (o * _silu(hg_ref[sl, :])).astype(out_ref.dtype)
        return tuple(states)

    s_b = lax.fori_loop(0, n_chunks // per, bwd_step,
                        tuple(init_ref[s, 1] if has_init else zero for s in range(n_sub)))
    if not has_init:
        for s in range(n_sub):
            fin_ref[s, 0] = sf_s[s]
            fin_ref[s, 1] = s_b[s]


def _hgrn(p, lb_tab, gain, tables, seq_len, n_seq, n_sub, row0, layer, mix, init=None, sbuf=None):
    lvl, tri = tables
    has_init = init is not None
    rows = n_sub * seq_len
    n_chunks = seq_len // CHUNK

    def col(c):
        return pl.BlockSpec((rows, HG_DIM), lambda b, h: (row0 + b, c + h))

    state_spec = pl.BlockSpec((n_sub, None, 2, None, HG_DIM, HG_DIM), lambda b, h: (b, layer, 0, h, 0, 0))
    in_specs = [col(COL_HQ), col(COL_HF), col(COL_HB), col(COL_HI), col(COL_HG),
                pl.BlockSpec((None, 6, HG_DIM), lambda b, h: (layer, 0, h)),
                pl.BlockSpec((1, HG_DIM), lambda b, h: (0, 0)),
                pl.BlockSpec((2, CHUNK, CHUNK), lambda b, h: (0, 0, 0)),
                pl.BlockSpec((CHUNK, CHUNK), lambda b, h: (0, 0))]
    args = [p, p, p, p, p, lb_tab, gain.reshape(1, HG_DIM), tri, lvl]
    if has_init:
        in_specs.append(state_spec)
        args.append(init)
    in_specs.append(ANY_SPEC)
    args.append(mix)
    aliases = {len(args) - 1: 0}
    out_specs = [pl.BlockSpec((rows, HG_DIM), lambda b, h: (row0 + b, MIX_HG + h))]
    out_shape = [jax.ShapeDtypeStruct((N_TOK, D_MODEL), BF16)]
    if not has_init:
        in_specs.append(ANY_SPEC)
        args.append(sbuf)
        aliases[len(args) - 1] = 1
        out_specs.append(state_spec)
        out_shape.append(jax.ShapeDtypeStruct((n_seq, DEPTH, 2, HG_HEADS, HG_DIM, HG_DIM), F32))

    def row_buf(dtype):
        return pltpu.VMEM((rows, HG_DIM), dtype)

    def state_buf(n):
        return pltpu.VMEM((n, HG_DIM, HG_DIM), F32)

    return pl.pallas_call(
        functools.partial(_hgrn_kernel, seq_len=seq_len, n_sub=n_sub, has_init=has_init),
        grid=(n_seq // n_sub, HG_HEADS),
        in_specs=in_specs,
        out_specs=out_specs,
        out_shape=out_shape,
        scratch_shapes=[row_buf(F32), row_buf(BF16),
                        state_buf(n_sub * n_chunks), state_buf(n_sub * n_chunks), state_buf(n_sub)],
        input_output_aliases=aliases,
        compiler_params=_params(("arbitrary", "arbitrary")),
        name="hgrn_lat" if has_init else "hgrn_ctx",
    )(*args)


def _route(out, slab0):
    top_i = out[:, 0:2].astype(jnp.int32)
    gates = out[:, 2:4]
    e_flat = top_i.reshape(-1)
    onehot = (e_flat[:, None] == jnp.arange(N_EXPERTS)[None, :]).astype(jnp.int32)
    csum = jnp.cumsum(onehot, axis=0)
    counts = csum[-1]
    padded = ((counts + MOE_TM - 1) // MOE_TM) * MOE_TM
    ends = jnp.cumsum(padded)
    starts = ends - padded
    pos = jnp.sum(onehot * (csum - 1 + starts[None, :]), axis=1)
    row_token = jnp.zeros((MOE_ROWS,), jnp.int32).at[pos].set(jnp.arange(2 * N_TOK, dtype=jnp.int32) // 2)
    tile_start = jnp.arange(MOE_TILES, dtype=jnp.int32) * MOE_TM
    tile_expert = jnp.minimum(jnp.sum(tile_start[:, None] >= ends[None, :], axis=1), N_EXPERTS - 1)
    tile_valid = (tile_start < ends[-1]).astype(jnp.int32)
    return gates, pos.reshape(N_TOK, 2), row_token, (slab0 + tile_expert).astype(jnp.int32), tile_valid


def _rows(table, idx):
    return table.at[idx].get(mode="promise_in_bounds")


def kernel(x_prompt, x_sample, cache_k, cache_v, state_hgrn, c, c_ctx, w_mod, b_mod, norm_gains,
           w_in, w_out, attn_sink, hg_lb_logits, hg_norm_gain, ffn_w1, ffn_w3, ffn_w2,
           moe_router, moe_w1, moe_w3, moe_w2):
    cvec = jnp.concatenate([c_ctx[None], c, jnp.zeros((3, D_MODEL), F32)], axis=0)
    mod = _modulation(cvec, w_mod, b_mod).reshape(DEPTH, 8, 6, D_MODEL)
    mod = jnp.pad(mod, ((0, 0), (0, 0), (0, 2), (0, 0)))
    group_src = np.array([0] * (N_PROMPT // MOD_GROUP) + list(range(1, 1 + DEC_BATCH)))
    mod = mod[:, group_src]

    lb_cum = jnp.cumsum(jax.nn.softmax(hg_lb_logits.astype(F32), axis=0), axis=0)
    lb = lb_cum - lb_cum[0:1]
    lb_tab = jnp.stack([lb[:, 0], 1.0 - lb[:, 0], jnp.log1p(-lb[:, 0]),
                        lb[:, 1], 1.0 - lb[:, 1], jnp.log1p(-lb[:, 1])], axis=1)

    rope = _rope_tables()
    tables = _level_tables()
    kc = cache_k.transpose(1, 0, 3, 2, 4).astype(BF16)
    vc = cache_v.transpose(1, 0, 3, 2, 4).astype(BF16)
    ffn_w = [ffn_w1, ffn_w3, ffn_w2]
    moe_w = [w.reshape((-1,) + w.shape[2:]) for w in (moe_w1, moe_w3, moe_w2)]

    x, h = _resnorm((x_prompt.reshape(N_PROMPT, D_MODEL), x_sample.reshape(N_SAMPLE, D_MODEL)), [],
                    None, None, 0, norm_gains[0, 0], mod[0], 1, 0)
    mix = jnp.zeros((N_TOK, D_MODEL), BF16)
    kbuf = jnp.zeros((BATCH, DEPTH, SEQ, ATT_KV_HEADS, HEAD_DIM), F32)
    vbuf = jnp.zeros((BATCH, DEPTH, SEQ, ATT_KV_HEADS, HEAD_DIM), F32)
    sbuf = jnp.zeros((BATCH, DEPTH, 2, HG_HEADS, HG_DIM, HG_DIM), F32)
    for l in range(DEPTH):
        i = l // 2
        moe = l % 2 == 1
        p = _dense(h, [w_in], l, F32, "proj_in")
        mix, kbuf, vbuf = _attn_ctx(p, attn_sink[l], l, mix, kbuf, vbuf)
        mix = _attn_lat(p, attn_sink[l], kc, vc, l, rope, mix)
        mix, sbuf = _hgrn(p, lb_tab, hg_norm_gain[l], tables, SEQ, BATCH, 4, 0, l, mix, sbuf=sbuf)
        (mix,) = _hgrn(p, lb_tab, hg_norm_gain[l], tables, DEC_SEQ, DEC_BATCH, 1, N_PROMPT // DEC_SEQ,
                       l, mix, init=state_hgrn)
        m = _dense(mix, [w_out], l, F32, "proj_out")
        if not moe:
            x, h = _resnorm(x, [m], norm_gains[l, 1], mod[l], 2, norm_gains[l, 2], mod[l], 4, 3)
            act = _dense(h, ffn_w[:2], i, BF16, "ffn_up")
            f = [_dense(act, ffn_w[2:], i, F32, "ffn_down")]
            gates = None
        else:
            router = jnp.pad(moe_router[i], ((0, 0), (0, HEAD_DIM - N_EXPERTS))).astype(BF16)
            x, h_wide, table = _resnorm(x, [m], norm_gains[l, 1], mod[l], 2, norm_gains[l, 2], mod[l], 4, 3,
                                        router=router)
            gates, pos, row_token, tile_slab, tile_valid = _route(table, i * N_EXPERTS)
            xs = _rows(h_wide, row_token)
            act = _gmm(xs, moe_w[:2], tile_slab, tile_valid, BF16, "moe_up")
            ys = _gmm(act, moe_w[2:], tile_slab, tile_valid, F32, "moe_down")
            f = [_rows(ys, pos.T.reshape(-1)).reshape(2, N_TOK, D_MODEL)]
        if l + 1 < DEPTH:
            x, h = _resnorm(x, f, norm_gains[l, 3], mod[l], 5, norm_gains[l + 1, 0], mod[l + 1], 1, 0,
                            gates=gates)
        else:
            y_prompt, y_latent = _resnorm(x, f, norm_gains[l, 3], mod[l], 5, None, None, 0, 0, gates=gates,
                                          split=True)
    return (y_prompt.reshape(BATCH, SEQ, D_MODEL), y_latent.reshape(DEC_BATCH, DEC_SEQ, D_MODEL),
            kbuf, vbuf, sbuf)
```
